```python
import jax
import jax.numpy as jnp
from jax import lax
import numpy as np

D_MODEL = 1024
BATCH = 16
SEQ = 256
DEPTH = 4
DEC_BATCH = 2
DEC_SEQ = 2048
PAST_LEN = 256

GRID_W = 64
EPS = 1e-6
HG_HEADS = 8
HG_DK = 64
HG_DV = 64
HG_CHUNK = 32
FORGET_FLOOR = 1e-6
MLA_HEADS = 8
MLA_Q_LORA = 384
MLA_KV_LORA = 256
MLA_NOPE = 64
MLA_ROPE = 32
MLA_V = 64
MLA_SCALE = (MLA_NOPE + MLA_ROPE) ** -0.5
ROPE_AXIS_DIM = MLA_ROPE // 2
ROPE_BASE = 10000.0
Q_BLOCK = 128
MIX_WIDTH = HG_HEADS * HG_DV + MLA_HEADS * MLA_V
IN_WIDTHS = (HG_HEADS * HG_DK, HG_HEADS * HG_DK, HG_HEADS * HG_DK, HG_HEADS * HG_DV, HG_HEADS * HG_DV, MLA_Q_LORA, MLA_KV_LORA, MLA_ROPE)
IN_WIDTH = 3 * HG_HEADS * HG_DK + 2 * HG_HEADS * HG_DV + MLA_Q_LORA + MLA_KV_LORA + MLA_ROPE
PEER_HEADS = 8
PEER_NKEYS = 128
PEER_EXPERTS = PEER_NKEYS * PEER_NKEYS
PEER_DQ = 256
PEER_TOPK = 16
TOK_BLOCK = 128

kernel_name = 'hymba_hgrn2_mla_peer_diffusion_step'


def rmsnorm(x, g):
    xf = x.astype(jnp.float32)
    y = xf * lax.rsqrt(jnp.mean(xf * xf, axis=-1, keepdims=True) + EPS)
    return (y * g.astype(jnp.float32)).astype(x.dtype)


def hgrn_lower_bounds(logits):
    p = jax.nn.softmax(logits.astype(jnp.float32), axis=1)
    return jnp.cumsum(p, axis=1) - p[:, :1]


def log_forget(z, lb):
    B, L, _ = z.shape
    zf = z.astype(jnp.float32).reshape(B, L, HG_HEADS, HG_DK)
    lb = lb.reshape(HG_HEADS, HG_DK)
    f = lb + (1.0 - lb) * jax.nn.sigmoid(zf)
    return jnp.log(jnp.maximum(f, FORGET_FLOOR))


def chunk_gla(q, logf, v, S0):
    B, L, H, dk = q.shape
    dv = v.shape[-1]
    n_chunks = L // HG_CHUNK
    k = -jnp.expm1(logf)

    def to_chunks(t):
        return jnp.moveaxis(t.astype(jnp.float32).reshape(B, n_chunks, HG_CHUNK, H, t.shape[-1]), 1, 0)

    mask = jnp.tril(jnp.ones((HG_CHUNK, HG_CHUNK), dtype=bool))[None, :, :, None, None]

    def step(S, inp):
        qc, gc, kc, vc = inp
        b = jnp.cumsum(gc, axis=1)
        o_inter = jnp.einsum('bthk,bhkv->bthv', qc * jnp.exp(b), S)
        diff = b[:, :, None] - b[:, None, :]
        decay = jnp.where(mask, jnp.exp(jnp.where(mask, diff, 0.0)), 0.0)
        A = jnp.einsum('bthk,btshk,bshk->bhts', qc, decay, kc)
        o_intra = jnp.einsum('bhts,bshv->bthv', A, vc)
        b_end = b[:, -1]
        S_new = jnp.exp(b_end)[..., None] * S + jnp.einsum('bshk,bshv->bhkv', kc * jnp.exp(b_end[:, None] - b), vc)
        return S_new, o_inter + o_intra

    S_final, o = lax.scan(step, S0.astype(jnp.float32), (to_chunks(q), to_chunks(logf), to_chunks(k), to_chunks(v)))
    return jnp.moveaxis(o, 0, 1).reshape(B, L, H, dv), S_final


def axial_rope_tables(L):
    rows = L // GRID_W
    row = jnp.broadcast_to(jnp.arange(rows)[:, None], (rows, GRID_W)).reshape(L).astype(jnp.float32)
    col = jnp.broadcast_to(jnp.arange(GRID_W)[None, :], (rows, GRID_W)).reshape(L).astype(jnp.float32)
    inv = 1.0 / (ROPE_BASE ** (jnp.arange(0, ROPE_AXIS_DIM, 2, dtype=jnp.float32) / ROPE_AXIS_DIM))
    ang = jnp.concatenate([row[:, None] * inv, col[:, None] * inv], axis=-1)
    return jnp.cos(ang), jnp.sin(ang)


def apply_axial_rope(x, cos, sin):
    half = ROPE_AXIS_DIM // 2
    xr = x.astype(jnp.float32).reshape(x.shape[:-1] + (2, 2, half))
    x1, x2 = xr[..., 0, :], xr[..., 1, :]
    c = cos.reshape(cos.shape[:-1] + (2, half))
    s = sin.reshape(sin.shape[:-1] + (2, half))
    out = jnp.stack([x1 * c - x2 * s, x1 * s + x2 * c], axis=-2)
    return out.reshape(x.shape).astype(x.dtype)


def attend(q_nope, q_pe, k_nope, k_pe, v):
    B, Lq, H, _ = q_nope.shape
    nb = Lq // Q_BLOCK

    def to_blocks(t):
        return jnp.moveaxis(t.reshape(B, nb, Q_BLOCK, H, t.shape[-1]), 1, 0)

    def block(args):
        qn, qp = args
        s = jnp.einsum('bqhd,bkhd->bhqk', qn, k_nope) + jnp.einsum('bqhr,bkr->bhqk', qp, k_pe)
        p = jax.nn.softmax(s.astype(jnp.float32) * MLA_SCALE, axis=-1)
        return jnp.einsum('bhqk,bkhd->bqhd', p.astype(v.dtype), v)

    o = lax.map(block, (to_blocks(q_nope), to_blocks(q_pe)))
    return jnp.moveaxis(o, 0, 1).reshape(B, Lq, H * v.shape[-1])


def mla_kv_up(ckv, w_kvb):
    B, L, _ = ckv.shape
    kv = (ckv @ w_kvb).reshape(B, L, MLA_HEADS, MLA_NOPE + MLA_V)
    return kv[..., :MLA_NOPE], kv[..., MLA_NOPE:]


def peer(h, w_q, sub_keys, u, v):
    B, L, D = h.shape
    tokens = h.reshape(B * L // TOK_BLOCK, TOK_BLOCK, D)
    k1 = sub_keys[0].astype(jnp.float32)
    k2 = sub_keys[1].astype(jnp.float32)

    def block(xb):
        q = (xb @ w_q).astype(jnp.float32).reshape(TOK_BLOCK, PEER_HEADS, 2, PEER_DQ // 2)
        s1 = jnp.einsum('thd,hkd->thk', q[:, :, 0], k1)
        s2 = jnp.einsum('thd,hkd->thk', q[:, :, 1], k2)
        v1, i1 = lax.top_k(s1, PEER_TOPK)
        v2, i2 = lax.top_k(s2, PEER_TOPK)
        cand_s = (v1[..., :, None] + v2[..., None, :]).reshape(TOK_BLOCK, PEER_HEADS, PEER_TOPK * PEER_TOPK)
        cand_i = (i1[..., :, None] * PEER_NKEYS + i2[..., None, :]).reshape(TOK_BLOCK, PEER_HEADS, PEER_TOPK * PEER_TOPK)
        best_s, best_pos = lax.top_k(cand_s, PEER_TOPK)
        idx = jnp.take_along_axis(cand_i, best_pos, axis=-1)
        gate = jax.nn.softmax(best_s, axis=-1)
        ue = u[idx]
        act = jax.nn.gelu(jnp.einsum('td,thkd->thk', xb, ue).astype(jnp.float32), approximate=False)
        ve = v[idx]
        return jnp.einsum('thk,thkd->td', (gate * act).astype(xb.dtype), ve)

    out = lax.map(block, tokens)
    return out.reshape(B, L, D)


def trunk_layer(x, cond, l, P, lb, ctx):
    B, L, _ = x.shape
    mod = jax.nn.silu(cond) @ P['w_ada'][l] + P['b_ada'][l]
    mod = mod[None, None] if cond.ndim == 1 else mod[:, None]
    shift1, scale1, gate1, shift2, scale2, gate2 = jnp.split(mod, 6, axis=-1)

    h = rmsnorm(x, P['norm_mix'][l]) * (1 + scale1) + shift1
    proj = h @ P['w_in'][l]
    split_points = [int(s) for s in np.cumsum(np.array(IN_WIDTHS))[:-1]]
    hq, hff, hfb, hi, hgt, cq, ckv_raw, kpe = jnp.split(proj, split_points, axis=-1)

    q = hq.reshape(B, L, HG_HEADS, HG_DK) * (HG_DK ** -0.5)
    logf_f = log_forget(hff, lb[0, l])
    logf_b = log_forget(hfb, lb[1, l])
    v_in = hi.reshape(B, L, HG_HEADS, HG_DV)
    if ctx is None:
        S0_f = jnp.zeros((B, HG_HEADS, HG_DK, HG_DV), jnp.float32)
        S0_b = S0_f
    else:
        S0_f = ctx[2][:, 0]
        S0_b = ctx[2][:, 1]
    o_f, S_f = chunk_gla(q, logf_f, v_in, S0_f)
    o_b, S_b = chunk_gla(q[:, ::-1], logf_b[:, ::-1], v_in[:, ::-1], S0_b)
    o = rmsnorm(o_f + o_b[:, ::-1], P['hgrn_norm'][l]) * jax.nn.silu(hgt.reshape(B, L, HG_HEADS, HG_DV).astype(jnp.float32))
    hg_out = o.reshape(B, L, HG_HEADS * HG_DV).astype(x.dtype)

    cq = rmsnorm(cq, P['mla_q_norm'][l])
    qf = (cq @ P['mla_wqb'][l]).reshape(B, L, MLA_HEADS, MLA_NOPE + MLA_ROPE)
    q_nope, q_pe = qf[..., :MLA_NOPE], qf[..., MLA_NOPE:]
    ckv = rmsnorm(ckv_raw, P['mla_kv_norm'][l])
    k_nope, v_att = mla_kv_up(ckv, P['mla_wkvb'][l])
    if ctx is None:
        attn = attend(q_nope, q_pe, k_nope, kpe, v_att)
        new_ctx = (ckv, kpe, jnp.stack([S_f, S_b], axis=1))
    else:
        cos, sin = axial_rope_tables(L)
        q_pe = apply_axial_rope(q_pe, cos[:, None], sin[:, None])
        kpe_lat = apply_axial_rope(kpe, cos, sin)
        kn_c, v_c = mla_kv_up(ctx[0], P['mla_wkvb'][l])
        attn = attend(q_nope, q_pe,
                      jnp.concatenate([kn_c.astype(k_nope.dtype), k_nope], axis=1),
                      jnp.concatenate([ctx[1].astype(kpe_lat.dtype), kpe_lat], axis=1),
                      jnp.concatenate([v_c.astype(v_att.dtype), v_att], axis=1))
        new_ctx = None

    mix = jnp.concatenate([hg_out, attn.astype(x.dtype)], axis=-1) @ P['w_out'][l]
    x = x + gate1 * mix

    h2 = rmsnorm(x, P['norm_ffn'][l]) * (1 + scale2) + shift2
    x = x + gate2 * peer(h2, P['peer_wq'][l], P['peer_keys'][l], P['peer_u'][l], P['peer_v'][l])
    return x, new_ctx


def setup_inputs(seed: int = 0) -> dict:
    key = jax.random.key(seed)
    ks = jax.random.split(key, 24)
    nrm = jax.random.normal
    D = D_MODEL
    f32 = jnp.float32
    return {
        'x_prompt': nrm(ks[0], (BATCH, SEQ, D), f32),
        'x_sample': nrm(ks[1], (DEC_BATCH, DEC_SEQ, D), f32),
        'cache_ckv': nrm(ks[2], (DEC_BATCH, DEPTH, PAST_LEN, MLA_KV_LORA), f32),
        'cache_kpe': nrm(ks[3], (DEC_BATCH, DEPTH, PAST_LEN, MLA_ROPE), f32),
        'state_hgrn': 0.5 * nrm(ks[4], (DEC_BATCH, DEPTH, 2, HG_HEADS, HG_DK, HG_DV), f32),
        'c': nrm(ks[5], (DEC_BATCH, D), f32),
        'c_ctx': nrm(ks[6], (D,), f32),
        'w_ada': 0.5 * D ** -0.5 * nrm(ks[7], (DEPTH, D, 6 * D), f32),
        'b_ada': 0.01 * nrm(ks[8], (DEPTH, 6 * D), f32),
        'norm_mix': 1.0 + 0.02 * nrm(ks[9], (DEPTH, D), f32),
        'w_in': D ** -0.5 * nrm(ks[10], (DEPTH, D, IN_WIDTH), f32),
        'hgrn_lb_logits': 0.5 * nrm(ks[11], (2, DEPTH, HG_HEADS * HG_DK), f32),
        'hgrn_norm': 1.0 + 0.02 * nrm(ks[12], (DEPTH, HG_DV), f32),
        'mla_q_norm': 1.0 + 0.02 * nrm(ks[13], (DEPTH, MLA_Q_LORA), f32),
        'mla_wqb': MLA_Q_LORA ** -0.5 * nrm(ks[14], (DEPTH, MLA_Q_LORA, MLA_HEADS * (MLA_NOPE + MLA_ROPE)), f32),
        'mla_kv_norm': 1.0 + 0.02 * nrm(ks[15], (DEPTH, MLA_KV_LORA), f32),
        'mla_wkvb': MLA_KV_LORA ** -0.5 * nrm(ks[16], (DEPTH, MLA_KV_LORA, MLA_HEADS * (MLA_NOPE + MLA_V)), f32),
        'w_out': MIX_WIDTH ** -0.5 * nrm(ks[17], (DEPTH, MIX_WIDTH, D), f32),
        'norm_ffn': 1.0 + 0.02 * nrm(ks[18], (DEPTH, D), f32),
        'peer_wq': D ** -0.5 * nrm(ks[19], (DEPTH, D, PEER_HEADS * PEER_DQ), f32),
        'peer_keys': (PEER_DQ // 2) ** -0.5 * nrm(ks[20], (DEPTH, 2, PEER_HEADS, PEER_NKEYS, PEER_DQ // 2), f32),
        'peer_u': D ** -0.5 * nrm(ks[21], (DEPTH, PEER_EXPERTS, D), f32),
        'peer_v': 0.5 * nrm(ks[22], (DEPTH, PEER_EXPERTS, D), f32),
        'final_norm': 1.0 + 0.02 * nrm(ks[23], (D,), f32),
    }


def reference(x_prompt, x_sample, cache_ckv, cache_kpe, state_hgrn, c, c_ctx,
              w_ada, b_ada, norm_mix, w_in, hgrn_lb_logits, hgrn_norm,
              mla_q_norm, mla_wqb, mla_kv_norm, mla_wkvb, w_out, norm_ffn,
              peer_wq, peer_keys, peer_u, peer_v, final_norm):
    P = {
        'w_ada': w_ada, 'b_ada': b_ada, 'norm_mix': norm_mix, 'w_in': w_in,
        'hgrn_norm': hgrn_norm, 'mla_q_norm': mla_q_norm, 'mla_wqb': mla_wqb,
        'mla_kv_norm': mla_kv_norm, 'mla_wkvb': mla_wkvb, 'w_out': w_out,
        'norm_ffn': norm_ffn, 'peer_wq': peer_wq, 'peer_keys': peer_keys,
        'peer_u': peer_u, 'peer_v': peer_v,
    }
    lb = hgrn_lower_bounds(hgrn_lb_logits)

    xp = x_prompt
    ckvs, kpes, states = [], [], []
    for l in range(DEPTH):
        xp, (ckv_l, kpe_l, st_l) = trunk_layer(xp, c_ctx, l, P, lb, None)
        ckvs.append(ckv_l)
        kpes.append(kpe_l)
        states.append(st_l)

    xs = x_sample
    for l in range(DEPTH):
        xs, _ = trunk_layer(xs, c, l, P, lb, (cache_ckv[:, l], cache_kpe[:, l], state_hgrn[:, l]))

    y_prompt = rmsnorm(xp, final_norm)
    y_sample = rmsnorm(xs, final_norm)
    new_cache_ckv = jnp.stack(ckvs, axis=1)
    new_cache_kpe = jnp.stack(kpes, axis=1)
    new_state_hgrn = jnp.stack(states, axis=1).astype(x_prompt.dtype)
    return (y_prompt, y_sample, new_cache_ckv, new_cache_kpe, new_state_hgrn)
```

```python
import functools
import math

import numpy as np
import jax
import jax.numpy as jnp
from jax import lax
from jax.experimental import pallas as pl
from jax.experimental.pallas import tpu as pltpu

F32 = jnp.float32
BF16 = jnp.bfloat16
HIGHEST = lax.Precision.HIGHEST

D_MODEL = 1024
EPS = 1e-6
GRID_W = 64
HG_HEADS = 8
HG_D = 64
HG_CHUNK = 32
HG_W = HG_HEADS * HG_D
FORGET_FLOOR = 1e-6
MLA_HEADS = 8
MLA_Q_LORA = 384
MLA_KV_LORA = 256
MLA_NOPE = 64
MLA_ROPE = 32
MLA_V = 64
MLA_SCALE = (MLA_NOPE + MLA_ROPE) ** -0.5
ROPE_AXIS_DIM = MLA_ROPE // 2
ROPE_BASE = 10000.0
PEER_HEADS = 8
PEER_NKEYS = 128
PEER_DQ = 256
PEER_TOPK = 16

LANES = 128
VMEM_LIMIT = 48 * 1024 * 1024

IN_HG = 5 * HG_W
IN_CQ0 = IN_HG
IN_CKV0 = IN_CQ0 + MLA_Q_LORA
IN_KPE0 = IN_CKV0 + MLA_KV_LORA
IN_PAD = IN_KPE0 + LANES


def _cparams(sem):
    return pltpu.CompilerParams(dimension_semantics=sem, vmem_limit_bytes=VMEM_LIMIT)


def _split_bf16(a):
    hi = a.astype(BF16)
    lo = (a - hi.astype(F32)).astype(BF16)
    return hi, lo


def _dot3(a, b, dims=(((1,), (0,)), ((), ()))):
    ah, al = _split_bf16(a)
    bh, bl = _split_bf16(b)
    d = functools.partial(lax.dot_general, dimension_numbers=dims, preferred_element_type=F32)
    return d(ah, bh) + d(al, bh) + d(ah, bl)


def _dot_sel(a, sel_bf):
    ah, al = _split_bf16(a)
    return (jnp.dot(ah, sel_bf, preferred_element_type=F32)
            + jnp.dot(al, sel_bf, preferred_element_type=F32))


def _rms_rows(x, g):
    return x * lax.rsqrt(jnp.mean(x * x, axis=-1, keepdims=True) + EPS) * g


def _ada_kernel(cond_ref, w_ref, b_ref, o_ref):
    cnd = cond_ref[...]
    s = cnd * jax.nn.sigmoid(cnd)
    o_ref[0] = jnp.dot(s, w_ref[0], precision=HIGHEST, preferred_element_type=F32) + b_ref[0]


def _ada_mod(cond8, w_ada, b_ada):
    depth, d, n6 = w_ada.shape
    tn = 1536
    return pl.pallas_call(
        _ada_kernel,
        grid=(depth, n6 // tn),
        in_specs=[pl.BlockSpec((8, d), lambda l, j: (0, 0)),
                  pl.BlockSpec((1, d, tn), lambda l, j: (l, 0, j)),
                  pl.BlockSpec((1, 1, tn), lambda l, j: (l, 0, j))],
        out_specs=pl.BlockSpec((1, 8, tn), lambda l, j: (l, 0, j)),
        out_shape=jax.ShapeDtypeStruct((depth, 8, n6), F32),
        compiler_params=_cparams(("parallel", "parallel")),
        name="ada_mod",
    )(cond8, w_ada, b_ada.reshape(depth, 1, n6))


def _inproj_kernel(x_ref, g_ref, mod_ref, w_ref, gq_ref, gkv_ref, hg_ref, cq_ref, ckv_ref, kpe_ref):
    x = x_ref[...]
    shift = mod_ref[0, :, 0:D_MODEL]
    scale = mod_ref[0, :, D_MODEL:2 * D_MODEL]
    h = (_rms_rows(x, g_ref[...]) * (1.0 + scale) + shift).astype(BF16)
    hg_ref[...] = jnp.dot(h, w_ref[:, 0:IN_HG], preferred_element_type=F32)
    cq = jnp.dot(h, w_ref[:, IN_CQ0:IN_CKV0], preferred_element_type=F32)
    cq_ref[...] = _rms_rows(cq, gq_ref[...]).astype(BF16)
    ckv = jnp.dot(h, w_ref[:, IN_CKV0:IN_KPE0], preferred_element_type=F32)
    ckv_ref[...] = _rms_rows(ckv, gkv_ref[...])
    kpe_ref[...] = jnp.dot(h, w_ref[:, IN_KPE0:IN_PAD], preferred_element_type=F32)


def _inproj(x, g, mod, w_pad, gq, gkv, seq_len, tm):
    n = x.shape[0]
    per_seq = seq_len // tm
    n_cond = mod.shape[0]
    cond_of = (lambda i: (0, 0, 0)) if n_cond == 1 else (lambda i: (i // per_seq, 0, 0))
    row = lambda i: (i, 0)
    const = lambda i: (0, 0)
    return pl.pallas_call(
        _inproj_kernel,
        grid=(n // tm,),
        in_specs=[pl.BlockSpec((tm, D_MODEL), row),
                  pl.BlockSpec((1, D_MODEL), const),
                  pl.BlockSpec((1, 1, 6 * D_MODEL), cond_of),
                  pl.BlockSpec((D_MODEL, IN_PAD), const),
                  pl.BlockSpec((1, MLA_Q_LORA), const),
                  pl.BlockSpec((1, MLA_KV_LORA), const)],
        out_specs=[pl.BlockSpec((tm, IN_HG), row),
                   pl.BlockSpec((tm, MLA_Q_LORA), row),
                   pl.BlockSpec((tm, MLA_KV_LORA), row),
                   pl.BlockSpec((tm, LANES), row)],
        out_shape=[jax.ShapeDtypeStruct((n, IN_HG), F32),
                   jax.ShapeDtypeStruct((n, MLA_Q_LORA), BF16),
                   jax.ShapeDtypeStruct((n, MLA_KV_LORA), F32),
                   jax.ShapeDtypeStruct((n, LANES), F32)],
        compiler_params=_cparams(("parallel",)),
        name="in_proj",
    )(x, g, mod, w_pad, gq, gkv)


def _hgrn_kernel(q_ref, ff_ref, fb_ref, v_ref, gt_ref, lbf_ref, lbb_ref, gn_ref, s0_ref,
                 out_ref, sfin_ref, acc_ref, *, seq_len):
    C = HG_CHUNK
    n_chunks = seq_len // C
    r2 = lax.broadcasted_iota(jnp.int32, (LANES, LANES), 0) // HG_D
    c2 = lax.broadcasted_iota(jnp.int32, (LANES, LANES), 1) // HG_D
    same_head = r2 == c2
    sel_bf = same_head.astype(BF16)
    ti = lax.broadcasted_iota(jnp.int32, (C, C), 0)
    ui = lax.broadcasted_iota(jnp.int32, (C, C), 1)
    tri_lo = (ui <= ti).astype(F32)
    tri_up = (ui >= ti).astype(F32)
    s3 = lax.broadcasted_iota(jnp.int32, (C, C, 1), 0)
    t3 = lax.broadcasted_iota(jnp.int32, (C, C, 1), 1)
    ones_c = jnp.ones((C, LANES), F32)

    def block_diag(s_a, s_b):
        z = jnp.zeros((HG_D, HG_D), F32)
        return jnp.concatenate([jnp.concatenate([s_a, z], axis=1),
                                jnp.concatenate([z, s_b], axis=1)], axis=0)

    def chunk_step(c, S, z_ref, lb_ref, forward):
        rows = pl.ds(pl.multiple_of(c * C, C), C)
        lb = lb_ref[...]
        f = jnp.maximum(lb + (1.0 - lb) * jax.nn.sigmoid(z_ref[rows, :]), FORGET_FLOOR)
        g = jnp.log(f)
        k = 1.0 - f
        q = q_ref[rows, :] * (HG_D ** -0.5)
        v = v_ref[rows, :]
        b = jnp.dot(tri_lo if forward else tri_up, g, precision=HIGHEST, preferred_element_type=F32)
        o_inter = _dot3(q * jnp.exp(b), S)
        mask = (t3 >= s3) if forward else (t3 <= s3)
        diff = b[None, :, :] - b[:, None, :]
        e = jnp.where(mask, jnp.exp(jnp.where(mask, diff, 0.0)), 0.0) * q[None, :, :] * k[:, None, :]
        a = _dot_sel(e.reshape(C * C, LANES), sel_bf).reshape(C, C, LANES)
        o_intra = jnp.sum(a * v[:, None, :], axis=0)
        b_end = b[C - 1:C, :] if forward else b[0:1, :]
        kk = k * jnp.exp(b_end - b)
        tdims = (((0,), (0,)), ((), ()))
        total = lax.dot_general(g, ones_c, tdims, precision=HIGHEST, preferred_element_type=F32)
        s_new = jnp.exp(total) * S + jnp.where(same_head, _dot3(kk, v, tdims), 0.0)
        return s_new, o_inter + o_intra

    s_f0 = block_diag(s0_ref[0, 0, 0], s0_ref[0, 0, 1])
    s_b0 = block_diag(s0_ref[0, 1, 0], s0_ref[0, 1, 1])

    def fwd_body(c, S):
        s_new, o = chunk_step(c, S, ff_ref, lbf_ref, True)
        acc_ref[pl.ds(pl.multiple_of(c * C, C), C), :] = o
        return s_new

    s_f = lax.fori_loop(0, n_chunks, fwd_body, s_f0)

    def bwd_body(i, S):
        c = n_chunks - 1 - i
        s_new, o = chunk_step(c, S, fb_ref, lbb_ref, False)
        rows = pl.ds(pl.multiple_of(c * C, C), C)
        acc_ref[rows, :] = acc_ref[rows, :] + o
        return s_new

    s_b = lax.fori_loop(0, n_chunks, bwd_body, s_b0)

    sfin_ref[0, 0, 0] = s_f[0:HG_D, 0:HG_D]
    sfin_ref[0, 0, 1] = s_f[HG_D:, HG_D:]
    sfin_ref[0, 1, 0] = s_b[0:HG_D, 0:HG_D]
    sfin_ref[0, 1, 1] = s_b[HG_D:, HG_D:]

    o = acc_ref[...]
    ms = _dot_sel(o * o, sel_bf) * (1.0 / HG_D)
    gt = gt_ref[...]
    out_ref[...] = (o * lax.rsqrt(ms + EPS) * gn_ref[...] * (gt * jax.nn.sigmoid(gt))).astype(BF16)


def _hgrn(hg, lb_f, lb_b, gn2, s0, batch, seq_len):
    n = hg.shape[0]
    pairs = HG_W // LANES
    part = lambda p: pl.BlockSpec((seq_len, LANES), lambda b, j, p=p: (b, p * pairs + j))
    lane_row = pl.BlockSpec((1, LANES), lambda b, j: (0, j))
    st = pl.BlockSpec((1, 2, 2, HG_D, HG_D), lambda b, j: (b, 0, j, 0, 0))
    return pl.pallas_call(
        functools.partial(_hgrn_kernel, seq_len=seq_len),
        grid=(batch, pairs),
        in_specs=[part(0), part(1), part(2), part(3), part(4), lane_row, lane_row,
                  pl.BlockSpec((1, LANES), lambda b, j: (0, 0)), st],
        out_specs=[pl.BlockSpec((seq_len, LANES), lambda b, j: (b, j)), st],
        out_shape=[jax.ShapeDtypeStruct((n, HG_W), BF16),
                   jax.ShapeDtypeStruct((batch, 2, HG_HEADS, HG_D, HG_D), F32)],
        scratch_shapes=[pltpu.VMEM((seq_len, LANES), F32)],
        compiler_params=_cparams(("parallel", "parallel")),
        name="hgrn2",
    )(hg, hg, hg, hg, hg, lb_f, lb_b, gn2, s0)


def _rope_lanes(x, cos, sin):
    return x * cos + pltpu.roll(x, 96, 1) * sin


def _qproj_kernel(cq_ref, w_ref, cos_ref, sin_ref, q_ref):
    cq = cq_ref[...]
    for h in range(MLA_HEADS):
        qh = jnp.dot(cq, w_ref[h], preferred_element_type=F32)
        q_ref[h] = (_rope_lanes(qh, cos_ref[...], sin_ref[...]) * MLA_SCALE).astype(BF16)


def _kvproj_kernel(ckv_ref, kpe_ref, w_ref, cos_ref, sin_ref, k_ref, kv_ref):
    ckv = ckv_ref[...].astype(BF16)
    kpe = _rope_lanes(kpe_ref[...], cos_ref[...], sin_ref[...])
    nope = lax.broadcasted_iota(jnp.int32, kpe.shape, 1) < MLA_NOPE
    for h in range(MLA_HEADS):
        kvh = jnp.dot(ckv, w_ref[h], preferred_element_type=F32)
        kv_ref[h] = kvh.astype(BF16)
        k_ref[h] = jnp.where(nope, kvh, kpe).astype(BF16)


def _qproj(cqn, wq, cos, sin, tm):
    n = cqn.shape[0]
    pos_blocks = cos.shape[0] // tm
    return pl.pallas_call(
        _qproj_kernel,
        grid=(n // tm,),
        in_specs=[pl.BlockSpec((tm, MLA_Q_LORA), lambda i: (i, 0)),
                  pl.BlockSpec((MLA_HEADS, MLA_Q_LORA, LANES), lambda i: (0, 0, 0)),
                  pl.BlockSpec((tm, LANES), lambda i: (i % pos_blocks, 0)),
                  pl.BlockSpec((tm, LANES), lambda i: (i % pos_blocks, 0))],
        out_specs=pl.BlockSpec((MLA_HEADS, tm, LANES), lambda i: (0, i, 0)),
        out_shape=jax.ShapeDtypeStruct((MLA_HEADS, n, LANES), BF16),
        compiler_params=_cparams(("parallel",)),
        name="mla_q_proj",
    )(cqn, wq, cos, sin)


def _kvproj(ckvn, kpe128, wkv, cos, sin, tm):
    n = ckvn.shape[0]
    pos_blocks = cos.shape[0] // tm
    hm = pl.BlockSpec((MLA_HEADS, tm, LANES), lambda i: (0, i, 0))
    return pl.pallas_call(
        _kvproj_kernel,
        grid=(n // tm,),
        in_specs=[pl.BlockSpec((tm, MLA_KV_LORA), lambda i: (i, 0)),
                  pl.BlockSpec((tm, LANES), lambda i: (i, 0)),
                  pl.BlockSpec((MLA_HEADS, MLA_KV_LORA, LANES), lambda i: (0, 0, 0)),
                  pl.BlockSpec((tm, LANES), lambda i: (i % pos_blocks, 0)),
                  pl.BlockSpec((tm, LANES), lambda i: (i % pos_blocks, 0))],
        out_specs=[hm, hm],
        out_shape=[jax.ShapeDtypeStruct((MLA_HEADS, n, LANES), BF16)] * 2,
        compiler_params=_cparams(("parallel",)),
        name="mla_kv_proj",
    )(ckvn, kpe128, wkv, cos, sin)


def _attn_kernel(q_ref, k_ref, kv_ref, o_ref):
    low = lax.broadcasted_iota(jnp.int32, (q_ref.shape[1], LANES), 1) < MLA_V
    nt = (((1,), (1,)), ((), ()))
    outs = []
    for h in range(MLA_HEADS):
        s = lax.dot_general(q_ref[h], k_ref[h], nt, preferred_element_type=F32)
        p = jnp.exp(s - jnp.max(s, axis=-1, keepdims=True))
        denom = jnp.sum(p, axis=-1, keepdims=True)
        o = jnp.dot(p.astype(BF16), kv_ref[h], preferred_element_type=F32) / denom
        outs.append(o)
    for j in range(MLA_HEADS // 2):
        pair = jnp.where(low, pltpu.roll(outs[2 * j], MLA_V, 1), outs[2 * j + 1])
        o_ref[:, j * LANES:(j + 1) * LANES] = pair.astype(BF16)


def _attention(q128, k128, kv128, batch, lq, lk, tq):
    n = q128.shape[1]
    qb = lq // tq
    kspec = pl.BlockSpec((MLA_HEADS, lk, LANES), lambda b, i: (0, b, 0))
    return pl.pallas_call(
        _attn_kernel,
        grid=(batch, qb),
        in_specs=[pl.BlockSpec((MLA_HEADS, tq, LANES), lambda b, i: (0, b * qb + i, 0)), kspec, kspec],
        out_specs=pl.BlockSpec((tq, MLA_HEADS * MLA_V), lambda b, i: (b * qb + i, 0)),
        out_shape=jax.ShapeDtypeStruct((n, MLA_HEADS * MLA_V), BF16),
        compiler_params=_cparams(("parallel", "parallel")),
        name="mla_attention",
    )(q128, k128, kv128)


def _outproj_kernel(x_ref, hg_ref, at_ref, w_ref, mod_ref, o_ref):
    gate = mod_ref[0, :, 2 * D_MODEL:3 * D_MODEL]
    mix = (jnp.dot(hg_ref[...], w_ref[0:HG_W, :], preferred_element_type=F32)
           + jnp.dot(at_ref[...], w_ref[HG_W:, :], preferred_element_type=F32))
    o_ref[...] = x_ref[...] + gate * mix


def _outproj(x, hg_out, attn, w_out, mod, seq_len, tm):
    n = x.shape[0]
    per_seq = seq_len // tm
    n_cond = mod.shape[0]
    cond_of = (lambda i: (0, 0, 0)) if n_cond == 1 else (lambda i: (i // per_seq, 0, 0))
    row = lambda i: (i, 0)
    return pl.pallas_call(
        _outproj_kernel,
        grid=(n // tm,),
        in_specs=[pl.BlockSpec((tm, D_MODEL), row),
                  pl.BlockSpec((tm, HG_W), row),
                  pl.BlockSpec((tm, MLA_HEADS * MLA_V), row),
                  pl.BlockSpec((2 * HG_W, D_MODEL), lambda i: (0, 0)),
                  pl.BlockSpec((1, 1, 6 * D_MODEL), cond_of)],
        out_specs=pl.BlockSpec((tm, D_MODEL), row),
        out_shape=jax.ShapeDtypeStruct((n, D_MODEL), F32),
        compiler_params=_cparams(("parallel",)),
        name="out_proj",
    )(x, hg_out, attn, w_out, mod)


_CAND = [(a, b) for a in range(PEER_TOPK) for b in range(PEER_TOPK) if (a + 1) * (b + 1) <= PEER_TOPK]
_CAND_PAD = -(-len(_CAND) // 8) * 8


def _topk_rows(work, k):
    rows = lax.broadcasted_iota(jnp.int32, work.shape, 0).astype(F32)
    n_rows = float(work.shape[0])
    vals, idxs = [], []
    for _ in range(k):
        m = jnp.max(work, axis=0, keepdims=True)
        sel = jnp.min(jnp.where(work == m, rows, n_rows), axis=0, keepdims=True)
        vals.append(m)
        idxs.append(sel)
        work = jnp.where(rows == sel, -jnp.inf, work)
    return jnp.concatenate(vals, axis=0), jnp.concatenate(idxs, axis=0)


def _peerq_kernel(x_ref, g_ref, mod_ref, w_ref, k1_ref, k2_ref, h2_ref, idx_ref, gate_ref):
    x = x_ref[...]
    shift = mod_ref[0, :, 3 * D_MODEL:4 * D_MODEL]
    scale = mod_ref[0, :, 4 * D_MODEL:5 * D_MODEL]
    h2 = _rms_rows(x, g_ref[...]) * (1.0 + scale) + shift
    h2_ref[...] = h2
    qp = jnp.dot(h2.astype(BF16), w_ref[...], preferred_element_type=F32)
    nt = (((1,), (1,)), ((), ()))
    half = PEER_DQ // 2
    for h in range(PEER_HEADS):
        q1 = qp[:, h * PEER_DQ:h * PEER_DQ + half]
        q2 = qp[:, h * PEER_DQ + half:(h + 1) * PEER_DQ]
        s1 = _dot3(k1_ref[h], q1, nt)
        s2 = _dot3(k2_ref[h], q2, nt)
        v1, i1 = _topk_rows(s1, PEER_TOPK)
        v2, i2 = _topk_rows(s2, PEER_TOPK)
        cs = [v1[a:a + 1, :] + v2[b:b + 1, :] for a, b in _CAND]
        ci = [i1[a:a + 1, :] * PEER_NKEYS + i2[b:b + 1, :] for a, b in _CAND]
        n_pad = _CAND_PAD - len(_CAND)
        if n_pad:
            cs.append(jnp.full((n_pad, x.shape[0]), -jnp.inf, F32))
            ci.append(jnp.zeros((n_pad, x.shape[0]), F32))
        cand_s = jnp.concatenate(cs, axis=0)
        cand_i = jnp.concatenate(ci, axis=0)
        best_s, best_pos = _topk_rows(cand_s, PEER_TOPK)
        rows = lax.broadcasted_iota(jnp.int32, cand_s.shape, 0).astype(F32)
        picked = [jnp.sum(jnp.where(rows == best_pos[r:r + 1, :], cand_i, 0.0), axis=0, keepdims=True)
                  for r in range(PEER_TOPK)]
        idx_ref[h * PEER_TOPK:(h + 1) * PEER_TOPK, :] = jnp.concatenate(picked, axis=0).astype(jnp.int32)
        ex = jnp.exp(best_s - best_s[0:1, :])
        gate_ref[h * PEER_TOPK:(h + 1) * PEER_TOPK, :] = ex / jnp.sum(ex, axis=0, keepdims=True)


def _peerq(x, g, mod, wq, k1, k2, seq_len, tm):
    n = x.shape[0]
    per_seq = seq_len // tm
    n_cond = mod.shape[0]
    cond_of = (lambda i: (0, 0, 0)) if n_cond == 1 else (lambda i: (i // per_seq, 0, 0))
    kspec = pl.BlockSpec((PEER_HEADS, PEER_NKEYS, PEER_DQ // 2), lambda i: (0, 0, 0))
    tk = PEER_HEADS * PEER_TOPK
    return pl.pallas_call(
        _peerq_kernel,
        grid=(n // tm,),
        in_specs=[pl.BlockSpec((tm, D_MODEL), lambda i: (i, 0)),
                  pl.BlockSpec((1, D_MODEL), lambda i: (0, 0)),
                  pl.BlockSpec((1, 1, 6 * D_MODEL), cond_of),
                  pl.BlockSpec((D_MODEL, PEER_HEADS * PEER_DQ), lambda i: (0, 0)),
                  kspec, kspec],
        out_specs=[pl.BlockSpec((tm, D_MODEL), lambda i: (i, 0)),
                   pl.BlockSpec((tk, tm), lambda i: (0, i)),
                   pl.BlockSpec((tk, tm), lambda i: (0, i))],
        out_shape=[jax.ShapeDtypeStruct((n, D_MODEL), F32),
                   jax.ShapeDtypeStruct((tk, n), jnp.int32),
                   jax.ShapeDtypeStruct((tk, n), F32)],
        compiler_params=_cparams(("parallel",)),
        name="peer_retrieve",
    )(x, g, mod, wq, k1, k2)


PEER_TB = 8
PEER_ROWS = PEER_TB * PEER_HEADS * PEER_TOPK


def _peer_expert_kernel(idx_cur_ref, idx_nxt_ref, x_ref, h2_ref, gate_ref, mod_ref, u_hbm, v_hbm,
                        o_ref, ubuf, vbuf, sem):
    i = pl.program_id(0)
    n_steps = pl.num_programs(0)
    slot = i % 2

    def issue(idx_ref, s):
        def body(r, carry):
            e = idx_ref[r]
            pltpu.make_async_copy(u_hbm.at[pl.ds(e, 1)], ubuf.at[s, pl.ds(r, 1)], sem.at[0, s]).start()
            pltpu.make_async_copy(v_hbm.at[pl.ds(e, 1)], vbuf.at[s, pl.ds(r, 1)], sem.at[1, s]).start()
            return carry
        lax.fori_loop(0, PEER_ROWS, body, 0)

    @pl.when(i == 0)
    def _():
        issue(idx_cur_ref, 0)

    @pl.when(i + 1 < n_steps)
    def _():
        issue(idx_nxt_ref, 1 - slot)

    pltpu.make_async_copy(u_hbm.at[pl.ds(0, PEER_ROWS)], ubuf.at[slot], sem.at[0, slot]).wait()
    pltpu.make_async_copy(v_hbm.at[pl.ds(0, PEER_ROWS)], vbuf.at[slot], sem.at[1, slot]).wait()

    tk = PEER_HEADS * PEER_TOPK
    ys = []
    for t in range(PEER_TB):
        rows = pl.ds(t * tk, tk)
        act = jnp.sum(ubuf[slot, rows, :] * h2_ref[t:t + 1, :], axis=-1, keepdims=True)
        gelu = 0.5 * act * (1.0 + lax.erf(act * (2.0 ** -0.5)))
        ys.append(jnp.sum((gate_ref[t] * gelu) * vbuf[slot, rows, :], axis=0, keepdims=True))
    y = jnp.concatenate(ys, axis=0)
    gate2 = mod_ref[0, :, 5 * D_MODEL:6 * D_MODEL]
    o_ref[...] = x_ref[...] + gate2 * y


def _peer_experts(idx_flat, x, h2, gate3, mod, u, v, seq_len):
    n = x.shape[0]
    n_steps = n // PEER_TB
    per_seq = seq_len // PEER_TB
    n_cond = mod.shape[0]
    cond_of = (lambda i: (0, 0, 0)) if n_cond == 1 else (lambda i: (i // per_seq, 0, 0))
    row = lambda i: (i, 0)
    smem = functools.partial(pl.BlockSpec, memory_space=pltpu.SMEM)
    return pl.pallas_call(
        _peer_expert_kernel,
        grid=(n_steps,),
        in_specs=[smem((PEER_ROWS,), lambda i: (i,)),
                  smem((PEER_ROWS,), lambda i: (jnp.minimum(i + 1, n_steps - 1),)),
                  pl.BlockSpec((PEER_TB, D_MODEL), row),
                  pl.BlockSpec((PEER_TB, D_MODEL), row),
                  pl.BlockSpec((PEER_TB, PEER_HEADS * PEER_TOPK, 1), lambda i: (i, 0, 0)),
                  pl.BlockSpec((1, 1, 6 * D_MODEL), cond_of),
                  pl.BlockSpec(memory_space=pl.ANY),
                  pl.BlockSpec(memory_space=pl.ANY)],
        out_specs=pl.BlockSpec((PEER_TB, D_MODEL), row),
        out_shape=jax.ShapeDtypeStruct((n, D_MODEL), F32),
        scratch_shapes=[pltpu.VMEM((2, PEER_ROWS, D_MODEL), F32),
                        pltpu.VMEM((2, PEER_ROWS, D_MODEL), F32),
                        pltpu.SemaphoreType.DMA((2, 2))],
        compiler_params=_cparams(("arbitrary",)),
        name="peer_experts",
    )(idx_flat, idx_flat, x, h2, gate3, mod, u, v)


def _final_norm_kernel(x_ref, g_ref, o_ref):
    o_ref[...] = _rms_rows(x_ref[...], g_ref[...])


def _final_norm(x, g, tm):
    n = x.shape[0]
    return pl.pallas_call(
        _final_norm_kernel,
        grid=(n // tm,),
        in_specs=[pl.BlockSpec((tm, D_MODEL), lambda i: (i, 0)), pl.BlockSpec((1, D_MODEL), lambda i: (0, 0))],
        out_specs=pl.BlockSpec((tm, D_MODEL), lambda i: (i, 0)),
        out_shape=jax.ShapeDtypeStruct((n, D_MODEL), F32),
        compiler_params=_cparams(("parallel",)),
        name="final_norm",
    )(x, g)


def _rot_half_columns(w):
    half = ROPE_AXIS_DIM // 2
    w4 = w.reshape(w.shape[:-1] + (2, 2, half))
    return jnp.stack([-w4[..., 1, :], w4[..., 0, :]], axis=-2).reshape(w.shape)


def _rope_tables(seq_len, rotate):
    cos = np.zeros((seq_len, LANES), np.float32)
    sin = np.zeros((seq_len, LANES), np.float32)
    cos[:, :MLA_NOPE + MLA_ROPE] = 1.0
    if rotate:
        pos = np.arange(seq_len)
        half = ROPE_AXIS_DIM // 2
        inv = (1.0 / (ROPE_BASE ** (np.arange(0, ROPE_AXIS_DIM, 2, dtype=np.float32) / ROPE_AXIS_DIM))).astype(np.float32)
        for axis, p in enumerate(((pos // GRID_W).astype(np.float32), (pos % GRID_W).astype(np.float32))):
            ang = p[:, None] * inv[None, :]
            for part in range(2):
                lo = MLA_NOPE + axis * ROPE_AXIS_DIM + part * half
                cos[:, lo:lo + half] = np.cos(ang)
                sin[:, lo:lo + half] = np.sin(ang)
    return jnp.asarray(cos), jnp.asarray(sin)


def _prep_layer_weights(l, w_in, mla_wqb, mla_wkvb, w_out, peer_wq):
    wi = w_in[l]
    kpe_w = wi[:, IN_KPE0:IN_KPE0 + MLA_ROPE]
    w_pad = jnp.concatenate(
        [wi[:, :IN_KPE0], jnp.zeros((D_MODEL, MLA_NOPE), F32), kpe_w, _rot_half_columns(kpe_w)], axis=1).astype(BF16)
    wq = mla_wqb[l].reshape(MLA_Q_LORA, MLA_HEADS, MLA_NOPE + MLA_ROPE)
    wq_pe = wq[..., MLA_NOPE:]
    wq128 = jnp.concatenate([wq, _rot_half_columns(wq_pe)], axis=-1).transpose(1, 0, 2).astype(BF16)
    wkv128 = mla_wkvb[l].reshape(MLA_KV_LORA, MLA_HEADS, MLA_NOPE + MLA_V).transpose(1, 0, 2).astype(BF16)
    return w_pad, wq128, wkv128, w_out[l].astype(BF16), peer_wq[l].astype(BF16)


def _trunk_layer(x, mod, l, W, P, lb, batch, seq_len, ctx, rope):
    w_pad, wq128, wkv128, wo, wpq = W
    tm = 256
    hg, cqn, ckvn, kpe128 = _inproj(x, P['norm_mix'][l][None], mod, w_pad, P['mla_q_norm'][l][None],
                                    P['mla_kv_norm'][l][None], seq_len, tm)
    if ctx is None:
        s0 = jnp.zeros((batch, 2, HG_HEADS, HG_D, HG_D), F32)
    else:
        s0 = ctx[2]
    gn2 = jnp.tile(P['hgrn_norm'][l], 2)[None]
    hg_out, s_fin = _hgrn(hg, lb[0, l][None], lb[1, l][None], gn2, s0, batch, seq_len)

    cos, sin = rope
    q128 = _qproj(cqn, wq128, cos, sin, tm)
    k128, kv128 = _kvproj(ckvn, kpe128, wkv128, cos, sin, tm)
    lk = seq_len
    if ctx is not None:
        past = ctx[0].shape[1]
        cos_c, sin_c = _rope_tables(past, False)
        kpe_c = jnp.pad(ctx[1].reshape(batch * past, MLA_ROPE), ((0, 0), (MLA_NOPE, LANES - MLA_NOPE - MLA_ROPE)))
        k_c, kv_c = _kvproj(ctx[0].reshape(batch * past, MLA_KV_LORA), kpe_c, wkv128, cos_c, sin_c, past)
        join = lambda a, b: jnp.concatenate(
            [a.reshape(MLA_HEADS, batch, past, LANES), b.reshape(MLA_HEADS, batch, seq_len, LANES)],
            axis=2).reshape(MLA_HEADS, batch * (past + seq_len), LANES)
        k128, kv128 = join(k_c, k128), join(kv_c, kv128)
        lk = past + seq_len
    attn = _attention(q128, k128, kv128, batch, seq_len, lk, 256)
    x = _outproj(x, hg_out, attn, wo, mod, seq_len, tm)

    keys = P['peer_keys'][l]
    h2, idx_t, gate_t = _peerq(x, P['norm_ffn'][l][None], mod, wpq, keys[0], keys[1], seq_len, tm)
    n = x.shape[0]
    idx_flat = idx_t.T.reshape(n * PEER_HEADS * PEER_TOPK)
    gate3 = gate_t.T.reshape(n, PEER_HEADS * PEER_TOPK, 1)
    x = _peer_experts(idx_flat, x, h2, gate3, mod, P['peer_u'][l], P['peer_v'][l], seq_len)
    return x, (ckvn, kpe128[:, MLA_NOPE:MLA_NOPE + MLA_ROPE], s_fin)


def kernel(x_prompt, x_sample, cache_ckv, cache_kpe, state_hgrn, c, c_ctx, w_ada, b_ada, norm_mix, w_in,
           hgrn_lb_logits, hgrn_norm, mla_q_norm, mla_wqb, mla_kv_norm, mla_wkvb, w_out, norm_ffn,
           peer_wq, peer_keys, peer_u, peer_v, final_norm):
    P = {'norm_mix': norm_mix, 'hgrn_norm': hgrn_norm, 'mla_q_norm': mla_q_norm, 'mla_kv_norm': mla_kv_norm,
         'norm_ffn': norm_ffn, 'peer_keys': peer_keys, 'peer_u': peer_u, 'peer_v': peer_v}
    depth = w_in.shape[0]
    bp, lp, _ = x_prompt.shape
    bs, ls, _ = x_sample.shape

    p = jax.nn.softmax(hgrn_lb_logits.astype(F32), axis=1)
    lb = jnp.cumsum(p, axis=1) - p[:, :1]

    cond8 = jnp.zeros((8, D_MODEL), F32).at[0].set(c_ctx).at[1:1 + bs].set(c)
    mod_all = _ada_mod(cond8, w_ada, b_ada)

    rope_p = _rope_tables(lp, False)
    rope_s = _rope_tables(ls, True)

    xp = x_prompt.reshape(bp * lp, D_MODEL)
    xs = x_sample.reshape(bs * ls, D_MODEL)
    ckvs, kpes, states = [], [], []
    for l in range(depth):
        W = _prep_layer_weights(l, w_in, mla_wqb, mla_wkvb, w_out, peer_wq)
        mod_p = mod_all[l, 0:1][:, None, :]
        mod_s = mod_all[l, 1:1 + bs][:, None, :]
        xp, (ckv_l, kpe_l, st_l) = _trunk_layer(xp, mod_p, l, W, P, lb, bp, lp, None, rope_p)
        ckvs.append(ckv_l.reshape(bp, lp, MLA_KV_LORA))
        kpes.append(kpe_l.reshape(bp, lp, MLA_ROPE))
        states.append(st_l)
        xs, _ = _trunk_layer(xs, mod_s, l, W, P, lb, bs, ls,
                             (cache_ckv[:, l], cache_kpe[:, l], state_hgrn[:, l]), rope_s)

    y_prompt = _final_norm(xp, final_norm[None], 256).reshape(bp, lp, D_MODEL)
    y_sample = _final_norm(xs, final_norm[None], 256).reshape(bs, ls, D_MODEL)
    return (y_prompt, y_sample, jnp.stack(ckvs, axis=1), jnp.stack(kpes, axis=1), jnp.stack(states, axis=1))
```

```python
import functools
import math

import numpy as np
import jax
import jax.numpy as jnp
from jax import lax
from jax.experimental import pallas as pl
from jax.experimental.pallas import tpu as pltpu
from jax.experimental.pallas import tpu_sc as plsc

F32 = jnp.float32
BF16 = jnp.bfloat16
HIGHEST = lax.Precision.HIGHEST

D_MODEL = 1024
EPS = 1e-6
GRID_W = 64
HG_HEADS = 8
HG_D = 64
HG_CHUNK = 32
HG_W = HG_HEADS * HG_D
FORGET_FLOOR = 1e-6
MLA_HEADS = 8
MLA_Q_LORA = 384
MLA_KV_LORA = 256
MLA_NOPE = 64
MLA_ROPE = 32
MLA_V = 64
MLA_SCALE = (MLA_NOPE + MLA_ROPE) ** -0.5
ROPE_AXIS_DIM = MLA_ROPE // 2
ROPE_BASE = 10000.0
PEER_HEADS = 8
PEER_NKEYS = 128
PEER_DQ = 256
PEER_TOPK = 16

LANES = 128
VMEM_LIMIT = 48 * 1024 * 1024

IN_HG = 5 * HG_W
IN_CQ0 = IN_HG
IN_CKV0 = IN_CQ0 + MLA_Q_LORA
IN_KPE0 = IN_CKV0 + MLA_KV_LORA
IN_PAD = IN_KPE0 + LANES


def _cparams(sem):
    return pltpu.CompilerParams(dimension_semantics=sem, vmem_limit_bytes=VMEM_LIMIT)


def _split_bf16(a):
    hi = a.astype(BF16)
    lo = (a - hi.astype(F32)).astype(BF16)
    return hi, lo


def _dot3(a, b, dims=(((1,), (0,)), ((), ()))):
    ah, al = _split_bf16(a)
    bh, bl = _split_bf16(b)
    d = functools.partial(lax.dot_general, dimension_numbers=dims, preferred_element_type=F32)
    return d(ah, bh) + d(al, bh) + d(ah, bl)


def _dot_sel(a, sel_bf):
    ah, al = _split_bf16(a)
    return (jnp.dot(ah, sel_bf, preferred_element_type=F32)
            + jnp.dot(al, sel_bf, preferred_element_type=F32))


def _rms_rows(x, g):
    return x * lax.rsqrt(jnp.mean(x * x, axis=-1, keepdims=True) + EPS) * g


def _ada_kernel(cond_ref, w_ref, b_ref, o_ref):
    cnd = cond_ref[...]
    s = cnd * jax.nn.sigmoid(cnd)
    o_ref[0] = jnp.dot(s, w_ref[0], precision=HIGHEST, preferred_element_type=F32) + b_ref[0]


def _ada_mod(cond8, w_ada, b_ada):
    depth, d, n6 = w_ada.shape
    tn = 1536
    return pl.pallas_call(
        _ada_kernel,
        grid=(depth, n6 // tn),
        in_specs=[pl.BlockSpec((8, d), lambda l, j: (0, 0)),
                  pl.BlockSpec((1, d, tn), lambda l, j: (l, 0, j)),
                  pl.BlockSpec((1, 1, tn), lambda l, j: (l, 0, j))],
        out_specs=pl.BlockSpec((1, 8, tn), lambda l, j: (l, 0, j)),
        out_shape=jax.ShapeDtypeStruct((depth, 8, n6), F32),
        compiler_params=_cparams(("parallel", "parallel")),
        name="ada_mod",
    )(cond8, w_ada, b_ada.reshape(depth, 1, n6))


def _inproj_kernel(x_ref, g_ref, mod_ref, w_ref, gq_ref, gkv_ref, hg_ref, cq_ref, ckv_ref, kpe_ref):
    x = x_ref[...]
    shift = mod_ref[0, :, 0:D_MODEL]
    scale = mod_ref[0, :, D_MODEL:2 * D_MODEL]
    h = (_rms_rows(x, g_ref[...]) * (1.0 + scale) + shift).astype(BF16)
    hg_ref[...] = jnp.dot(h, w_ref[:, 0:IN_HG], preferred_element_type=F32)
    cq = jnp.dot(h, w_ref[:, IN_CQ0:IN_CKV0], preferred_element_type=F32)
    cq_ref[...] = _rms_rows(cq, gq_ref[...]).astype(BF16)
    ckv = jnp.dot(h, w_ref[:, IN_CKV0:IN_KPE0], preferred_element_type=F32)
    ckv_ref[...] = _rms_rows(ckv, gkv_ref[...])
    kpe_ref[...] = jnp.dot(h, w_ref[:, IN_KPE0:IN_PAD], preferred_element_type=F32)


def _inproj(x, g, mod, w_pad, gq, gkv, seq_len, tm):
    n = x.shape[0]
    per_seq = seq_len // tm
    n_cond = mod.shape[0]
    cond_of = (lambda i: (0, 0, 0)) if n_cond == 1 else (lambda i: (i // per_seq, 0, 0))
    row = lambda i: (i, 0)
    const = lambda i: (0, 0)
    return pl.pallas_call(
        _inproj_kernel,
        grid=(n // tm,),
        in_specs=[pl.BlockSpec((tm, D_MODEL), row),
                  pl.BlockSpec((1, D_MODEL), const),
                  pl.BlockSpec((1, 1, 6 * D_MODEL), cond_of),
                  pl.BlockSpec((D_MODEL, IN_PAD), const),
                  pl.BlockSpec((1, MLA_Q_LORA), const),
                  pl.BlockSpec((1, MLA_KV_LORA), const)],
        out_specs=[pl.BlockSpec((tm, IN_HG), row),
                   pl.BlockSpec((tm, MLA_Q_LORA), row),
                   pl.BlockSpec((tm, MLA_KV_LORA), row),
                   pl.BlockSpec((tm, LANES), row)],
        out_shape=[jax.ShapeDtypeStruct((n, IN_HG), F32),
                   jax.ShapeDtypeStruct((n, MLA_Q_LORA), BF16),
                   jax.ShapeDtypeStruct((n, MLA_KV_LORA), F32),
                   jax.ShapeDtypeStruct((n, LANES), F32)],
        compiler_params=_cparams(("parallel",)),
        name="in_proj",
    )(x, g, mod, w_pad, gq, gkv)


def _hgrn_kernel(q_ref, ff_ref, fb_ref, v_ref, gt_ref, lbf_ref, lbb_ref, gn_ref, s0_ref,
                 out_ref, sfin_ref, acc_ref, *, seq_len):
    C = HG_CHUNK
    n_chunks = seq_len // C
    r2 = lax.broadcasted_iota(jnp.int32, (LANES, LANES), 0) // HG_D
    c2 = lax.broadcasted_iota(jnp.int32, (LANES, LANES), 1) // HG_D
    same_head = r2 == c2
    sel_bf = same_head.astype(BF16)
    ti = lax.broadcasted_iota(jnp.int32, (C, C), 0)
    ui = lax.broadcasted_iota(jnp.int32, (C, C), 1)
    tri_lo = (ui <= ti).astype(F32)
    tri_up = (ui >= ti).astype(F32)
    s3 = lax.broadcasted_iota(jnp.int32, (C, C, 1), 0)
    t3 = lax.broadcasted_iota(jnp.int32, (C, C, 1), 1)
    ones_c = jnp.ones((C, LANES), F32)

    def block_diag(s_a, s_b):
        z = jnp.zeros((HG_D, HG_D), F32)
        return jnp.concatenate([jnp.concatenate([s_a, z], axis=1),
                                jnp.concatenate([z, s_b], axis=1)], axis=0)

    def chunk_step(c, S, z_ref, lb_ref, forward):
        rows = pl.ds(pl.multiple_of(c * C, C), C)
        lb = lb_ref[...]
        f = jnp.maximum(lb + (1.0 - lb) * jax.nn.sigmoid(z_ref[rows, :]), FORGET_FLOOR)
        g = jnp.log(f)
        k = 1.0 - f
        q = q_ref[rows, :] * (HG_D ** -0.5)
        v = v_ref[rows, :]
        b = jnp.dot(tri_lo if forward else tri_up, g, precision=HIGHEST, preferred_element_type=F32)
        o_inter = _dot3(q * jnp.exp(b), S)
        mask = (t3 >= s3) if forward else (t3 <= s3)
        diff = b[None, :, :] - b[:, None, :]
        e = jnp.where(mask, jnp.exp(jnp.where(mask, diff, 0.0)), 0.0) * q[None, :, :] * k[:, None, :]
        a = _dot_sel(e.reshape(C * C, LANES), sel_bf).reshape(C, C, LANES)
        o_intra = jnp.sum(a * v[:, None, :], axis=0)
        b_end = b[C - 1:C, :] if forward else b[0:1, :]
        kk = k * jnp.exp(b_end - b)
        tdims = (((0,), (0,)), ((), ()))
        total = lax.dot_general(g, ones_c, tdims, precision=HIGHEST, preferred_element_type=F32)
        s_new = jnp.exp(total) * S + jnp.where(same_head, _dot3(kk, v, tdims), 0.0)
        return s_new, o_inter + o_intra

    s_f0 = block_diag(s0_ref[0, 0, 0], s0_ref[0, 0, 1])
    s_b0 = block_diag(s0_ref[0, 1, 0], s0_ref[0, 1, 1])

    def fwd_body(c, S):
        s_new, o = chunk_step(c, S, ff_ref, lbf_ref, True)
        acc_ref[pl.ds(pl.multiple_of(c * C, C), C), :] = o
        return s_new

    s_f = lax.fori_loop(0, n_chunks, fwd_body, s_f0)

    def bwd_body(i, S):
        c = n_chunks - 1 - i
        s_new, o = chunk_step(c, S, fb_ref, lbb_ref, False)
        rows = pl.ds(pl.multiple_of(c * C, C), C)
        acc_ref[rows, :] = acc_ref[rows, :] + o
        return s_new

    s_b = lax.fori_loop(0, n_chunks, bwd_body, s_b0)

    sfin_ref[0, 0, 0] = s_f[0:HG_D, 0:HG_D]
    sfin_ref[0, 0, 1] = s_f[HG_D:, HG_D:]
    sfin_ref[0, 1, 0] = s_b[0:HG_D, 0:HG_D]
    sfin_ref[0, 1, 1] = s_b[HG_D:, HG_D:]

    o = acc_ref[...]
    ms = _dot_sel(o * o, sel_bf) * (1.0 / HG_D)
    gt = gt_ref[...]
    out_ref[...] = (o * lax.rsqrt(ms + EPS) * gn_ref[...] * (gt * jax.nn.sigmoid(gt))).astype(BF16)


def _hgrn(hg, lb_f, lb_b, gn2, s0, batch, seq_len):
    n = hg.shape[0]
    pairs = HG_W // LANES
    part = lambda p: pl.BlockSpec((seq_len, LANES), lambda b, j, p=p: (b, p * pairs + j))
    lane_row = pl.BlockSpec((1, LANES), lambda b, j: (0, j))
    st = pl.BlockSpec((1, 2, 2, HG_D, HG_D), lambda b, j: (b, 0, j, 0, 0))
    return pl.pallas_call(
        functools.partial(_hgrn_kernel, seq_len=seq_len),
        grid=(batch, pairs),
        in_specs=[part(0), part(1), part(2), part(3), part(4), lane_row, lane_row,
                  pl.BlockSpec((1, LANES), lambda b, j: (0, 0)), st],
        out_specs=[pl.BlockSpec((seq_len, LANES), lambda b, j: (b, j)), st],
        out_shape=[jax.ShapeDtypeStruct((n, HG_W), BF16),
                   jax.ShapeDtypeStruct((batch, 2, HG_HEADS, HG_D, HG_D), F32)],
        scratch_shapes=[pltpu.VMEM((seq_len, LANES), F32)],
        compiler_params=_cparams(("parallel", "parallel")),
        name="hgrn2",
    )(hg, hg, hg, hg, hg, lb_f, lb_b, gn2, s0)


def _rope_lanes(x, cos, sin):
    return x * cos + pltpu.roll(x, 96, 1) * sin


def _qproj_kernel(cq_ref, w_ref, cos_ref, sin_ref, q_ref):
    cq = cq_ref[...]
    for h in range(MLA_HEADS):
        qh = jnp.dot(cq, w_ref[h], preferred_element_type=F32)
        q_ref[h] = (_rope_lanes(qh, cos_ref[...], sin_ref[...]) * MLA_SCALE).astype(BF16)


def _kvproj_kernel(ckv_ref, kpe_ref, w_ref, cos_ref, sin_ref, k_ref, kv_ref):
    ckv = ckv_ref[...].astype(BF16)
    kpe = _rope_lanes(kpe_ref[...], cos_ref[...], sin_ref[...])
    nope = lax.broadcasted_iota(jnp.int32, kpe.shape, 1) < MLA_NOPE
    for h in range(MLA_HEADS):
        kvh = jnp.dot(ckv, w_ref[h], preferred_element_type=F32)
        kv_ref[h] = kvh.astype(BF16)
        k_ref[h] = jnp.where(nope, kvh, kpe).astype(BF16)


def _qproj(cqn, wq, cos, sin, tm):
    n = cqn.shape[0]
    pos_blocks = cos.shape[0] // tm
    return pl.pallas_call(
        _qproj_kernel,
        grid=(n // tm,),
        in_specs=[pl.BlockSpec((tm, MLA_Q_LORA), lambda i: (i, 0)),
                  pl.BlockSpec((MLA_HEADS, MLA_Q_LORA, LANES), lambda i: (0, 0, 0)),
                  pl.BlockSpec((tm, LANES), lambda i: (i % pos_blocks, 0)),
                  pl.BlockSpec((tm, LANES), lambda i: (i % pos_blocks, 0))],
        out_specs=pl.BlockSpec((MLA_HEADS, tm, LANES), lambda i: (0, i, 0)),
        out_shape=jax.ShapeDtypeStruct((MLA_HEADS, n, LANES), BF16),
        compiler_params=_cparams(("parallel",)),
        name="mla_q_proj",
    )(cqn, wq, cos, sin)


def _kvproj(ckvn, kpe128, wkv, cos, sin, tm):
    n = ckvn.shape[0]
    pos_blocks = cos.shape[0] // tm
    hm = pl.BlockSpec((MLA_HEADS, tm, LANES), lambda i: (0, i, 0))
    return pl.pallas_call(
        _kvproj_kernel,
        grid=(n // tm,),
        in_specs=[pl.BlockSpec((tm, MLA_KV_LORA), lambda i: (i, 0)),
                  pl.BlockSpec((tm, LANES), lambda i: (i, 0)),
                  pl.BlockSpec((MLA_HEADS, MLA_KV_LORA, LANES), lambda i: (0, 0, 0)),
                  pl.BlockSpec((tm, LANES), lambda i: (i % pos_blocks, 0)),
                  pl.BlockSpec((tm, LANES), lambda i: (i % pos_blocks, 0))],
        out_specs=[hm, hm],
        out_shape=[jax.ShapeDtypeStruct((MLA_HEADS, n, LANES), BF16)] * 2,
        compiler_params=_cparams(("parallel",)),
        name="mla_kv_proj",
    )(ckvn, kpe128, wkv, cos, sin)


def _attn_kernel(q_ref, k_ref, kv_ref, o_ref):
    low = lax.broadcasted_iota(jnp.int32, (q_ref.shape[1], LANES), 1) < MLA_V
    nt = (((1,), (1,)), ((), ()))
    outs = []
    for h in range(MLA_HEADS):
        s = lax.dot_general(q_ref[h], k_ref[h], nt, preferred_element_type=F32)
        p = jnp.exp(s - jnp.max(s, axis=-1, keepdims=True))
        denom = jnp.sum(p, axis=-1, keepdims=True)
        o = jnp.dot(p.astype(BF16), kv_ref[h], preferred_element_type=F32) / denom
        outs.append(o)
    for j in range(MLA_HEADS // 2):
        pair = jnp.where(low, pltpu.roll(outs[2 * j], MLA_V, 1), outs[2 * j + 1])
        o_ref[:, j * LANES:(j + 1) * LANES] = pair.astype(BF16)


def _attention(q128, k128, kv128, batch, lq, lk, tq):
    n = q128.shape[1]
    qb = lq // tq
    kspec = pl.BlockSpec((MLA_HEADS, lk, LANES), lambda b, i: (0, b, 0))
    return pl.pallas_call(
        _attn_kernel,
        grid=(batch, qb),
        in_specs=[pl.BlockSpec((MLA_HEADS, tq, LANES), lambda b, i: (0, b * qb + i, 0)), kspec, kspec],
        out_specs=pl.BlockSpec((tq, MLA_HEADS * MLA_V), lambda b, i: (b * qb + i, 0)),
        out_shape=jax.ShapeDtypeStruct((n, MLA_HEADS * MLA_V), BF16),
        compiler_params=_cparams(("parallel", "parallel")),
        name="mla_attention",
    )(q128, k128, kv128)


def _outproj_kernel(x_ref, hg_ref, at_ref, w_ref, mod_ref, o_ref):
    gate = mod_ref[0, :, 2 * D_MODEL:3 * D_MODEL]
    mix = (jnp.dot(hg_ref[...], w_ref[0:HG_W, :], preferred_element_type=F32)
           + jnp.dot(at_ref[...], w_ref[HG_W:, :], preferred_element_type=F32))
    o_ref[...] = x_ref[...] + gate * mix


def _outproj(x, hg_out, attn, w_out, mod, seq_len, tm):
    n = x.shape[0]
    per_seq = seq_len // tm
    n_cond = mod.shape[0]
    cond_of = (lambda i: (0, 0, 0)) if n_cond == 1 else (lambda i: (i // per_seq, 0, 0))
    row = lambda i: (i, 0)
    return pl.pallas_call(
        _outproj_kernel,
        grid=(n // tm,),
        in_specs=[pl.BlockSpec((tm, D_MODEL), row),
                  pl.BlockSpec((tm, HG_W), row),
                  pl.BlockSpec((tm, MLA_HEADS * MLA_V), row),
                  pl.BlockSpec((2 * HG_W, D_MODEL), lambda i: (0, 0)),
                  pl.BlockSpec((1, 1, 6 * D_MODEL), cond_of)],
        out_specs=pl.BlockSpec((tm, D_MODEL), row),
        out_shape=jax.ShapeDtypeStruct((n, D_MODEL), F32),
        compiler_params=_cparams(("parallel",)),
        name="out_proj",
    )(x, hg_out, attn, w_out, mod)


_CAND = [(a, b) for a in range(PEER_TOPK) for b in range(PEER_TOPK) if (a + 1) * (b + 1) <= PEER_TOPK]
_CAND_PAD = -(-len(_CAND) // 8) * 8


def _topk_rows(work, k):
    rows = lax.broadcasted_iota(jnp.int32, work.shape, 0).astype(F32)
    n_rows = float(work.shape[0])
    vals, idxs = [], []
    for _ in range(k):
        m = jnp.max(work, axis=0, keepdims=True)
        sel = jnp.min(jnp.where(work == m, rows, n_rows), axis=0, keepdims=True)
        vals.append(m)
        idxs.append(sel)
        work = jnp.where(rows == sel, -jnp.inf, work)
    return jnp.concatenate(vals, axis=0), jnp.concatenate(idxs, axis=0)


def _peerq_kernel(x_ref, g_ref, mod_ref, w_ref, k1_ref, k2_ref, h2_ref, idx_ref, gate_ref, *, idx_offset):
    idx_rows, gate_rows = [], []
    x = x_ref[...]
    shift = mod_ref[0, :, 3 * D_MODEL:4 * D_MODEL]
    scale = mod_ref[0, :, 4 * D_MODEL:5 * D_MODEL]
    h2 = _rms_rows(x, g_ref[...]) * (1.0 + scale) + shift
    h2_ref[...] = h2
    qp = jnp.dot(h2.astype(BF16), w_ref[...], preferred_element_type=F32)
    nt = (((1,), (1,)), ((), ()))
    half = PEER_DQ // 2
    for h in range(PEER_HEADS):
        q1 = qp[:, h * PEER_DQ:h * PEER_DQ + half]
        q2 = qp[:, h * PEER_DQ + half:(h + 1) * PEER_DQ]
        s1 = _dot3(k1_ref[h], q1, nt)
        s2 = _dot3(k2_ref[h], q2, nt)
        v1, i1 = _topk_rows(s1, PEER_TOPK)
        v2, i2 = _topk_rows(s2, PEER_TOPK)
        cs = [v1[a:a + 1, :] + v2[b:b + 1, :] for a, b in _CAND]
        ci = [i1[a:a + 1, :] * PEER_NKEYS + i2[b:b + 1, :] for a, b in _CAND]
        n_pad = _CAND_PAD - len(_CAND)
        if n_pad:
            cs.append(jnp.full((n_pad, x.shape[0]), -jnp.inf, F32))
            ci.append(jnp.zeros((n_pad, x.shape[0]), F32))
        cand_s = jnp.concatenate(cs, axis=0)
        cand_i = jnp.concatenate(ci, axis=0)
        best_s, best_pos = _topk_rows(cand_s, PEER_TOPK)
        rows = lax.broadcasted_iota(jnp.int32, cand_s.shape, 0).astype(F32)
        picked = [jnp.sum(jnp.where(rows == best_pos[r:r + 1, :], cand_i, 0.0), axis=0, keepdims=True)
                  for r in range(PEER_TOPK)]
        idx_rows.extend(picked)
        ex = jnp.exp(best_s - best_s[0:1, :])
        gate_rows.append(ex / jnp.sum(ex, axis=0, keepdims=True))
    idx_ref[...] = jnp.concatenate(idx_rows, axis=0).T.astype(jnp.int32) + idx_offset
    gate_ref[...] = jnp.concatenate(gate_rows, axis=0).T


def _peerq(x, g, mod, wq, k1, k2, seq_len, tm, idx_offset):
    n = x.shape[0]
    per_seq = seq_len // tm
    n_cond = mod.shape[0]
    cond_of = (lambda i: (0, 0, 0)) if n_cond == 1 else (lambda i: (i // per_seq, 0, 0))
    kspec = pl.BlockSpec((PEER_HEADS, PEER_NKEYS, PEER_DQ // 2), lambda i: (0, 0, 0))
    tk = PEER_HEADS * PEER_TOPK
    return pl.pallas_call(
        functools.partial(_peerq_kernel, idx_offset=idx_offset),
        grid=(n // tm,),
        in_specs=[pl.BlockSpec((tm, D_MODEL), lambda i: (i, 0)),
                  pl.BlockSpec((1, D_MODEL), lambda i: (0, 0)),
                  pl.BlockSpec((1, 1, 6 * D_MODEL), cond_of),
                  pl.BlockSpec((D_MODEL, PEER_HEADS * PEER_DQ), lambda i: (0, 0)),
                  kspec, kspec],
        out_specs=[pl.BlockSpec((tm, D_MODEL), lambda i: (i, 0)),
                   pl.BlockSpec((tm, tk), lambda i: (i, 0)),
                   pl.BlockSpec((tm, tk), lambda i: (i, 0))],
        out_shape=[jax.ShapeDtypeStruct((n, D_MODEL), F32),
                   jax.ShapeDtypeStruct((n, tk), jnp.int32),
                   jax.ShapeDtypeStruct((n, tk), F32)],
        compiler_params=_cparams(("parallel",)),
        name="peer_retrieve",
    )(x, g, mod, wq, k1, k2)


SC_CORES = 2
SC_SUBCORES = 16
SC_LANES = 16
SC_TILES = SC_CORES * SC_SUBCORES
SC_NBUF = 4
PEER_TK = PEER_HEADS * PEER_TOPK
D_VREGS = D_MODEL // SC_LANES
V_COLS = 16


def _sc_mesh():
    return plsc.VectorSubcoreMesh(core_axis_name="c", subcore_axis_name="s")


def _sc_params():
    return pltpu.CompilerParams(needs_layout_passes=False)


def _sc_token_base(tokens_per_tile):
    return (lax.axis_index("s") * SC_CORES + lax.axis_index("c")) * tokens_per_tile


def _peer_u_body(h2_hbm, idx_hbm, u_hbm, act_hbm, idx_v, act_v, x_v, rows_v, sem_x, sem_g, *, tpt):
    base = _sc_token_base(tpt)
    n_items = tpt * PEER_HEADS
    pltpu.sync_copy(idx_hbm.at[pl.ds(base, tpt)], idx_v)

    def gather(i, slot):
        t, h = i // PEER_HEADS, i % PEER_HEADS
        return pltpu.make_async_copy(u_hbm.at[idx_v.at[t, pl.ds(h * PEER_TOPK, PEER_TOPK)]],
                                     rows_v.at[slot], sem_g.at[slot])

    def x_copy(t):
        return pltpu.make_async_copy(h2_hbm.at[base + t], x_v.at[t % 2], sem_x.at[t % 2])

    x_copy(0).start()
    for b in range(SC_NBUF - 1):
        gather(b, b).start()
    lane = lax.iota(jnp.int32, SC_LANES)
    zero = jnp.zeros((SC_LANES,), F32)

    @pl.loop(0, n_items, step=SC_NBUF)
    def _(i0):
        for b in range(SC_NBUF):
            i = i0 + b
            t, h = i // PEER_HEADS, i % PEER_HEADS

            @pl.when(h == 0)
            def _():
                x_copy(t).wait()

                @pl.when(t + 1 < tpt)
                def _():
                    x_copy(t + 1).start()

            @pl.when(i + SC_NBUF - 1 < n_items)
            def _():
                gather(i + SC_NBUF - 1, (b + SC_NBUF - 1) % SC_NBUF).start()

            gather(i, b).wait()
            xs = t % 2

            def cbody(c, accs):
                col = pl.ds(c * SC_LANES, SC_LANES)
                xv = x_v[xs, col]
                return tuple(a + xv * rows_v[b, j, col] for j, a in enumerate(accs))

            accs = plsc.parallel_loop(0, D_VREGS, unroll=2, carry=(zero,) * PEER_TOPK)(cbody)
            vec = zero
            for j in range(PEER_TOPK):
                vec = jnp.where(lane == j, jnp.sum(accs[j]), vec)
            act_v[t, pl.ds(h * PEER_TOPK, PEER_TOPK)] = vec

    pltpu.sync_copy(act_v, act_hbm.at[pl.ds(base, tpt)])


def _peer_v_body(w_hbm, idx_hbm, v_hbm, y_hbm, idx_v, w_v, y_v, rows_v, sem_y, sem_g, *, tpt):
    base = _sc_token_base(tpt)
    n_items = tpt * PEER_HEADS
    pltpu.sync_copy(idx_hbm.at[pl.ds(base, tpt)], idx_v)
    pltpu.sync_copy(w_hbm.at[pl.ds(base, tpt)], w_v)

    def gather(i, slot):
        t, h = i // PEER_HEADS, i % PEER_HEADS
        return pltpu.make_async_copy(v_hbm.at[idx_v.at[t, pl.ds(h * PEER_TOPK, PEER_TOPK)]],
                                     rows_v.at[slot], sem_g.at[slot])

    def y_copy(t):
        return pltpu.make_async_copy(y_v.at[t % 2], y_hbm.at[base + t], sem_y.at[t % 2])

    for b in range(SC_NBUF - 1):
        gather(b, b).start()
    lane = lax.iota(jnp.int32, SC_LANES)
    zero = jnp.zeros((SC_LANES,), F32)

    @pl.loop(0, n_items, step=SC_NBUF)
    def _(i0):
        for b in range(SC_NBUF):
            i = i0 + b
            t, h = i // PEER_HEADS, i % PEER_HEADS
            ys = t % 2

            @pl.when(i + SC_NBUF - 1 < n_items)
            def _():
                gather(i + SC_NBUF - 1, (b + SC_NBUF - 1) % SC_NBUF).start()

            @pl.when(h == 0)
            def _():
                @pl.when(t >= 2)
                def _():
                    y_copy(t - 2).wait()

                @pl.loop(0, D_VREGS)
                def _(c):
                    y_v[ys, pl.ds(c * SC_LANES, SC_LANES)] = zero

            gather(i, b).wait()
            wv = w_v[t, pl.ds(h * PEER_TOPK, PEER_TOPK)]
            ws = [jnp.sum(jnp.where(lane == j, wv, 0.0)) for j in range(PEER_TOPK)]

            @pl.loop(0, D_VREGS // V_COLS)
            def _(cb):
                for c in range(V_COLS):
                    col = pl.ds((cb * V_COLS + c) * SC_LANES, SC_LANES)
                    acc = ws[0] * rows_v[b, 0, col]
                    for j in range(1, PEER_TOPK):
                        acc = acc + ws[j] * rows_v[b, j, col]
                    y_v[ys, col] = y_v[ys, col] + acc

            @pl.when(h == PEER_HEADS - 1)
            def _():
                y_copy(t).start()

    y_copy(tpt - 2).wait()
    y_copy(tpt - 1).wait()


def _sc_scratch(tpt, side_dtype):
    return [pltpu.VMEM((tpt, PEER_TK), jnp.int32),
            pltpu.VMEM((tpt, PEER_TK), side_dtype),
            pltpu.VMEM((2, D_MODEL), F32),
            pltpu.VMEM((SC_NBUF, PEER_TOPK, D_MODEL), F32),
            pltpu.SemaphoreType.DMA((2,)),
            pltpu.SemaphoreType.DMA((SC_NBUF,))]


def _peer_u(h2, idx, u_all):
    n = h2.shape[0]
    tpt = n // SC_TILES
    return pl.kernel(
        functools.partial(_peer_u_body, tpt=tpt),
        out_type=jax.ShapeDtypeStruct((n, PEER_TK), F32),
        mesh=_sc_mesh(),
        scratch_types=_sc_scratch(tpt, F32),
        compiler_params=_sc_params(),
        name="peer_u_sc",
    )(h2, idx, u_all)


def _peer_v(w, idx, v_all):
    n = w.shape[0]
    tpt = n // SC_TILES
    return pl.kernel(
        functools.partial(_peer_v_body, tpt=tpt),
        out_type=jax.ShapeDtypeStruct((n, D_MODEL), F32),
        mesh=_sc_mesh(),
        scratch_types=_sc_scratch(tpt, F32),
        compiler_params=_sc_params(),
        name="peer_v_sc",
    )(w, idx, v_all)


def _peer_gelu_kernel(gate_ref, act_ref, w_ref):
    act = act_ref[...]
    w_ref[...] = gate_ref[...] * (0.5 * act * (1.0 + lax.erf(act * (2.0 ** -0.5))))


def _peer_gelu(gate, act, tm):
    n = gate.shape[0]
    spec = pl.BlockSpec((tm, PEER_TK), lambda i: (i, 0))
    return pl.pallas_call(
        _peer_gelu_kernel,
        grid=(n // tm,),
        in_specs=[spec, spec],
        out_specs=spec,
        out_shape=jax.ShapeDtypeStruct((n, PEER_TK), F32),
        compiler_params=_cparams(("parallel",)),
        name="peer_gelu",
    )(gate, act)


def _peer_residual_kernel(x_ref, y_ref, mod_ref, o_ref):
    o_ref[...] = x_ref[...] + mod_ref[0, :, 5 * D_MODEL:6 * D_MODEL] * y_ref[...]


def _peer_residual(x, y, mod, seq_len, tm):
    n = x.shape[0]
    per_seq = seq_len // tm
    n_cond = mod.shape[0]
    cond_of = (lambda i: (0, 0, 0)) if n_cond == 1 else (lambda i: (i // per_seq, 0, 0))
    row = pl.BlockSpec((tm, D_MODEL), lambda i: (i, 0))
    return pl.pallas_call(
        _peer_residual_kernel,
        grid=(n // tm,),
        in_specs=[row, row, pl.BlockSpec((1, 1, 6 * D_MODEL), cond_of)],
        out_specs=row,
        out_shape=jax.ShapeDtypeStruct((n, D_MODEL), F32),
        compiler_params=_cparams(("parallel",)),
        name="peer_residual",
    )(x, y, mod)


def _final_norm_kernel(x_ref, g_ref, o_ref):
    o_ref[...] = _rms_rows(x_ref[...], g_ref[...])


def _final_norm(x, g, tm):
    n = x.shape[0]
    return pl.pallas_call(
        _final_norm_kernel,
        grid=(n // tm,),
        in_specs=[pl.BlockSpec((tm, D_MODEL), lambda i: (i, 0)), pl.BlockSpec((1, D_MODEL), lambda i: (0, 0))],
        out_specs=pl.BlockSpec((tm, D_MODEL), lambda i: (i, 0)),
        out_shape=jax.ShapeDtypeStruct((n, D_MODEL), F32),
        compiler_params=_cparams(("parallel",)),
        name="final_norm",
    )(x, g)


def _rot_half_columns(w):
    half = ROPE_AXIS_DIM // 2
    w4 = w.reshape(w.shape[:-1] + (2, 2, half))
    return jnp.stack([-w4[..., 1, :], w4[..., 0, :]], axis=-2).reshape(w.shape)


def _rope_tables(seq_len, rotate):
    cos = np.zeros((seq_len, LANES), np.float32)
    sin = np.zeros((seq_len, LANES), np.float32)
    cos[:, :MLA_NOPE + MLA_ROPE] = 1.0
    if rotate:
        pos = np.arange(seq_len)
        half = ROPE_AXIS_DIM // 2
        inv = (1.0 / (ROPE_BASE ** (np.arange(0, ROPE_AXIS_DIM, 2, dtype=np.float32) / ROPE_AXIS_DIM))).astype(np.float32)
        for axis, p in enumerate(((pos // GRID_W).astype(np.float32), (pos % GRID_W).astype(np.float32))):
            ang = p[:, None] * inv[None, :]
            for part in range(2):
                lo = MLA_NOPE + axis * ROPE_AXIS_DIM + part * half
                cos[:, lo:lo + half] = np.cos(ang)
                sin[:, lo:lo + half] = np.sin(ang)
    return jnp.asarray(cos), jnp.asarray(sin)


def _prep_layer_weights(l, w_in, mla_wqb, mla_wkvb, w_out, peer_wq):
    wi = w_in[l]
    kpe_w = wi[:, IN_KPE0:IN_KPE0 + MLA_ROPE]
    w_pad = jnp.concatenate(
        [wi[:, :IN_KPE0], jnp.zeros((D_MODEL, MLA_NOPE), F32), kpe_w, _rot_half_columns(kpe_w)], axis=1).astype(BF16)
    wq = mla_wqb[l].reshape(MLA_Q_LORA, MLA_HEADS, MLA_NOPE + MLA_ROPE)
    wq_pe = wq[..., MLA_NOPE:]
    wq128 = jnp.concatenate([wq, _rot_half_columns(wq_pe)], axis=-1).transpose(1, 0, 2).astype(BF16)
    wkv128 = mla_wkvb[l].reshape(MLA_KV_LORA, MLA_HEADS, MLA_NOPE + MLA_V).transpose(1, 0, 2).astype(BF16)
    return w_pad, wq128, wkv128, w_out[l].astype(BF16), peer_wq[l].astype(BF16)


def _trunk_layer(x, mod, l, W, P, lb, batch, seq_len, ctx, rope):
    w_pad, wq128, wkv128, wo, wpq = W
    tm = 256
    hg, cqn, ckvn, kpe128 = _inproj(x, P['norm_mix'][l][None], mod, w_pad, P['mla_q_norm'][l][None],
                                    P['mla_kv_norm'][l][None], seq_len, tm)
    if ctx is None:
        s0 = jnp.zeros((batch, 2, HG_HEADS, HG_D, HG_D), F32)
    else:
        s0 = ctx[2]
    gn2 = jnp.tile(P['hgrn_norm'][l], 2)[None]
    hg_out, s_fin = _hgrn(hg, lb[0, l][None], lb[1, l][None], gn2, s0, batch, seq_len)

    cos, sin = rope
    q128 = _qproj(cqn, wq128, cos, sin, tm)
    k128, kv128 = _kvproj(ckvn, kpe128, wkv128, cos, sin, tm)
    lk = seq_len
    if ctx is not None:
        past = ctx[0].shape[1]
        cos_c, sin_c = _rope_tables(past, False)
        kpe_c = jnp.pad(ctx[1].reshape(batch * past, MLA_ROPE), ((0, 0), (MLA_NOPE, LANES - MLA_NOPE - MLA_ROPE)))
        k_c, kv_c = _kvproj(ctx[0].reshape(batch * past, MLA_KV_LORA), kpe_c, wkv128, cos_c, sin_c, past)
        join = lambda a, b: jnp.concatenate(
            [a.reshape(MLA_HEADS, batch, past, LANES), b.reshape(MLA_HEADS, batch, seq_len, LANES)],
            axis=2).reshape(MLA_HEADS, batch * (past + seq_len), LANES)
        k128, kv128 = join(k_c, k128), join(kv_c, kv128)
        lk = past + seq_len
    attn = _attention(q128, k128, kv128, batch, seq_len, lk, 256)
    x = _outproj(x, hg_out, attn, wo, mod, seq_len, tm)

    keys = P['peer_keys'][l]
    n_experts = P['peer_u'].shape[1]
    h2, idx, gate = _peerq(x, P['norm_ffn'][l][None], mod, wpq, keys[0], keys[1], seq_len, tm, l * n_experts)
    act = _peer_u(h2, idx, P['peer_u'].reshape(-1, D_MODEL))
    w = _peer_gelu(gate, act, tm)
    y = _peer_v(w, idx, P['peer_v'].reshape(-1, D_MODEL))
    x = _peer_residual(x, y, mod, seq_len, tm)
    return x, (ckvn, kpe128[:, MLA_NOPE:MLA_NOPE + MLA_ROPE], s_fin)


def kernel(x_prompt, x_sample, cache_ckv, cache_kpe, state_hgrn, c, c_ctx, w_ada, b_ada, norm_mix, w_in,
           hgrn_lb_logits, hgrn_norm, mla_q_norm, mla_wqb, mla_kv_norm, mla_wkvb, w_out, norm_ffn,
           peer_wq, peer_keys, peer_u, peer_v, final_norm):
    P = {'norm_mix': norm_mix, 'hgrn_norm': hgrn_norm, 'mla_q_norm': mla_q_norm, 'mla_kv_norm': mla_kv_norm,
         'norm_ffn': norm_ffn, 'peer_keys': peer_keys, 'peer_u': peer_u, 'peer_v': peer_v}
    depth = w_in.shape[0]
    bp, lp, _ = x_prompt.shape
    bs, ls, _ = x_sample.shape

    p = jax.nn.softmax(hgrn_lb_logits.astype(F32), axis=1)
    lb = jnp.cumsum(p, axis=1) - p[:, :1]

    cond8 = jnp.zeros((8, D_MODEL), F32).at[0].set(c_ctx).at[1:1 + bs].set(c)
    mod_all = _ada_mod(cond8, w_ada, b_ada)

    rope_p = _rope_tables(lp, False)
    rope_s = _rope_tables(ls, True)

    xp = x_prompt.reshape(bp * lp, D_MODEL)
    xs = x_sample.reshape(bs * ls, D_MODEL)
    ckvs, kpes, states = [], [], []
    for l in range(depth):
        W = _prep_layer_weights(l, w_in, mla_wqb, mla_wkvb, w_out, peer_wq)
        mod_p = mod_all[l, 0:1][:, None, :]
        mod_s = mod_all[l, 1:1 + bs][:, None, :]
        xp, (ckv_l, kpe_l, st_l) = _trunk_layer(xp, mod_p, l, W, P, lb, bp, lp, None, rope_p)
        ckvs.append(ckv_l.reshape(bp, lp, MLA_KV_LORA))
        kpes.append(kpe_l.reshape(bp, lp, MLA_ROPE))
        states.append(st_l)
        xs, _ = _trunk_layer(xs, mod_s, l, W, P, lb, bs, ls,
                             (cache_ckv[:, l], cache_kpe[:, l], state_hgrn[:, l]), rope_s)

    y_prompt = _final_norm(xp, final_norm[None], 256).reshape(bp, lp, D_MODEL)
    y_sample = _final_norm(xs, final_norm[None], 256).reshape(bs, ls, D_MODEL)
    return (y_prompt, y_sample, jnp.stack(ckvs, axis=1), jnp.stack(kpes, axis=1), jnp.stack(states, axis=1))
```

```python
import functools
import math

import numpy as np
import jax
import jax.numpy as jnp
from jax import lax
from jax.experimental import pallas as pl
from jax.experimental.pallas import tpu as pltpu
from jax.experimental.pallas import tpu_sc as plsc

F32 = jnp.float32
BF16 = jnp.bfloat16
HIGHEST = lax.Precision.HIGHEST

D_MODEL = 1024
EPS = 1e-6
GRID_W = 64
HG_HEADS = 8
HG_D = 64
HG_CHUNK = 32
HG_W = HG_HEADS * HG_D
FORGET_FLOOR = 1e-6
MLA_HEADS = 8
MLA_Q_LORA = 384
MLA_KV_LORA = 256
MLA_NOPE = 64
MLA_ROPE = 32
MLA_V = 64
MLA_SCALE = (MLA_NOPE + MLA_ROPE) ** -0.5
ROPE_AXIS_DIM = MLA_ROPE // 2
ROPE_BASE = 10000.0
PEER_HEADS = 8
PEER_NKEYS = 128
PEER_DQ = 256
PEER_TOPK = 16

LANES = 128
VMEM_LIMIT = 48 * 1024 * 1024

IN_HG = 5 * HG_W
IN_CQ0 = IN_HG
IN_CKV0 = IN_CQ0 + MLA_Q_LORA
IN_KPE0 = IN_CKV0 + MLA_KV_LORA
IN_PAD = IN_KPE0 + LANES


def _cparams(sem):
    return pltpu.CompilerParams(dimension_semantics=sem, vmem_limit_bytes=VMEM_LIMIT)


def _split_bf16(a):
    hi = a.astype(BF16)
    lo = (a - hi.astype(F32)).astype(BF16)
    return hi, lo


def _dot3(a, b, dims=(((1,), (0,)), ((), ()))):
    ah, al = _split_bf16(a)
    bh, bl = _split_bf16(b)
    d = functools.partial(lax.dot_general, dimension_numbers=dims, preferred_element_type=F32)
    return d(ah, bh) + d(al, bh) + d(ah, bl)


def _dot_sel(a, sel_bf):
    ah, al = _split_bf16(a)
    return (jnp.dot(ah, sel_bf, preferred_element_type=F32)
            + jnp.dot(al, sel_bf, preferred_element_type=F32))


def _rms_rows(x, g):
    return x * lax.rsqrt(jnp.mean(x * x, axis=-1, keepdims=True) + EPS) * g


def _ada_kernel(cond_ref, w_ref, b_ref, o_ref):
    cnd = cond_ref[...]
    s = cnd * jax.nn.sigmoid(cnd)
    o_ref[0] = jnp.dot(s, w_ref[0], precision=HIGHEST, preferred_element_type=F32) + b_ref[0]


def _ada_mod(cond8, w_ada, b_ada):
    depth, d, n6 = w_ada.shape
    tn = 1536
    return pl.pallas_call(
        _ada_kernel,
        grid=(depth, n6 // tn),
        in_specs=[pl.BlockSpec((8, d), lambda l, j: (0, 0)),
                  pl.BlockSpec((1, d, tn), lambda l, j: (l, 0, j)),
                  pl.BlockSpec((1, 1, tn), lambda l, j: (l, 0, j))],
        out_specs=pl.BlockSpec((1, 8, tn), lambda l, j: (l, 0, j)),
        out_shape=jax.ShapeDtypeStruct((depth, 8, n6), F32),
        compiler_params=_cparams(("parallel", "parallel")),
        name="ada_mod",
    )(cond8, w_ada, b_ada.reshape(depth, 1, n6))


def _inproj_kernel(x_ref, g_ref, mod_ref, w_ref, gq_ref, gkv_ref, hg_ref, cq_ref, ckv_ref, kpe_ref):
    x = x_ref[...]
    shift = mod_ref[0, :, 0:D_MODEL]
    scale = mod_ref[0, :, D_MODEL:2 * D_MODEL]
    h = (_rms_rows(x, g_ref[...]) * (1.0 + scale) + shift).astype(BF16)
    hg_ref[...] = jnp.dot(h, w_ref[:, 0:IN_HG], preferred_element_type=F32)
    cq = jnp.dot(h, w_ref[:, IN_CQ0:IN_CKV0], preferred_element_type=F32)
    cq_ref[...] = _rms_rows(cq, gq_ref[...]).astype(BF16)
    ckv = jnp.dot(h, w_ref[:, IN_CKV0:IN_KPE0], preferred_element_type=F32)
    ckv_ref[...] = _rms_rows(ckv, gkv_ref[...])
    kpe_ref[...] = jnp.dot(h, w_ref[:, IN_KPE0:IN_PAD], preferred_element_type=F32)


def _inproj(x, g, mod, w_pad, gq, gkv, seq_len, tm):
    n = x.shape[0]
    per_seq = seq_len // tm
    n_cond = mod.shape[0]
    cond_of = (lambda i: (0, 0, 0)) if n_cond == 1 else (lambda i: (i // per_seq, 0, 0))
    row = lambda i: (i, 0)
    const = lambda i: (0, 0)
    return pl.pallas_call(
        _inproj_kernel,
        grid=(n // tm,),
        in_specs=[pl.BlockSpec((tm, D_MODEL), row),
                  pl.BlockSpec((1, D_MODEL), const),
                  pl.BlockSpec((1, 1, 6 * D_MODEL), cond_of),
                  pl.BlockSpec((D_MODEL, IN_PAD), const),
                  pl.BlockSpec((1, MLA_Q_LORA), const),
                  pl.BlockSpec((1, MLA_KV_LORA), const)],
        out_specs=[pl.BlockSpec((tm, IN_HG), row),
                   pl.BlockSpec((tm, MLA_Q_LORA), row),
                   pl.BlockSpec((tm, MLA_KV_LORA), row),
                   pl.BlockSpec((tm, LANES), row)],
        out_shape=[jax.ShapeDtypeStruct((n, IN_HG), F32),
                   jax.ShapeDtypeStruct((n, MLA_Q_LORA), BF16),
                   jax.ShapeDtypeStruct((n, MLA_KV_LORA), F32),
                   jax.ShapeDtypeStruct((n, LANES), F32)],
        compiler_params=_cparams(("parallel",)),
        name="in_proj",
    )(x, g, mod, w_pad, gq, gkv)


def _hgrn_kernel(q_ref, ff_ref, fb_ref, v_ref, gt_ref, lbf_ref, lbb_ref, gn_ref, s0_ref,
                 out_ref, sfin_ref, acc_ref, accb_ref, *, seq_len):
    C = HG_CHUNK
    n_chunks = seq_len // C
    r2 = lax.broadcasted_iota(jnp.int32, (LANES, LANES), 0) // HG_D
    c2 = lax.broadcasted_iota(jnp.int32, (LANES, LANES), 1) // HG_D
    same_head = r2 == c2
    sel_bf = same_head.astype(BF16)
    ti = lax.broadcasted_iota(jnp.int32, (C, C), 0)
    ui = lax.broadcasted_iota(jnp.int32, (C, C), 1)
    tri_lo = (ui <= ti).astype(F32)
    tri_up = (ui >= ti).astype(F32)
    s3 = lax.broadcasted_iota(jnp.int32, (C, C, 1), 0)
    t3 = lax.broadcasted_iota(jnp.int32, (C, C, 1), 1)
    ones_c = jnp.ones((C, LANES), F32)

    def block_diag(s_a, s_b):
        z = jnp.zeros((HG_D, HG_D), F32)
        return jnp.concatenate([jnp.concatenate([s_a, z], axis=1),
                                jnp.concatenate([z, s_b], axis=1)], axis=0)

    def chunk_step(c, S, z_ref, lb_ref, forward):
        rows = pl.ds(pl.multiple_of(c * C, C), C)
        lb = lb_ref[...]
        f = jnp.maximum(lb + (1.0 - lb) * jax.nn.sigmoid(z_ref[rows, :]), FORGET_FLOOR)
        g = jnp.log(f)
        k = 1.0 - f
        q = q_ref[rows, :] * (HG_D ** -0.5)
        v = v_ref[rows, :]
        b = jnp.dot(tri_lo if forward else tri_up, g, precision=HIGHEST, preferred_element_type=F32)
        o_inter = _dot3(q * jnp.exp(b), S)
        mask = (t3 >= s3) if forward else (t3 <= s3)
        diff = b[None, :, :] - b[:, None, :]
        e = jnp.where(mask, jnp.exp(jnp.where(mask, diff, 0.0)), 0.0) * q[None, :, :] * k[:, None, :]
        a = _dot_sel(e.reshape(C * C, LANES), sel_bf).reshape(C, C, LANES)
        o_intra = jnp.sum(a * v[:, None, :], axis=0)
        b_end = b[C - 1:C, :] if forward else b[0:1, :]
        kk = k * jnp.exp(b_end - b)
        tdims = (((0,), (0,)), ((), ()))
        total = lax.dot_general(g, ones_c, tdims, precision=HIGHEST, preferred_element_type=F32)
        s_new = jnp.exp(total) * S + jnp.where(same_head, _dot3(kk, v, tdims), 0.0)
        return s_new, o_inter + o_intra

    s_f0 = block_diag(s0_ref[0, 0, 0], s0_ref[0, 0, 1])
    s_b0 = block_diag(s0_ref[0, 1, 0], s0_ref[0, 1, 1])

    def body(i, carry):
        s_f, s_b = carry
        cb = n_chunks - 1 - i
        s_f, o_f = chunk_step(i, s_f, ff_ref, lbf_ref, True)
        s_b, o_b = chunk_step(cb, s_b, fb_ref, lbb_ref, False)
        acc_ref[pl.ds(pl.multiple_of(i * C, C), C), :] = o_f
        accb_ref[pl.ds(pl.multiple_of(cb * C, C), C), :] = o_b
        return s_f, s_b

    s_f, s_b = lax.fori_loop(0, n_chunks, body, (s_f0, s_b0))

    sfin_ref[0, 0, 0] = s_f[0:HG_D, 0:HG_D]
    sfin_ref[0, 0, 1] = s_f[HG_D:, HG_D:]
    sfin_ref[0, 1, 0] = s_b[0:HG_D, 0:HG_D]
    sfin_ref[0, 1, 1] = s_b[HG_D:, HG_D:]

    o = acc_ref[...] + accb_ref[...]
    ms = _dot_sel(o * o, sel_bf) * (1.0 / HG_D)
    gt = gt_ref[...]
    out_ref[...] = (o * lax.rsqrt(ms + EPS) * gn_ref[...] * (gt * jax.nn.sigmoid(gt))).astype(BF16)


def _hgrn(hg, lb_f, lb_b, gn2, s0, batch, seq_len):
    n = hg.shape[0]
    pairs = HG_W // LANES
    part = lambda p: pl.BlockSpec((seq_len, LANES), lambda b, j, p=p: (b, p * pairs + j))
    lane_row = pl.BlockSpec((1, LANES), lambda b, j: (0, j))
    st = pl.BlockSpec((1, 2, 2, HG_D, HG_D), lambda b, j: (b, 0, j, 0, 0))
    return pl.pallas_call(
        functools.partial(_hgrn_kernel, seq_len=seq_len),
        grid=(batch, pairs),
        in_specs=[part(0), part(1), part(2), part(3), part(4), lane_row, lane_row,
                  pl.BlockSpec((1, LANES), lambda b, j: (0, 0)), st],
        out_specs=[pl.BlockSpec((seq_len, LANES), lambda b, j: (b, j)), st],
        out_shape=[jax.ShapeDtypeStruct((n, HG_W), BF16),
                   jax.ShapeDtypeStruct((batch, 2, HG_HEADS, HG_D, HG_D), F32)],
        scratch_shapes=[pltpu.VMEM((seq_len, LANES), F32), pltpu.VMEM((seq_len, LANES), F32)],
        compiler_params=_cparams(("parallel", "parallel")),
        name="hgrn2",
    )(hg, hg, hg, hg, hg, lb_f, lb_b, gn2, s0)


def _rope_lanes(x, cos, sin):
    return x * cos + pltpu.roll(x, 96, 1) * sin


def _qproj_kernel(cq_ref, w_ref, cos_ref, sin_ref, q_ref):
    cq = cq_ref[...]
    for h in range(MLA_HEADS):
        qh = jnp.dot(cq, w_ref[h], preferred_element_type=F32)
        q_ref[h] = (_rope_lanes(qh, cos_ref[...], sin_ref[...]) * MLA_SCALE).astype(BF16)


def _kvproj_kernel(ckv_ref, kpe_ref, w_ref, cos_ref, sin_ref, k_ref, kv_ref):
    ckv = ckv_ref[...].astype(BF16)
    kpe = _rope_lanes(kpe_ref[...], cos_ref[...], sin_ref[...])
    nope = lax.broadcasted_iota(jnp.int32, kpe.shape, 1) < MLA_NOPE
    for h in range(MLA_HEADS):
        kvh = jnp.dot(ckv, w_ref[h], preferred_element_type=F32)
        kv_ref[h] = kvh.astype(BF16)
        k_ref[h] = jnp.where(nope, kvh, kpe).astype(BF16)


def _qproj(cqn, wq, cos, sin, tm):
    n = cqn.shape[0]
    pos_blocks = cos.shape[0] // tm
    return pl.pallas_call(
        _qproj_kernel,
        grid=(n // tm,),
        in_specs=[pl.BlockSpec((tm, MLA_Q_LORA), lambda i: (i, 0)),
                  pl.BlockSpec((MLA_HEADS, MLA_Q_LORA, LANES), lambda i: (0, 0, 0)),
                  pl.BlockSpec((tm, LANES), lambda i: (i % pos_blocks, 0)),
                  pl.BlockSpec((tm, LANES), lambda i: (i % pos_blocks, 0))],
        out_specs=pl.BlockSpec((MLA_HEADS, tm, LANES), lambda i: (0, i, 0)),
        out_shape=jax.ShapeDtypeStruct((MLA_HEADS, n, LANES), BF16),
        compiler_params=_cparams(("parallel",)),
        name="mla_q_proj",
    )(cqn, wq, cos, sin)


def _kvproj(ckvn, kpe128, wkv, cos, sin, tm):
    n = ckvn.shape[0]
    pos_blocks = cos.shape[0] // tm
    hm = pl.BlockSpec((MLA_HEADS, tm, LANES), lambda i: (0, i, 0))
    return pl.pallas_call(
        _kvproj_kernel,
        grid=(n // tm,),
        in_specs=[pl.BlockSpec((tm, MLA_KV_LORA), lambda i: (i, 0)),
                  pl.BlockSpec((tm, LANES), lambda i: (i, 0)),
                  pl.BlockSpec((MLA_HEADS, MLA_KV_LORA, LANES), lambda i: (0, 0, 0)),
                  pl.BlockSpec((tm, LANES), lambda i: (i % pos_blocks, 0)),
                  pl.BlockSpec((tm, LANES), lambda i: (i % pos_blocks, 0))],
        out_specs=[hm, hm],
        out_shape=[jax.ShapeDtypeStruct((MLA_HEADS, n, LANES), BF16)] * 2,
        compiler_params=_cparams(("parallel",)),
        name="mla_kv_proj",
    )(ckvn, kpe128, wkv, cos, sin)


def _attn_kernel(q_ref, k_ref, kv_ref, o_ref):
    low = lax.broadcasted_iota(jnp.int32, (q_ref.shape[1], LANES), 1) < MLA_V
    nt = (((1,), (1,)), ((), ()))
    outs = []
    for h in range(MLA_HEADS):
        s = lax.dot_general(q_ref[h], k_ref[h], nt, preferred_element_type=F32)
        p = jnp.exp(s - jnp.max(s, axis=-1, keepdims=True))
        denom = jnp.sum(p, axis=-1, keepdims=True)
        o = jnp.dot(p.astype(BF16), kv_ref[h], preferred_element_type=F32) / denom
        outs.append(o)
    for j in range(MLA_HEADS // 2):
        pair = jnp.where(low, pltpu.roll(outs[2 * j], MLA_V, 1), outs[2 * j + 1])
        o_ref[:, j * LANES:(j + 1) * LANES] = pair.astype(BF16)


def _attention(q128, k128, kv128, batch, lq, lk, tq):
    n = q128.shape[1]
    qb = lq // tq
    kspec = pl.BlockSpec((MLA_HEADS, lk, LANES), lambda b, i: (0, b, 0))
    return pl.pallas_call(
        _attn_kernel,
        grid=(batch, qb),
        in_specs=[pl.BlockSpec((MLA_HEADS, tq, LANES), lambda b, i: (0, b * qb + i, 0)), kspec, kspec],
        out_specs=pl.BlockSpec((tq, MLA_HEADS * MLA_V), lambda b, i: (b * qb + i, 0)),
        out_shape=jax.ShapeDtypeStruct((n, MLA_HEADS * MLA_V), BF16),
        compiler_params=_cparams(("parallel", "parallel")),
        name="mla_attention",
    )(q128, k128, kv128)


def _outproj_kernel(x_ref, hg_ref, at_ref, w_ref, mod_ref, o_ref):
    gate = mod_ref[0, :, 2 * D_MODEL:3 * D_MODEL]
    mix = (jnp.dot(hg_ref[...], w_ref[0:HG_W, :], preferred_element_type=F32)
           + jnp.dot(at_ref[...], w_ref[HG_W:, :], preferred_element_type=F32))
    o_ref[...] = x_ref[...] + gate * mix


def _outproj(x, hg_out, attn, w_out, mod, seq_len, tm):
    n = x.shape[0]
    per_seq = seq_len // tm
    n_cond = mod.shape[0]
    cond_of = (lambda i: (0, 0, 0)) if n_cond == 1 else (lambda i: (i // per_seq, 0, 0))
    row = lambda i: (i, 0)
    return pl.pallas_call(
        _outproj_kernel,
        grid=(n // tm,),
        in_specs=[pl.BlockSpec((tm, D_MODEL), row),
                  pl.BlockSpec((tm, HG_W), row),
                  pl.BlockSpec((tm, MLA_HEADS * MLA_V), row),
                  pl.BlockSpec((2 * HG_W, D_MODEL), lambda i: (0, 0)),
                  pl.BlockSpec((1, 1, 6 * D_MODEL), cond_of)],
        out_specs=pl.BlockSpec((tm, D_MODEL), row),
        out_shape=jax.ShapeDtypeStruct((n, D_MODEL), F32),
        compiler_params=_cparams(("parallel",)),
        name="out_proj",
    )(x, hg_out, attn, w_out, mod)


_CAND = [(a, b) for a in range(PEER_TOPK) for b in range(PEER_TOPK) if (a + 1) * (b + 1) <= PEER_TOPK]
_CAND_PAD = -(-len(_CAND) // 8) * 8


def _topk_rows(work, k):
    rows = lax.broadcasted_iota(jnp.int32, work.shape, 0).astype(F32)
    n_rows = float(work.shape[0])
    vals, idxs = [], []
    for _ in range(k):
        m = jnp.max(work, axis=0, keepdims=True)
        sel = jnp.min(jnp.where(work == m, rows, n_rows), axis=0, keepdims=True)
        vals.append(m)
        idxs.append(sel)
        work = jnp.where(rows == sel, -jnp.inf, work)
    return jnp.concatenate(vals, axis=0), jnp.concatenate(idxs, axis=0)


def _peerq_kernel(x_ref, g_ref, mod_ref, w_ref, k1_ref, k2_ref, h2_ref, idx_ref, gate_ref, *, idx_offset):
    idx_rows, gate_rows = [], []
    x = x_ref[...]
    shift = mod_ref[0, :, 3 * D_MODEL:4 * D_MODEL]
    scale = mod_ref[0, :, 4 * D_MODEL:5 * D_MODEL]
    h2 = _rms_rows(x, g_ref[...]) * (1.0 + scale) + shift
    h2_ref[...] = h2
    qp = jnp.dot(h2.astype(BF16), w_ref[...], preferred_element_type=F32)
    nt = (((1,), (1,)), ((), ()))
    half = PEER_DQ // 2
    for h in range(PEER_HEADS):
        q1 = qp[:, h * PEER_DQ:h * PEER_DQ + half]
        q2 = qp[:, h * PEER_DQ + half:(h + 1) * PEER_DQ]
        s1 = _dot3(k1_ref[h], q1, nt)
        s2 = _dot3(k2_ref[h], q2, nt)
        v1, i1 = _topk_rows(s1, PEER_TOPK)
        v2, i2 = _topk_rows(s2, PEER_TOPK)
        cs = [v1[a:a + 1, :] + v2[b:b + 1, :] for a, b in _CAND]
        ci = [i1[a:a + 1, :] * PEER_NKEYS + i2[b:b + 1, :] for a, b in _CAND]
        n_pad = _CAND_PAD - len(_CAND)
        if n_pad:
            cs.append(jnp.full((n_pad, x.shape[0]), -jnp.inf, F32))
            ci.append(jnp.zeros((n_pad, x.shape[0]), F32))
        cand_s = jnp.concatenate(cs, axis=0)
        cand_i = jnp.concatenate(ci, axis=0)
        best_s, best_pos = _topk_rows(cand_s, PEER_TOPK)
        rows = lax.broadcasted_iota(jnp.int32, cand_s.shape, 0).astype(F32)
        picked = [jnp.sum(jnp.where(rows == best_pos[r:r + 1, :], cand_i, 0.0), axis=0, keepdims=True)
                  for r in range(PEER_TOPK)]
        idx_rows.extend(picked)
        ex = jnp.exp(best_s - best_s[0:1, :])
        gate_rows.append(ex / jnp.sum(ex, axis=0, keepdims=True))
    idx_ref[...] = jnp.concatenate(idx_rows, axis=0).T.astype(jnp.int32) + idx_offset
    gate_ref[...] = jnp.concatenate(gate_rows, axis=0).T


def _peerq(x, g, mod, wq, k1, k2, seq_len, tm, idx_offset):
    n = x.shape[0]
    per_seq = seq_len // tm
    n_cond = mod.shape[0]
    cond_of = (lambda i: (0, 0, 0)) if n_cond == 1 else (lambda i: (i // per_seq, 0, 0))
    kspec = pl.BlockSpec((PEER_HEADS, PEER_NKEYS, PEER_DQ // 2), lambda i: (0, 0, 0))
    tk = PEER_HEADS * PEER_TOPK
    return pl.pallas_call(
        functools.partial(_peerq_kernel, idx_offset=idx_offset),
        grid=(n // tm,),
        in_specs=[pl.BlockSpec((tm, D_MODEL), lambda i: (i, 0)),
                  pl.BlockSpec((1, D_MODEL), lambda i: (0, 0)),
                  pl.BlockSpec((1, 1, 6 * D_MODEL), cond_of),
                  pl.BlockSpec((D_MODEL, PEER_HEADS * PEER_DQ), lambda i: (0, 0)),
                  kspec, kspec],
        out_specs=[pl.BlockSpec((tm, D_MODEL), lambda i: (i, 0)),
                   pl.BlockSpec((tm, tk), lambda i: (i, 0)),
                   pl.BlockSpec((tm, tk), lambda i: (i, 0))],
        out_shape=[jax.ShapeDtypeStruct((n, D_MODEL), F32),
                   jax.ShapeDtypeStruct((n, tk), jnp.int32),
                   jax.ShapeDtypeStruct((n, tk), F32)],
        compiler_params=_cparams(("parallel",)),
        name="peer_retrieve",
    )(x, g, mod, wq, k1, k2)


SC_CORES = 2
SC_SUBCORES = 16
SC_LANES = 16
SC_TILES = SC_CORES * SC_SUBCORES
SC_NBUF = 4
PEER_TK = PEER_HEADS * PEER_TOPK
D_VREGS = D_MODEL // SC_LANES
V_COLS = 16


def _sc_mesh():
    return plsc.VectorSubcoreMesh(core_axis_name="c", subcore_axis_name="s")


def _sc_params():
    return pltpu.CompilerParams(needs_layout_passes=False)


def _sc_token_base(tokens_per_tile):
    return (lax.axis_index("s") * SC_CORES + lax.axis_index("c")) * tokens_per_tile


def _peer_u_body(h2_hbm, idx_hbm, u_hbm, act_hbm, idx_v, act_v, x_v, rows_v, sem_x, sem_g, *, tpt):
    base = _sc_token_base(tpt)
    n_items = tpt * PEER_HEADS
    pltpu.sync_copy(idx_hbm.at[pl.ds(base, tpt)], idx_v)

    def gather(i, slot):
        t, h = i // PEER_HEADS, i % PEER_HEADS
        return pltpu.make_async_copy(u_hbm.at[idx_v.at[t, pl.ds(h * PEER_TOPK, PEER_TOPK)]],
                                     rows_v.at[slot], sem_g.at[slot])

    def x_copy(t):
        return pltpu.make_async_copy(h2_hbm.at[base + t], x_v.at[t % 2], sem_x.at[t % 2])

    x_copy(0).start()
    for b in range(SC_NBUF - 1):
        gather(b, b).start()
    lane = lax.iota(jnp.int32, SC_LANES)
    zero = jnp.zeros((SC_LANES,), F32)

    @pl.loop(0, n_items, step=SC_NBUF)
    def _(i0):
        for b in range(SC_NBUF):
            i = i0 + b
            t, h = i // PEER_HEADS, i % PEER_HEADS

            @pl.when(h == 0)
            def _():
                x_copy(t).wait()

                @pl.when(t + 1 < tpt)
                def _():
                    x_copy(t + 1).start()

            @pl.when(i + SC_NBUF - 1 < n_items)
            def _():
                gather(i + SC_NBUF - 1, (b + SC_NBUF - 1) % SC_NBUF).start()

            gather(i, b).wait()
            xs = t % 2

            def cbody(c, accs):
                col = pl.ds(c * SC_LANES, SC_LANES)
                xv = x_v[xs, col]
                return tuple(a + xv * rows_v[b, j, col] for j, a in enumerate(accs))

            accs = plsc.parallel_loop(0, D_VREGS, unroll=2, carry=(zero,) * PEER_TOPK)(cbody)
            vec = zero
            for j in range(PEER_TOPK):
                vec = jnp.where(lane == j, jnp.sum(accs[j]), vec)
            act_v[t, pl.ds(h * PEER_TOPK, PEER_TOPK)] = vec

    pltpu.sync_copy(act_v, act_hbm.at[pl.ds(base, tpt)])


def _peer_v_body(w_hbm, idx_hbm, v_hbm, y_hbm, idx_v, w_v, y_v, rows_v, sem_y, sem_g, *, tpt):
    base = _sc_token_base(tpt)
    n_items = tpt * PEER_HEADS
    pltpu.sync_copy(idx_hbm.at[pl.ds(base, tpt)], idx_v)
    pltpu.sync_copy(w_hbm.at[pl.ds(base, tpt)], w_v)

    def gather(i, slot):
        t, h = i // PEER_HEADS, i % PEER_HEADS
        return pltpu.make_async_copy(v_hbm.at[idx_v.at[t, pl.ds(h * PEER_TOPK, PEER_TOPK)]],
                                     rows_v.at[slot], sem_g.at[slot])

    def y_copy(t):
        return pltpu.make_async_copy(y_v.at[t % 2], y_hbm.at[base + t], sem_y.at[t % 2])

    for b in range(SC_NBUF - 1):
        gather(b, b).start()
    lane = lax.iota(jnp.int32, SC_LANES)
    zero = jnp.zeros((SC_LANES,), F32)

    @pl.loop(0, n_items, step=SC_NBUF)
    def _(i0):
        for b in range(SC_NBUF):
            i = i0 + b
            t, h = i // PEER_HEADS, i % PEER_HEADS
            ys = t % 2

            @pl.when(i + SC_NBUF - 1 < n_items)
            def _():
                gather(i + SC_NBUF - 1, (b + SC_NBUF - 1) % SC_NBUF).start()

            @pl.when(h == 0)
            def _():
                @pl.when(t >= 2)
                def _():
                    y_copy(t - 2).wait()

                @pl.loop(0, D_VREGS)
                def _(c):
                    y_v[ys, pl.ds(c * SC_LANES, SC_LANES)] = zero

            gather(i, b).wait()
            wv = w_v[t, pl.ds(h * PEER_TOPK, PEER_TOPK)]
            ws = [jnp.sum(jnp.where(lane == j, wv, 0.0)) for j in range(PEER_TOPK)]

            @plsc.parallel_loop(0, D_VREGS, unroll=2)
            def _(c):
                col = pl.ds(c * SC_LANES, SC_LANES)
                terms = [ws[j] * rows_v[b, j, col] for j in range(PEER_TOPK)]
                while len(terms) > 1:
                    terms = [terms[k] + terms[k + 1] for k in range(0, len(terms), 2)]
                y_v[ys, col] = y_v[ys, col] + terms[0]

            @pl.when(h == PEER_HEADS - 1)
            def _():
                y_copy(t).start()

    y_copy(tpt - 2).wait()
    y_copy(tpt - 1).wait()


def _sc_scratch(tpt, side_dtype):
    return [pltpu.VMEM((tpt, PEER_TK), jnp.int32),
            pltpu.VMEM((tpt, PEER_TK), side_dtype),
            pltpu.VMEM((2, D_MODEL), F32),
            pltpu.VMEM((SC_NBUF, PEER_TOPK, D_MODEL), F32),
            pltpu.SemaphoreType.DMA((2,)),
            pltpu.SemaphoreType.DMA((SC_NBUF,))]


def _peer_u(h2, idx, u_all):
    n = h2.shape[0]
    tpt = n // SC_TILES
    return pl.kernel(
        functools.partial(_peer_u_body, tpt=tpt),
        out_type=jax.ShapeDtypeStruct((n, PEER_TK), F32),
        mesh=_sc_mesh(),
        scratch_types=_sc_scratch(tpt, F32),
        compiler_params=_sc_params(),
        name="peer_u_sc",
    )(h2, idx, u_all)


def _peer_v(w, idx, v_all):
    n = w.shape[0]
    tpt = n // SC_TILES
    return pl.kernel(
        functools.partial(_peer_v_body, tpt=tpt),
        out_type=jax.ShapeDtypeStruct((n, D_MODEL), F32),
        mesh=_sc_mesh(),
        scratch_types=_sc_scratch(tpt, F32),
        compiler_params=_sc_params(),
        name="peer_v_sc",
    )(w, idx, v_all)


def _peer_gelu_kernel(gate_ref, act_ref, w_ref):
    act = act_ref[...]
    w_ref[...] = gate_ref[...] * (0.5 * act * (1.0 + lax.erf(act * (2.0 ** -0.5))))


def _peer_gelu(gate, act, tm):
    n = gate.shape[0]
    spec = pl.BlockSpec((tm, PEER_TK), lambda i: (i, 0))
    return pl.pallas_call(
        _peer_gelu_kernel,
        grid=(n // tm,),
        in_specs=[spec, spec],
        out_specs=spec,
        out_shape=jax.ShapeDtypeStruct((n, PEER_TK), F32),
        compiler_params=_cparams(("parallel",)),
        name="peer_gelu",
    )(gate, act)


def _peer_residual_kernel(x_ref, y_ref, mod_ref, o_ref):
    o_ref[...] = x_ref[...] + mod_ref[0, :, 5 * D_MODEL:6 * D_MODEL] * y_ref[...]


def _peer_residual(x, y, mod, seq_len, tm):
    n = x.shape[0]
    per_seq = seq_len // tm
    n_cond = mod.shape[0]
    cond_of = (lambda i: (0, 0, 0)) if n_cond == 1 else (lambda i: (i // per_seq, 0, 0))
    row = pl.BlockSpec((tm, D_MODEL), lambda i: (i, 0))
    return pl.pallas_call(
        _peer_residual_kernel,
        grid=(n // tm,),
        in_specs=[row, row, pl.BlockSpec((1, 1, 6 * D_MODEL), cond_of)],
        out_specs=row,
        out_shape=jax.ShapeDtypeStruct((n, D_MODEL), F32),
        compiler_params=_cparams(("parallel",)),
        name="peer_residual",
    )(x, y, mod)


def _final_norm_kernel(x_ref, g_ref, o_ref):
    o_ref[...] = _rms_rows(x_ref[...], g_ref[...])


def _final_norm(x, g, tm):
    n = x.shape[0]
    return pl.pallas_call(
        _final_norm_kernel,
        grid=(n // tm,),
        in_specs=[pl.BlockSpec((tm, D_MODEL), lambda i: (i, 0)), pl.BlockSpec((1, D_MODEL), lambda i: (0, 0))],
        out_specs=pl.BlockSpec((tm, D_MODEL), lambda i: (i, 0)),
        out_shape=jax.ShapeDtypeStruct((n, D_MODEL), F32),
        compiler_params=_cparams(("parallel",)),
        name="final_norm",
    )(x, g)


def _rot_half_columns(w):
    half = ROPE_AXIS_DIM // 2
    w4 = w.reshape(w.shape[:-1] + (2, 2, half))
    return jnp.stack([-w4[..., 1, :], w4[..., 0, :]], axis=-2).reshape(w.shape)


def _rope_tables(seq_len, rotate):
    cos = np.zeros((seq_len, LANES), np.float32)
    sin = np.zeros((seq_len, LANES), np.float32)
    cos[:, :MLA_NOPE + MLA_ROPE] = 1.0
    if rotate:
        pos = np.arange(seq_len)
        half = ROPE_AXIS_DIM // 2
        inv = (1.0 / (ROPE_BASE ** (np.arange(0, ROPE_AXIS_DIM, 2, dtype=np.float32) / ROPE_AXIS_DIM))).astype(np.float32)
        for axis, p in enumerate(((pos // GRID_W).astype(np.float32), (pos % GRID_W).astype(np.float32))):
            ang = p[:, None] * inv[None, :]
            for part in range(2):
                lo = MLA_NOPE + axis * ROPE_AXIS_DIM + part * half
                cos[:, lo:lo + half] = np.cos(ang)
                sin[:, lo:lo + half] = np.sin(ang)
    return jnp.asarray(cos), jnp.asarray(sin)


def _prep_layer_weights(l, w_in, mla_wqb, mla_wkvb, w_out, peer_wq):
    wi = w_in[l]
    kpe_w = wi[:, IN_KPE0:IN_KPE0 + MLA_ROPE]
    w_pad = jnp.concatenate(
        [wi[:, :IN_KPE0], jnp.zeros((D_MODEL, MLA_NOPE), F32), kpe_w, _rot_half_columns(kpe_w)], axis=1).astype(BF16)
    wq = mla_wqb[l].reshape(MLA_Q_LORA, MLA_HEADS, MLA_NOPE + MLA_ROPE)
    wq_pe = wq[..., MLA_NOPE:]
    wq128 = jnp.concatenate([wq, _rot_half_columns(wq_pe)], axis=-1).transpose(1, 0, 2).astype(BF16)
    wkv128 = mla_wkvb[l].reshape(MLA_KV_LORA, MLA_HEADS, MLA_NOPE + MLA_V).transpose(1, 0, 2).astype(BF16)
    return w_pad, wq128, wkv128, w_out[l].astype(BF16), peer_wq[l].astype(BF16)


def _trunk_layer(x, mod, l, W, P, lb, batch, seq_len, ctx, rope):
    w_pad, wq128, wkv128, wo, wpq = W
    tm = 256
    hg, cqn, ckvn, kpe128 = _inproj(x, P['norm_mix'][l][None], mod, w_pad, P['mla_q_norm'][l][None],
                                    P['mla_kv_norm'][l][None], seq_len, tm)
    if ctx is None:
        s0 = jnp.zeros((batch, 2, HG_HEADS, HG_D, HG_D), F32)
    else:
        s0 = ctx[2]
    gn2 = jnp.tile(P['hgrn_norm'][l], 2)[None]
    hg_out, s_fin = _hgrn(hg, lb[0, l][None], lb[1, l][None], gn2, s0, batch, seq_len)

    cos, sin = rope
    q128 = _qproj(cqn, wq128, cos, sin, tm)
    k128, kv128 = _kvproj(ckvn, kpe128, wkv128, cos, sin, tm)
    lk = seq_len
    if ctx is not None:
        past = ctx[0].shape[1]
        cos_c, sin_c = _rope_tables(past, False)
        kpe_c = jnp.pad(ctx[1].reshape(batch * past, MLA_ROPE), ((0, 0), (MLA_NOPE, LANES - MLA_NOPE - MLA_ROPE)))
        k_c, kv_c = _kvproj(ctx[0].reshape(batch * past, MLA_KV_LORA), kpe_c, wkv128, cos_c, sin_c, past)
        join = lambda a, b: jnp.concatenate(
            [a.reshape(MLA_HEADS, batch, past, LANES), b.reshape(MLA_HEADS, batch, seq_len, LANES)],
            axis=2).reshape(MLA_HEADS, batch * (past + seq_len), LANES)
        k128, kv128 = join(k_c, k128), join(kv_c, kv128)
        lk = past + seq_len
    attn = _attention(q128, k128, kv128, batch, seq_len, lk, 256)
    x = _outproj(x, hg_out, attn, wo, mod, seq_len, tm)

    keys = P['peer_keys'][l]
    n_experts = P['peer_u'].shape[1]
    h2, idx, gate = _peerq(x, P['norm_ffn'][l][None], mod, wpq, keys[0], keys[1], seq_len, tm, l * n_experts)
    act = _peer_u(h2, idx, P['peer_u'].reshape(-1, D_MODEL))
    w = _peer_gelu(gate, act, tm)
    y = _peer_v(w, idx, P['peer_v'].reshape(-1, D_MODEL))
    x = _peer_residual(x, y, mod, seq_len, tm)
    return x, (ckvn, kpe128[:, MLA_NOPE:MLA_NOPE + MLA_ROPE], s_fin)


def kernel(x_prompt, x_sample, cache_ckv, cache_kpe, state_hgrn, c, c_ctx, w_ada, b_ada, norm_mix, w_in,
           hgrn_lb_logits, hgrn_norm, mla_q_norm, mla_wqb, mla_kv_norm, mla_wkvb, w_out, norm_ffn,
           peer_wq, peer_keys, peer_u, peer_v, final_norm):
    P = {'norm_mix': norm_mix, 'hgrn_norm': hgrn_norm, 'mla_q_norm': mla_q_norm, 'mla_kv_norm': mla_kv_norm,
         'norm_ffn': norm_ffn, 'peer_keys': peer_keys, 'peer_u': peer_u, 'peer_v': peer_v}
    depth = w_in.shape[0]
    bp, lp, _ = x_prompt.shape
    bs, ls, _ = x_sample.shape

    p = jax.nn.softmax(hgrn_lb_logits.astype(F32), axis=1)
    lb = jnp.cumsum(p, axis=1) - p[:, :1]

    cond8 = jnp.zeros((8, D_MODEL), F32).at[0].set(c_ctx).at[1:1 + bs].set(c)
    mod_all = _ada_mod(cond8, w_ada, b_ada)

    rope_p = _rope_tables(lp, False)
    rope_s = _rope_tables(ls, True)

    xp = x_prompt.reshape(bp * lp, D_MODEL)
    xs = x_sample.reshape(bs * ls, D_MODEL)
    ckvs, kpes, states = [], [], []
    for l in range(depth):
        W = _prep_layer_weights(l, w_in, mla_wqb, mla_wkvb, w_out, peer_wq)
        mod_p = mod_all[l, 0:1][:, None, :]
        mod_s = mod_all[l, 1:1 + bs][:, None, :]
        xp, (ckv_l, kpe_l, st_l) = _trunk_layer(xp, mod_p, l, W, P, lb, bp, lp, None, rope_p)
        ckvs.append(ckv_l.reshape(bp, lp, MLA_KV_LORA))
        kpes.append(kpe_l.reshape(bp, lp, MLA_ROPE))
        states.append(st_l)
        xs, _ = _trunk_layer(xs, mod_s, l, W, P, lb, bs, ls,
                             (cache_ckv[:, l], cache_kpe[:, l], state_hgrn[:, l]), rope_s)

    y_prompt = _final_norm(xp, final_norm[None], 256).reshape(bp, lp, D_MODEL)
    y_sample = _final_norm(xs, final_norm[None], 256).reshape(bs, ls, D_MODEL)
    return (y_prompt, y_sample, jnp.stack(ckvs, axis=1), jnp.stack(kpes, axis=1), jnp.stack(states, axis=1))
```

```python
import functools
import math

import numpy as np
import jax
import jax.numpy as jnp
from jax import lax
from jax.experimental import pallas as pl
from jax.experimental.pallas import tpu as pltpu
from jax.experimental.pallas import tpu_sc as plsc

F32 = jnp.float32
BF16 = jnp.bfloat16
HIGHEST = lax.Precision.HIGHEST

D_MODEL = 1024
EPS = 1e-6
GRID_W = 64
HG_HEADS = 8
HG_D = 64
HG_CHUNK = 32
HG_W = HG_HEADS * HG_D
FORGET_FLOOR = 1e-6
MLA_HEADS = 8
MLA_Q_LORA = 384
MLA_KV_LORA = 256
MLA_NOPE = 64
MLA_ROPE = 32
MLA_V = 64
MLA_SCALE = (MLA_NOPE + MLA_ROPE) ** -0.5
ROPE_AXIS_DIM = MLA_ROPE // 2
ROPE_BASE = 10000.0
PEER_HEADS = 8
PEER_NKEYS = 128
PEER_DQ = 256
PEER_TOPK = 16

LANES = 128
VMEM_LIMIT = 48 * 1024 * 1024

IN_HG = 5 * HG_W
IN_CQ0 = IN_HG
IN_CKV0 = IN_CQ0 + MLA_Q_LORA
IN_KPE0 = IN_CKV0 + MLA_KV_LORA
IN_PAD = IN_KPE0 + LANES


def _cparams(sem):
    return pltpu.CompilerParams(dimension_semantics=sem, vmem_limit_bytes=VMEM_LIMIT)


def _split_bf16(a):
    hi = a.astype(BF16)
    lo = (a - hi.astype(F32)).astype(BF16)
    return hi, lo


def _dot3(a, b, dims=(((1,), (0,)), ((), ()))):
    ah, al = _split_bf16(a)
    bh, bl = _split_bf16(b)
    d = functools.partial(lax.dot_general, dimension_numbers=dims, preferred_element_type=F32)
    return d(ah, bh) + d(al, bh) + d(ah, bl)


def _dot_sel(a, sel_bf):
    ah, al = _split_bf16(a)
    return (jnp.dot(ah, sel_bf, preferred_element_type=F32)
            + jnp.dot(al, sel_bf, preferred_element_type=F32))


def _rms_rows(x, g):
    return x * lax.rsqrt(jnp.mean(x * x, axis=-1, keepdims=True) + EPS) * g


def _ada_kernel(cond_ref, w_ref, b_ref, o_ref):
    cnd = cond_ref[...]
    s = cnd * jax.nn.sigmoid(cnd)
    o_ref[0] = jnp.dot(s, w_ref[0], precision=HIGHEST, preferred_element_type=F32) + b_ref[0]


def _ada_mod(cond8, w_ada, b_ada):
    depth, d, n6 = w_ada.shape
    tn = 1536
    return pl.pallas_call(
        _ada_kernel,
        grid=(depth, n6 // tn),
        in_specs=[pl.BlockSpec((8, d), lambda l, j: (0, 0)),
                  pl.BlockSpec((1, d, tn), lambda l, j: (l, 0, j)),
                  pl.BlockSpec((1, 1, tn), lambda l, j: (l, 0, j))],
        out_specs=pl.BlockSpec((1, 8, tn), lambda l, j: (l, 0, j)),
        out_shape=jax.ShapeDtypeStruct((depth, 8, n6), F32),
        compiler_params=_cparams(("parallel", "parallel")),
        name="ada_mod",
    )(cond8, w_ada, b_ada.reshape(depth, 1, n6))


def _inproj_kernel(x_ref, g_ref, mod_ref, w_ref, gq_ref, gkv_ref, hg_ref, cq_ref, ckv_ref, kpe_ref):
    x = x_ref[...]
    shift = mod_ref[0, :, 0:D_MODEL]
    scale = mod_ref[0, :, D_MODEL:2 * D_MODEL]
    h = (_rms_rows(x, g_ref[...]) * (1.0 + scale) + shift).astype(BF16)
    hg_ref[...] = jnp.dot(h, w_ref[:, 0:IN_HG], preferred_element_type=F32)
    cq = jnp.dot(h, w_ref[:, IN_CQ0:IN_CKV0], preferred_element_type=F32)
    cq_ref[...] = _rms_rows(cq, gq_ref[...]).astype(BF16)
    ckv = jnp.dot(h, w_ref[:, IN_CKV0:IN_KPE0], preferred_element_type=F32)
    ckv_ref[...] = _rms_rows(ckv, gkv_ref[...])
    kpe_ref[...] = jnp.dot(h, w_ref[:, IN_KPE0:IN_PAD], preferred_element_type=F32)


def _inproj(x, g, mod, w_pad, gq, gkv, seq_len, tm):
    n = x.shape[0]
    per_seq = seq_len // tm
    n_cond = mod.shape[0]
    cond_of = (lambda i: (0, 0, 0)) if n_cond == 1 else (lambda i: (i // per_seq, 0, 0))
    row = lambda i: (i, 0)
    const = lambda i: (0, 0)
    return pl.pallas_call(
        _inproj_kernel,
        grid=(n // tm,),
        in_specs=[pl.BlockSpec((tm, D_MODEL), row),
                  pl.BlockSpec((1, D_MODEL), const),
                  pl.BlockSpec((1, 1, 6 * D_MODEL), cond_of),
                  pl.BlockSpec((D_MODEL, IN_PAD), const),
                  pl.BlockSpec((1, MLA_Q_LORA), const),
                  pl.BlockSpec((1, MLA_KV_LORA), const)],
        out_specs=[pl.BlockSpec((tm, IN_HG), row),
                   pl.BlockSpec((tm, MLA_Q_LORA), row),
                   pl.BlockSpec((tm, MLA_KV_LORA), row),
                   pl.BlockSpec((tm, LANES), row)],
        out_shape=[jax.ShapeDtypeStruct((n, IN_HG), F32),
                   jax.ShapeDtypeStruct((n, MLA_Q_LORA), BF16),
                   jax.ShapeDtypeStruct((n, MLA_KV_LORA), F32),
                   jax.ShapeDtypeStruct((n, LANES), F32)],
        compiler_params=_cparams(("parallel",)),
        name="in_proj",
    )(x, g, mod, w_pad, gq, gkv)


def _hgrn_kernel(q_ref, ff_ref, fb_ref, v_ref, gt_ref, lbf_ref, lbb_ref, gn_ref, s0_ref,
                 out_ref, sfin_ref, acc_ref, accb_ref, *, seq_len):
    C = HG_CHUNK
    n_chunks = seq_len // C
    r2 = lax.broadcasted_iota(jnp.int32, (LANES, LANES), 0) // HG_D
    c2 = lax.broadcasted_iota(jnp.int32, (LANES, LANES), 1) // HG_D
    same_head = r2 == c2
    sel_bf = same_head.astype(BF16)
    ti = lax.broadcasted_iota(jnp.int32, (C, C), 0)
    ui = lax.broadcasted_iota(jnp.int32, (C, C), 1)
    tri_lo = (ui <= ti).astype(F32)
    tri_up = (ui >= ti).astype(F32)
    s3 = lax.broadcasted_iota(jnp.int32, (C, C, 1), 0)
    t3 = lax.broadcasted_iota(jnp.int32, (C, C, 1), 1)
    ones_c = jnp.ones((C, LANES), F32)

    def block_diag(s_a, s_b):
        z = jnp.zeros((HG_D, HG_D), F32)
        return jnp.concatenate([jnp.concatenate([s_a, z], axis=1),
                                jnp.concatenate([z, s_b], axis=1)], axis=0)

    def chunk_step(c, S, z_ref, lb_ref, forward):
        rows = pl.ds(pl.multiple_of(c * C, C), C)
        lb = lb_ref[...]
        f = jnp.maximum(lb + (1.0 - lb) * jax.nn.sigmoid(z_ref[rows, :]), FORGET_FLOOR)
        g = jnp.log(f)
        k = 1.0 - f
        q = q_ref[rows, :] * (HG_D ** -0.5)
        v = v_ref[rows, :]
        b = jnp.dot(tri_lo if forward else tri_up, g, precision=HIGHEST, preferred_element_type=F32)
        o_inter = _dot3(q * jnp.exp(b), S)
        mask = (t3 >= s3) if forward else (t3 <= s3)
        diff = b[None, :, :] - b[:, None, :]
        e = jnp.where(mask, jnp.exp(jnp.where(mask, diff, 0.0)), 0.0) * q[None, :, :] * k[:, None, :]
        a = _dot_sel(e.reshape(C * C, LANES), sel_bf).reshape(C, C, LANES)
        o_intra = jnp.sum(a * v[:, None, :], axis=0)
        b_end = b[C - 1:C, :] if forward else b[0:1, :]
        kk = k * jnp.exp(b_end - b)
        tdims = (((0,), (0,)), ((), ()))
        total = lax.dot_general(g, ones_c, tdims, precision=HIGHEST, preferred_element_type=F32)
        s_new = jnp.exp(total) * S + jnp.where(same_head, _dot3(kk, v, tdims), 0.0)
        return s_new, o_inter + o_intra

    s_f0 = block_diag(s0_ref[0, 0, 0], s0_ref[0, 0, 1])
    s_b0 = block_diag(s0_ref[0, 1, 0], s0_ref[0, 1, 1])

    def body(i, carry):
        s_f, s_b = carry
        cb = n_chunks - 1 - i
        s_f, o_f = chunk_step(i, s_f, ff_ref, lbf_ref, True)
        s_b, o_b = chunk_step(cb, s_b, fb_ref, lbb_ref, False)
        acc_ref[pl.ds(pl.multiple_of(i * C, C), C), :] = o_f
        accb_ref[pl.ds(pl.multiple_of(cb * C, C), C), :] = o_b
        return s_f, s_b

    s_f, s_b = lax.fori_loop(0, n_chunks, body, (s_f0, s_b0))

    sfin_ref[0, 0, 0] = s_f[0:HG_D, 0:HG_D]
    sfin_ref[0, 0, 1] = s_f[HG_D:, HG_D:]
    sfin_ref[0, 1, 0] = s_b[0:HG_D, 0:HG_D]
    sfin_ref[0, 1, 1] = s_b[HG_D:, HG_D:]

    o = acc_ref[...] + accb_ref[...]
    ms = _dot_sel(o * o, sel_bf) * (1.0 / HG_D)
    gt = gt_ref[...]
    out_ref[...] = (o * lax.rsqrt(ms + EPS) * gn_ref[...] * (gt * jax.nn.sigmoid(gt))).astype(BF16)


def _hgrn(hg, lb_f, lb_b, gn2, s0, batch, seq_len):
    n = hg.shape[0]
    pairs = HG_W // LANES
    part = lambda p: pl.BlockSpec((seq_len, LANES), lambda b, j, p=p: (b, p * pairs + j))
    lane_row = pl.BlockSpec((1, LANES), lambda b, j: (0, j))
    st = pl.BlockSpec((1, 2, 2, HG_D, HG_D), lambda b, j: (b, 0, j, 0, 0))
    return pl.pallas_call(
        functools.partial(_hgrn_kernel, seq_len=seq_len),
        grid=(batch, pairs),
        in_specs=[part(0), part(1), part(2), part(3), part(4), lane_row, lane_row,
                  pl.BlockSpec((1, LANES), lambda b, j: (0, 0)), st],
        out_specs=[pl.BlockSpec((seq_len, LANES), lambda b, j: (b, j)), st],
        out_shape=[jax.ShapeDtypeStruct((n, HG_W), BF16),
                   jax.ShapeDtypeStruct((batch, 2, HG_HEADS, HG_D, HG_D), F32)],
        scratch_shapes=[pltpu.VMEM((seq_len, LANES), F32), pltpu.VMEM((seq_len, LANES), F32)],
        compiler_params=_cparams(("parallel", "parallel")),
        name="hgrn2",
    )(hg, hg, hg, hg, hg, lb_f, lb_b, gn2, s0)


def _rope_lanes(x, cos, sin):
    return x * cos + pltpu.roll(x, 96, 1) * sin


def _qproj_kernel(cq_ref, w_ref, cos_ref, sin_ref, q_ref):
    cq = cq_ref[...]
    for h in range(MLA_HEADS):
        qh = jnp.dot(cq, w_ref[h], preferred_element_type=F32)
        q_ref[h] = (_rope_lanes(qh, cos_ref[...], sin_ref[...]) * MLA_SCALE).astype(BF16)


def _kvproj_kernel(ckv_ref, kpe_ref, w_ref, cos_ref, sin_ref, k_ref, kv_ref):
    ckv = ckv_ref[...].astype(BF16)
    kpe = _rope_lanes(kpe_ref[...], cos_ref[...], sin_ref[...])
    nope = lax.broadcasted_iota(jnp.int32, kpe.shape, 1) < MLA_NOPE
    for h in range(MLA_HEADS):
        kvh = jnp.dot(ckv, w_ref[h], preferred_element_type=F32)
        kv_ref[h] = kvh.astype(BF16)
        k_ref[h] = jnp.where(nope, kvh, kpe).astype(BF16)


def _qproj(cqn, wq, cos, sin, tm):
    n = cqn.shape[0]
    pos_blocks = cos.shape[0] // tm
    return pl.pallas_call(
        _qproj_kernel,
        grid=(n // tm,),
        in_specs=[pl.BlockSpec((tm, MLA_Q_LORA), lambda i: (i, 0)),
                  pl.BlockSpec((MLA_HEADS, MLA_Q_LORA, LANES), lambda i: (0, 0, 0)),
                  pl.BlockSpec((tm, LANES), lambda i: (i % pos_blocks, 0)),
                  pl.BlockSpec((tm, LANES), lambda i: (i % pos_blocks, 0))],
        out_specs=pl.BlockSpec((MLA_HEADS, tm, LANES), lambda i: (0, i, 0)),
        out_shape=jax.ShapeDtypeStruct((MLA_HEADS, n, LANES), BF16),
        compiler_params=_cparams(("parallel",)),
        name="mla_q_proj",
    )(cqn, wq, cos, sin)


def _kvproj(ckvn, kpe128, wkv, cos, sin, tm):
    n = ckvn.shape[0]
    pos_blocks = cos.shape[0] // tm
    hm = pl.BlockSpec((MLA_HEADS, tm, LANES), lambda i: (0, i, 0))
    return pl.pallas_call(
        _kvproj_kernel,
        grid=(n // tm,),
        in_specs=[pl.BlockSpec((tm, MLA_KV_LORA), lambda i: (i, 0)),
                  pl.BlockSpec((tm, LANES), lambda i: (i, 0)),
                  pl.BlockSpec((MLA_HEADS, MLA_KV_LORA, LANES), lambda i: (0, 0, 0)),
                  pl.BlockSpec((tm, LANES), lambda i: (i % pos_blocks, 0)),
                  pl.BlockSpec((tm, LANES), lambda i: (i % pos_blocks, 0))],
        out_specs=[hm, hm],
        out_shape=[jax.ShapeDtypeStruct((MLA_HEADS, n, LANES), BF16)] * 2,
        compiler_params=_cparams(("parallel",)),
        name="mla_kv_proj",
    )(ckvn, kpe128, wkv, cos, sin)


def _attn_kernel(q_ref, k_ref, kv_ref, o_ref):
    low = lax.broadcasted_iota(jnp.int32, (q_ref.shape[1], LANES), 1) < MLA_V
    nt = (((1,), (1,)), ((), ()))
    outs = []
    for h in range(MLA_HEADS):
        s = lax.dot_general(q_ref[h], k_ref[h], nt, preferred_element_type=F32)
        p = jnp.exp(s - jnp.max(s, axis=-1, keepdims=True))
        denom = jnp.sum(p, axis=-1, keepdims=True)
        o = jnp.dot(p.astype(BF16), kv_ref[h], preferred_element_type=F32) / denom
        outs.append(o)
    for j in range(MLA_HEADS // 2):
        pair = jnp.where(low, pltpu.roll(outs[2 * j], MLA_V, 1), outs[2 * j + 1])
        o_ref[:, j * LANES:(j + 1) * LANES] = pair.astype(BF16)


def _attention(q128, k128, kv128, batch, lq, lk, tq):
    n = q128.shape[1]
    qb = lq // tq
    kspec = pl.BlockSpec((MLA_HEADS, lk, LANES), lambda b, i: (0, b, 0))
    return pl.pallas_call(
        _attn_kernel,
        grid=(batch, qb),
        in_specs=[pl.BlockSpec((MLA_HEADS, tq, LANES), lambda b, i: (0, b * qb + i, 0)), kspec, kspec],
        out_specs=pl.BlockSpec((tq, MLA_HEADS * MLA_V), lambda b, i: (b * qb + i, 0)),
        out_shape=jax.ShapeDtypeStruct((n, MLA_HEADS * MLA_V), BF16),
        compiler_params=_cparams(("parallel", "parallel")),
        name="mla_attention",
    )(q128, k128, kv128)


def _outproj_kernel(x_ref, hg_ref, at_ref, w_ref, mod_ref, o_ref):
    gate = mod_ref[0, :, 2 * D_MODEL:3 * D_MODEL]
    mix = (jnp.dot(hg_ref[...], w_ref[0:HG_W, :], preferred_element_type=F32)
           + jnp.dot(at_ref[...], w_ref[HG_W:, :], preferred_element_type=F32))
    o_ref[...] = x_ref[...] + gate * mix


def _outproj(x, hg_out, attn, w_out, mod, seq_len, tm):
    n = x.shape[0]
    per_seq = seq_len // tm
    n_cond = mod.shape[0]
    cond_of = (lambda i: (0, 0, 0)) if n_cond == 1 else (lambda i: (i // per_seq, 0, 0))
    row = lambda i: (i, 0)
    return pl.pallas_call(
        _outproj_kernel,
        grid=(n // tm,),
        in_specs=[pl.BlockSpec((tm, D_MODEL), row),
                  pl.BlockSpec((tm, HG_W), row),
                  pl.BlockSpec((tm, MLA_HEADS * MLA_V), row),
                  pl.BlockSpec((2 * HG_W, D_MODEL), lambda i: (0, 0)),
                  pl.BlockSpec((1, 1, 6 * D_MODEL), cond_of)],
        out_specs=pl.BlockSpec((tm, D_MODEL), row),
        out_shape=jax.ShapeDtypeStruct((n, D_MODEL), F32),
        compiler_params=_cparams(("parallel",)),
        name="out_proj",
    )(x, hg_out, attn, w_out, mod)


_CAND = [(a, b) for a in range(PEER_TOPK) for b in range(PEER_TOPK) if (a + 1) * (b + 1) <= PEER_TOPK]
_CAND_PAD = -(-len(_CAND) // 8) * 8


def _topk_rows(work, k):
    rows = lax.broadcasted_iota(jnp.int32, work.shape, 0).astype(F32)
    n_rows = float(work.shape[0])
    vals, idxs = [], []
    for _ in range(k):
        m = jnp.max(work, axis=0, keepdims=True)
        sel = jnp.min(jnp.where(work == m, rows, n_rows), axis=0, keepdims=True)
        vals.append(m)
        idxs.append(sel)
        work = jnp.where(rows == sel, -jnp.inf, work)
    return jnp.concatenate(vals, axis=0), jnp.concatenate(idxs, axis=0)


def _peerq_kernel(x_ref, g_ref, mod_ref, w_ref, k1_ref, k2_ref, h2_ref, idx_ref, gate_ref, *, idx_offset):
    idx_rows, gate_rows = [], []
    x = x_ref[...]
    shift = mod_ref[0, :, 3 * D_MODEL:4 * D_MODEL]
    scale = mod_ref[0, :, 4 * D_MODEL:5 * D_MODEL]
    h2 = _rms_rows(x, g_ref[...]) * (1.0 + scale) + shift
    h2b = h2.astype(BF16)
    bits = lax.bitcast_convert_type(h2b.astype(F32), jnp.uint32)
    h2_ref[...] = (bits[:, :PACK_W] >> 16) | bits[:, PACK_W:]
    qp = jnp.dot(h2b, w_ref[...], preferred_element_type=F32)
    nt = (((1,), (1,)), ((), ()))
    half = PEER_DQ // 2
    for h in range(PEER_HEADS):
        q1 = qp[:, h * PEER_DQ:h * PEER_DQ + half]
        q2 = qp[:, h * PEER_DQ + half:(h + 1) * PEER_DQ]
        s1 = _dot3(k1_ref[h], q1, nt)
        s2 = _dot3(k2_ref[h], q2, nt)
        v1, i1 = _topk_rows(s1, PEER_TOPK)
        v2, i2 = _topk_rows(s2, PEER_TOPK)
        cs = [v1[a:a + 1, :] + v2[b:b + 1, :] for a, b in _CAND]
        ci = [i1[a:a + 1, :] * PEER_NKEYS + i2[b:b + 1, :] for a, b in _CAND]
        n_pad = _CAND_PAD - len(_CAND)
        if n_pad:
            cs.append(jnp.full((n_pad, x.shape[0]), -jnp.inf, F32))
            ci.append(jnp.zeros((n_pad, x.shape[0]), F32))
        cand_s = jnp.concatenate(cs, axis=0)
        cand_i = jnp.concatenate(ci, axis=0)
        best_s, best_pos = _topk_rows(cand_s, PEER_TOPK)
        rows = lax.broadcasted_iota(jnp.int32, cand_s.shape, 0).astype(F32)
        picked = [jnp.sum(jnp.where(rows == best_pos[r:r + 1, :], cand_i, 0.0), axis=0, keepdims=True)
                  for r in range(PEER_TOPK)]
        idx_rows.extend(picked)
        ex = jnp.exp(best_s - best_s[0:1, :])
        gate_rows.append(ex / jnp.sum(ex, axis=0, keepdims=True))
    idx_ref[...] = jnp.concatenate(idx_rows, axis=0).T.astype(jnp.int32) + idx_offset
    gate_ref[...] = jnp.concatenate(gate_rows, axis=0).T


def _peerq(x, g, mod, wq, k1, k2, seq_len, tm, idx_offset):
    n = x.shape[0]
    per_seq = seq_len // tm
    n_cond = mod.shape[0]
    cond_of = (lambda i: (0, 0, 0)) if n_cond == 1 else (lambda i: (i // per_seq, 0, 0))
    kspec = pl.BlockSpec((PEER_HEADS, PEER_NKEYS, PEER_DQ // 2), lambda i: (0, 0, 0))
    tk = PEER_HEADS * PEER_TOPK
    return pl.pallas_call(
        functools.partial(_peerq_kernel, idx_offset=idx_offset),
        grid=(n // tm,),
        in_specs=[pl.BlockSpec((tm, D_MODEL), lambda i: (i, 0)),
                  pl.BlockSpec((1, D_MODEL), lambda i: (0, 0)),
                  pl.BlockSpec((1, 1, 6 * D_MODEL), cond_of),
                  pl.BlockSpec((D_MODEL, PEER_HEADS * PEER_DQ), lambda i: (0, 0)),
                  kspec, kspec],
        out_specs=[pl.BlockSpec((tm, D_MODEL // 2), lambda i: (i, 0)),
                   pl.BlockSpec((tm, tk), lambda i: (i, 0)),
                   pl.BlockSpec((tm, tk), lambda i: (i, 0))],
        out_shape=[jax.ShapeDtypeStruct((n, D_MODEL // 2), jnp.uint32),
                   jax.ShapeDtypeStruct((n, tk), jnp.int32),
                   jax.ShapeDtypeStruct((n, tk), F32)],
        compiler_params=_cparams(("parallel",)),
        name="peer_retrieve",
    )(x, g, mod, wq, k1, k2)


SC_CORES = 2
SC_SUBCORES = 16
SC_LANES = 16
SC_TILES = SC_CORES * SC_SUBCORES
SC_NBUF = 4
PEER_TK = PEER_HEADS * PEER_TOPK
D_VREGS = D_MODEL // SC_LANES
BF_RUN = 4


PACK_W = D_MODEL // 2


def _pack_bf16_pairs(table):
    bits = lax.bitcast_convert_type(table.astype(BF16), jnp.uint16).astype(jnp.uint32)
    return bits[:, :PACK_W] | (bits[:, PACK_W:] << 16)


def _unpack_pair(word):
    lo = lax.bitcast_convert_type(word << 16, F32)
    hi = lax.bitcast_convert_type(word & jnp.uint32(0xFFFF0000), F32)
    return lo, hi


def _sc_mesh():
    return plsc.VectorSubcoreMesh(core_axis_name="c", subcore_axis_name="s")


def _sc_params():
    return pltpu.CompilerParams(needs_layout_passes=False)


def _sc_token_base(tokens_per_tile):
    return (lax.axis_index("s") * SC_CORES + lax.axis_index("c")) * tokens_per_tile


def _peer_u_body(h2_hbm, idx_hbm, u_hbm, act_hbm, idx_v, act_v, x_v, rows_v, sem_x, sem_g, *, tpt):
    base = _sc_token_base(tpt)
    n_items = tpt * PEER_HEADS
    pltpu.sync_copy(idx_hbm.at[pl.ds(base, tpt)], idx_v)

    def gather(i, slot):
        t, h = i // PEER_HEADS, i % PEER_HEADS
        return pltpu.make_async_copy(u_hbm.at[idx_v.at[t, pl.ds(h * PEER_TOPK, PEER_TOPK)]],
                                     rows_v.at[slot], sem_g.at[slot])

    def x_copy(t):
        return pltpu.make_async_copy(h2_hbm.at[base + t], x_v.at[t % 2], sem_x.at[t % 2])

    x_copy(0).start()
    for b in range(SC_NBUF - 1):
        gather(b, b).start()
    lane = lax.iota(jnp.int32, SC_LANES)
    zero = jnp.zeros((SC_LANES,), F32)

    @pl.loop(0, n_items, step=SC_NBUF)
    def _(i0):
        for b in range(SC_NBUF):
            i = i0 + b
            t, h = i // PEER_HEADS, i % PEER_HEADS

            @pl.when(h == 0)
            def _():
                x_copy(t).wait()

                @pl.when(t + 1 < tpt)
                def _():
                    x_copy(t + 1).start()

            @pl.when(i + SC_NBUF - 1 < n_items)
            def _():
                gather(i + SC_NBUF - 1, (b + SC_NBUF - 1) % SC_NBUF).start()

            gather(i, b).wait()
            xs = t % 2

            def cbody(g, accs):
                parts = [None] * PEER_TOPK
                for cc in range(BF_RUN):
                    col = pl.ds((g * BF_RUN + cc) * SC_LANES, SC_LANES)
                    xb = plsc.bitcast(x_v[xs, col], BF16)
                    for j in range(PEER_TOPK):
                        p = xb * plsc.bitcast(rows_v[b, j, col], BF16)
                        parts[j] = p if cc == 0 else parts[j] + p
                out = []
                for j, a in enumerate(accs):
                    lo, hi = _unpack_pair(plsc.bitcast(parts[j], jnp.uint32))
                    out.append(a + lo + hi)
                return tuple(out)

            accs = plsc.parallel_loop(0, PACK_W // SC_LANES // BF_RUN, carry=(zero,) * PEER_TOPK)(cbody)
            vec = zero
            for j in range(PEER_TOPK):
                vec = jnp.where(lane == j, jnp.sum(accs[j]), vec)
            act_v[t, pl.ds(h * PEER_TOPK, PEER_TOPK)] = vec

    pltpu.sync_copy(act_v, act_hbm.at[pl.ds(base, tpt)])


def _peer_v_body(w_hbm, idx_hbm, v_hbm, y_hbm, idx_v, w_v, y_v, rows_v, sem_y, sem_g, *, tpt):
    base = _sc_token_base(tpt)
    n_items = tpt * PEER_HEADS
    pltpu.sync_copy(idx_hbm.at[pl.ds(base, tpt)], idx_v)
    pltpu.sync_copy(w_hbm.at[pl.ds(base, tpt)], w_v)

    def gather(i, slot):
        t, h = i // PEER_HEADS, i % PEER_HEADS
        return pltpu.make_async_copy(v_hbm.at[idx_v.at[t, pl.ds(h * PEER_TOPK, PEER_TOPK)]],
                                     rows_v.at[slot], sem_g.at[slot])

    def y_copy(t):
        return pltpu.make_async_copy(y_v.at[t % 2], y_hbm.at[base + t], sem_y.at[t % 2])

    for b in range(SC_NBUF - 1):
        gather(b, b).start()
    lane = lax.iota(jnp.int32, SC_LANES)
    zero = jnp.zeros((SC_LANES,), F32)

    @pl.loop(0, n_items, step=SC_NBUF)
    def _(i0):
        for b in range(SC_NBUF):
            i = i0 + b
            t, h = i // PEER_HEADS, i % PEER_HEADS
            ys = t % 2

            @pl.when(i + SC_NBUF - 1 < n_items)
            def _():
                gather(i + SC_NBUF - 1, (b + SC_NBUF - 1) % SC_NBUF).start()

            @pl.when(h == 0)
            def _():
                @pl.when(t >= 2)
                def _():
                    y_copy(t - 2).wait()

                @pl.loop(0, D_VREGS)
                def _(c):
                    y_v[ys, pl.ds(c * SC_LANES, SC_LANES)] = zero

            gather(i, b).wait()
            wv = w_v[t, pl.ds(h * PEER_TOPK, PEER_TOPK)]
            ws = [plsc.bitcast(jnp.broadcast_to(jnp.sum(jnp.where(lane == j, wv, 0)), (SC_LANES,)), BF16)
                  for j in range(PEER_TOPK)]

            @plsc.parallel_loop(0, PACK_W // SC_LANES)
            def _(c):
                col = pl.ds(c * SC_LANES, SC_LANES)
                col_hi = pl.ds(PACK_W + c * SC_LANES, SC_LANES)
                terms = [ws[j] * plsc.bitcast(rows_v[b, j, col], BF16) for j in range(PEER_TOPK)]
                while len(terms) > 1:
                    terms = [terms[k] + terms[k + 1] for k in range(0, len(terms), 2)]
                lo, hi = _unpack_pair(plsc.bitcast(terms[0], jnp.uint32))
                y_v[ys, col] = y_v[ys, col] + lo
                y_v[ys, col_hi] = y_v[ys, col_hi] + hi

            @pl.when(h == PEER_HEADS - 1)
            def _():
                y_copy(t).start()

    y_copy(tpt - 2).wait()
    y_copy(tpt - 1).wait()


def _sc_scratch(tpt, side_dtype, token_buf):
    return [pltpu.VMEM((tpt, PEER_TK), jnp.int32),
            pltpu.VMEM((tpt, PEER_TK), side_dtype),
            token_buf,
            pltpu.VMEM((SC_NBUF, PEER_TOPK, PACK_W), jnp.uint32),
            pltpu.SemaphoreType.DMA((2,)),
            pltpu.SemaphoreType.DMA((SC_NBUF,))]


def _peer_u(h2, idx, u_all):
    n = h2.shape[0]
    tpt = n // SC_TILES
    return pl.kernel(
        functools.partial(_peer_u_body, tpt=tpt),
        out_type=jax.ShapeDtypeStruct((n, PEER_TK), F32),
        mesh=_sc_mesh(),
        scratch_types=_sc_scratch(tpt, F32, pltpu.VMEM((2, PACK_W), jnp.uint32)),
        compiler_params=_sc_params(),
        name="peer_u_sc",
    )(h2, idx, u_all)


def _peer_v(w, idx, v_all):
    n = w.shape[0]
    tpt = n // SC_TILES
    return pl.kernel(
        functools.partial(_peer_v_body, tpt=tpt),
        out_type=jax.ShapeDtypeStruct((n, D_MODEL), F32),
        mesh=_sc_mesh(),
        scratch_types=_sc_scratch(tpt, jnp.int32, pltpu.VMEM((2, D_MODEL), F32)),
        compiler_params=_sc_params(),
        name="peer_v_sc",
    )(w, idx, v_all)


def _peer_gelu_kernel(gate_ref, act_ref, w_ref):
    act = act_ref[...]
    w = gate_ref[...] * (0.5 * act * (1.0 + lax.erf(act * (2.0 ** -0.5))))
    bits = lax.bitcast_convert_type(w.astype(BF16).astype(F32), jnp.uint32)
    w_ref[...] = lax.bitcast_convert_type(bits | (bits >> 16), jnp.int32)


def _peer_gelu(gate, act, tm):
    n = gate.shape[0]
    spec = pl.BlockSpec((tm, PEER_TK), lambda i: (i, 0))
    return pl.pallas_call(
        _peer_gelu_kernel,
        grid=(n // tm,),
        in_specs=[spec, spec],
        out_specs=spec,
        out_shape=jax.ShapeDtypeStruct((n, PEER_TK), jnp.int32),
        compiler_params=_cparams(("parallel",)),
        name="peer_gelu",
    )(gate, act)


def _peer_residual_kernel(x_ref, y_ref, mod_ref, o_ref):
    o_ref[...] = x_ref[...] + mod_ref[0, :, 5 * D_MODEL:6 * D_MODEL] * y_ref[...]


def _peer_residual(x, y, mod, seq_len, tm):
    n = x.shape[0]
    per_seq = seq_len // tm
    n_cond = mod.shape[0]
    cond_of = (lambda i: (0, 0, 0)) if n_cond == 1 else (lambda i: (i // per_seq, 0, 0))
    row = pl.BlockSpec((tm, D_MODEL), lambda i: (i, 0))
    return pl.pallas_call(
        _peer_residual_kernel,
        grid=(n // tm,),
        in_specs=[row, row, pl.BlockSpec((1, 1, 6 * D_MODEL), cond_of)],
        out_specs=row,
        out_shape=jax.ShapeDtypeStruct((n, D_MODEL), F32),
        compiler_params=_cparams(("parallel",)),
        name="peer_residual",
    )(x, y, mod)


def _final_norm_kernel(x_ref, g_ref, o_ref):
    o_ref[...] = _rms_rows(x_ref[...], g_ref[...])


def _final_norm(x, g, tm):
    n = x.shape[0]
    return pl.pallas_call(
        _final_norm_kernel,
        grid=(n // tm,),
        in_specs=[pl.BlockSpec((tm, D_MODEL), lambda i: (i, 0)), pl.BlockSpec((1, D_MODEL), lambda i: (0, 0))],
        out_specs=pl.BlockSpec((tm, D_MODEL), lambda i: (i, 0)),
        out_shape=jax.ShapeDtypeStruct((n, D_MODEL), F32),
        compiler_params=_cparams(("parallel",)),
        name="final_norm",
    )(x, g)


def _rot_half_columns(w):
    half = ROPE_AXIS_DIM // 2
    w4 = w.reshape(w.shape[:-1] + (2, 2, half))
    return jnp.stack([-w4[..., 1, :], w4[..., 0, :]], axis=-2).reshape(w.shape)


def _rope_tables(seq_len, rotate):
    cos = np.zeros((seq_len, LANES), np.float32)
    sin = np.zeros((seq_len, LANES), np.float32)
    cos[:, :MLA_NOPE + MLA_ROPE] = 1.0
    if rotate:
        pos = np.arange(seq_len)
        half = ROPE_AXIS_DIM // 2
        inv = (1.0 / (ROPE_BASE ** (np.arange(0, ROPE_AXIS_DIM, 2, dtype=np.float32) / ROPE_AXIS_DIM))).astype(np.float32)
        for axis, p in enumerate(((pos // GRID_W).astype(np.float32), (pos % GRID_W).astype(np.float32))):
            ang = p[:, None] * inv[None, :]
            for part in range(2):
                lo = MLA_NOPE + axis * ROPE_AXIS_DIM + part * half
                cos[:, lo:lo + half] = np.cos(ang)
                sin[:, lo:lo + half] = np.sin(ang)
    return jnp.asarray(cos), jnp.asarray(sin)


def _prep_layer_weights(l, w_in, mla_wqb, mla_wkvb, w_out, peer_wq):
    wi = w_in[l]
    kpe_w = wi[:, IN_KPE0:IN_KPE0 + MLA_ROPE]
    w_pad = jnp.concatenate(
        [wi[:, :IN_KPE0], jnp.zeros((D_MODEL, MLA_NOPE), F32), kpe_w, _rot_half_columns(kpe_w)], axis=1).astype(BF16)
    wq = mla_wqb[l].reshape(MLA_Q_LORA, MLA_HEADS, MLA_NOPE + MLA_ROPE)
    wq_pe = wq[..., MLA_NOPE:]
    wq128 = jnp.concatenate([wq, _rot_half_columns(wq_pe)], axis=-1).transpose(1, 0, 2).astype(BF16)
    wkv128 = mla_wkvb[l].reshape(MLA_KV_LORA, MLA_HEADS, MLA_NOPE + MLA_V).transpose(1, 0, 2).astype(BF16)
    return w_pad, wq128, wkv128, w_out[l].astype(BF16), peer_wq[l].astype(BF16)


def _trunk_layer(x, mod, l, W, P, lb, batch, seq_len, ctx, rope):
    w_pad, wq128, wkv128, wo, wpq = W
    tm = 256
    hg, cqn, ckvn, kpe128 = _inproj(x, P['norm_mix'][l][None], mod, w_pad, P['mla_q_norm'][l][None],
                                    P['mla_kv_norm'][l][None], seq_len, tm)
    if ctx is None:
        s0 = jnp.zeros((batch, 2, HG_HEADS, HG_D, HG_D), F32)
    else:
        s0 = ctx[2]
    gn2 = jnp.tile(P['hgrn_norm'][l], 2)[None]
    hg_out, s_fin = _hgrn(hg, lb[0, l][None], lb[1, l][None], gn2, s0, batch, seq_len)

    cos, sin = rope
    q128 = _qproj(cqn, wq128, cos, sin, tm)
    k128, kv128 = _kvproj(ckvn, kpe128, wkv128, cos, sin, tm)
    lk = seq_len
    if ctx is not None:
        past = ctx[0].shape[1]
        cos_c, sin_c = _rope_tables(past, False)
        kpe_c = jnp.pad(ctx[1].reshape(batch * past, MLA_ROPE), ((0, 0), (MLA_NOPE, LANES - MLA_NOPE - MLA_ROPE)))
        k_c, kv_c = _kvproj(ctx[0].reshape(batch * past, MLA_KV_LORA), kpe_c, wkv128, cos_c, sin_c, past)
        join = lambda a, b: jnp.concatenate(
            [a.reshape(MLA_HEADS, batch, past, LANES), b.reshape(MLA_HEADS, batch, seq_len, LANES)],
            axis=2).reshape(MLA_HEADS, batch * (past + seq_len), LANES)
        k128, kv128 = join(k_c, k128), join(kv_c, kv128)
        lk = past + seq_len
    attn = _attention(q128, k128, kv128, batch, seq_len, lk, 256)
    x = _outproj(x, hg_out, attn, wo, mod, seq_len, tm)

    keys = P['peer_keys'][l]
    n_experts = P['peer_u'].shape[1]
    h2, idx, gate = _peerq(x, P['norm_ffn'][l][None], mod, wpq, keys[0], keys[1], seq_len, tm, l * n_experts)
    act = _peer_u(h2, idx, P['peer_u_packed'])
    w = _peer_gelu(gate, act, tm)
    y = _peer_v(w, idx, P['peer_v_packed'])
    x = _peer_residual(x, y, mod, seq_len, tm)
    return x, (ckvn, kpe128[:, MLA_NOPE:MLA_NOPE + MLA_ROPE], s_fin)


def kernel(x_prompt, x_sample, cache_ckv, cache_kpe, state_hgrn, c, c_ctx, w_ada, b_ada, norm_mix, w_in,
           hgrn_lb_logits, hgrn_norm, mla_q_norm, mla_wqb, mla_kv_norm, mla_wkvb, w_out, norm_ffn,
           peer_wq, peer_keys, peer_u, peer_v, final_norm):
    P = {'norm_mix': norm_mix, 'hgrn_norm': hgrn_norm, 'mla_q_norm': mla_q_norm, 'mla_kv_norm': mla_kv_norm,
         'norm_ffn': norm_ffn, 'peer_keys': peer_keys, 'peer_u': peer_u,
         'peer_u_packed': _pack_bf16_pairs(peer_u.reshape(-1, D_MODEL)),
         'peer_v_packed': _pack_bf16_pairs(peer_v.reshape(-1, D_MODEL))}
    depth = w_in.shape[0]
    bp, lp, _ = x_prompt.shape
    bs, ls, _ = x_sample.shape

    p = jax.nn.softmax(hgrn_lb_logits.astype(F32), axis=1)
    lb = jnp.cumsum(p, axis=1) - p[:, :1]

    cond8 = jnp.zeros((8, D_MODEL), F32).at[0].set(c_ctx).at[1:1 + bs].set(c)
    mod_all = _ada_mod(cond8, w_ada, b_ada)

    rope_p = _rope_tables(lp, False)
    rope_s = _rope_tables(ls, True)

    xp = x_prompt.reshape(bp * lp, D_MODEL)
    xs = x_sample.reshape(bs * ls, D_MODEL)
    ckvs, kpes, states = [], [], []
    for l in range(depth):
        W = _prep_layer_weights(l, w_in, mla_wqb, mla_wkvb, w_out, peer_wq)
        mod_p = mod_all[l, 0:1][:, None, :]
        mod_s = mod_all[l, 1:1 + bs][:, None, :]
        xp, (ckv_l, kpe_l, st_l) = _trunk_layer(xp, mod_p, l, W, P, lb, bp, lp, None, rope_p)
        ckvs.append(ckv_l.reshape(bp, lp, MLA_KV_LORA))
        kpes.append(kpe_l.reshape(bp, lp, MLA_ROPE))
        states.append(st_l)
        xs, _ = _trunk_layer(xs, mod_s, l, W, P, lb, bs, ls,
                             (cache_ckv[:, l], cache_kpe[:, l], state_hgrn[:, l]), rope_s)

    y_prompt = _final_norm(xp, final_norm[None], 256).reshape(bp, lp, D_MODEL)
    y_sample = _final_norm(xs, final_norm[None], 256).reshape(bs, ls, D_MODEL)
    return (y_prompt, y_sample, jnp.stack(ckvs, axis=1), jnp.stack(kpes, axis=1), jnp.stack(states, axis=1))
```

```python
import functools
import math

import numpy as np
import jax
import jax.numpy as jnp
from jax import lax
from jax.experimental import pallas as pl
from jax.experimental.pallas import tpu as pltpu
from jax.experimental.pallas import tpu_sc as plsc

F32 = jnp.float32
BF16 = jnp.bfloat16
HIGHEST = lax.Precision.HIGHEST

D_MODEL = 1024
EPS = 1e-6
GRID_W = 64
HG_HEADS = 8
HG_D = 64
HG_CHUNK = 32
HG_W = HG_HEADS * HG_D
FORGET_FLOOR = 1e-6
MLA_HEADS = 8
MLA_Q_LORA = 384
MLA_KV_LORA = 256
MLA_NOPE = 64
MLA_ROPE = 32
MLA_V = 64
MLA_SCALE = (MLA_NOPE + MLA_ROPE) ** -0.5
ROPE_AXIS_DIM = MLA_ROPE // 2
ROPE_BASE = 10000.0
PEER_HEADS = 8
PEER_NKEYS = 128
PEER_DQ = 256
PEER_TOPK = 16

LANES = 128
VMEM_LIMIT = 48 * 1024 * 1024

IN_HG = 5 * HG_W
IN_CQ0 = IN_HG
IN_CKV0 = IN_CQ0 + MLA_Q_LORA
IN_KPE0 = IN_CKV0 + MLA_KV_LORA
IN_PAD = IN_KPE0 + LANES


def _cparams(sem):
    return pltpu.CompilerParams(dimension_semantics=sem, vmem_limit_bytes=VMEM_LIMIT)


def _split_bf16(a):
    hi = a.astype(BF16)
    lo = (a - hi.astype(F32)).astype(BF16)
    return hi, lo


def _dot3(a, b, dims=(((1,), (0,)), ((), ()))):
    ah, al = _split_bf16(a)
    bh, bl = _split_bf16(b)
    d = functools.partial(lax.dot_general, dimension_numbers=dims, preferred_element_type=F32)
    return d(ah, bh) + d(al, bh) + d(ah, bl)


def _dot_sel(a, sel_bf):
    ah, al = _split_bf16(a)
    return (jnp.dot(ah, sel_bf, preferred_element_type=F32)
            + jnp.dot(al, sel_bf, preferred_element_type=F32))


def _rms_rows(x, g):
    return x * lax.rsqrt(jnp.mean(x * x, axis=-1, keepdims=True) + EPS) * g


def _ada_kernel(cond_ref, w_ref, b_ref, o_ref):
    cnd = cond_ref[...]
    s = cnd * jax.nn.sigmoid(cnd)
    o_ref[0] = jnp.dot(s, w_ref[0], precision=HIGHEST, preferred_element_type=F32) + b_ref[0]


def _ada_mod(cond8, w_ada, b_ada):
    depth, d, n6 = w_ada.shape
    tn = 1536
    return pl.pallas_call(
        _ada_kernel,
        grid=(depth, n6 // tn),
        in_specs=[pl.BlockSpec((8, d), lambda l, j: (0, 0)),
                  pl.BlockSpec((1, d, tn), lambda l, j: (l, 0, j)),
                  pl.BlockSpec((1, 1, tn), lambda l, j: (l, 0, j))],
        out_specs=pl.BlockSpec((1, 8, tn), lambda l, j: (l, 0, j)),
        out_shape=jax.ShapeDtypeStruct((depth, 8, n6), F32),
        compiler_params=_cparams(("parallel", "parallel")),
        name="ada_mod",
    )(cond8, w_ada, b_ada.reshape(depth, 1, n6))


def _inproj_kernel(x_ref, g_ref, mod_ref, w_ref, gq_ref, gkv_ref, hg_ref, cq_ref, ckv_ref, kpe_ref):
    x = x_ref[...]
    shift = mod_ref[0, :, 0:D_MODEL]
    scale = mod_ref[0, :, D_MODEL:2 * D_MODEL]
    h = (_rms_rows(x, g_ref[...]) * (1.0 + scale) + shift).astype(BF16)
    hg_ref[...] = jnp.dot(h, w_ref[:, 0:IN_HG], preferred_element_type=F32)
    cq = jnp.dot(h, w_ref[:, IN_CQ0:IN_CKV0], preferred_element_type=F32)
    cq_ref[...] = _rms_rows(cq, gq_ref[...]).astype(BF16)
    ckv = jnp.dot(h, w_ref[:, IN_CKV0:IN_KPE0], preferred_element_type=F32)
    ckv_ref[...] = _rms_rows(ckv, gkv_ref[...])
    kpe_ref[...] = jnp.dot(h, w_ref[:, IN_KPE0:IN_PAD], preferred_element_type=F32)


def _inproj(x, g, mod, w_pad, gq, gkv, seq_len, tm):
    n = x.shape[0]
    per_seq = seq_len // tm
    n_cond = mod.shape[0]
    cond_of = (lambda i: (0, 0, 0)) if n_cond == 1 else (lambda i: (i // per_seq, 0, 0))
    row = lambda i: (i, 0)
    const = lambda i: (0, 0)
    return pl.pallas_call(
        _inproj_kernel,
        grid=(n // tm,),
        in_specs=[pl.BlockSpec((tm, D_MODEL), row),
                  pl.BlockSpec((1, D_MODEL), const),
                  pl.BlockSpec((1, 1, 6 * D_MODEL), cond_of),
                  pl.BlockSpec((D_MODEL, IN_PAD), const),
                  pl.BlockSpec((1, MLA_Q_LORA), const),
                  pl.BlockSpec((1, MLA_KV_LORA), const)],
        out_specs=[pl.BlockSpec((tm, IN_HG), row),
                   pl.BlockSpec((tm, MLA_Q_LORA), row),
                   pl.BlockSpec((tm, MLA_KV_LORA), row),
                   pl.BlockSpec((tm, LANES), row)],
        out_shape=[jax.ShapeDtypeStruct((n, IN_HG), F32),
                   jax.ShapeDtypeStruct((n, MLA_Q_LORA), BF16),
                   jax.ShapeDtypeStruct((n, MLA_KV_LORA), F32),
                   jax.ShapeDtypeStruct((n, LANES), F32)],
        compiler_params=_cparams(("parallel",)),
        name="in_proj",
    )(x, g, mod, w_pad, gq, gkv)


HG_SUB = 8
HG_UNROLL = 2


def _hgrn_kernel(q_ref, ff_ref, fb_ref, v_ref, gt_ref, lbf_ref, lbb_ref, gn_ref, s0_ref,
                 out_ref, sfin_ref, qt_ref, acc_ref, u_ref, d_ref, *, seq_len):
    C, SB = HG_CHUNK, HG_SUB
    n_sub = C // SB
    n_chunks = seq_len // C
    r2 = lax.broadcasted_iota(jnp.int32, (LANES, LANES), 0) // HG_D
    c2 = lax.broadcasted_iota(jnp.int32, (LANES, LANES), 1) // HG_D
    same_head = r2 == c2
    sel_bf = same_head.astype(BF16)
    ti = lax.broadcasted_iota(jnp.int32, (C, C), 0)
    ui = lax.broadcasted_iota(jnp.int32, (C, C), 1)
    tri = ((ui <= ti).astype(F32), (ui >= ti).astype(F32))
    s3 = lax.broadcasted_iota(jnp.int32, (SB, SB, 1), 0)
    t3 = lax.broadcasted_iota(jnp.int32, (SB, SB, 1), 1)
    diag_mask = (t3 >= s3, t3 <= s3)
    key_row = lax.broadcasted_iota(jnp.int32, (C, 1), 0)
    lane1 = lax.broadcasted_iota(jnp.int32, (1, LANES), 1)
    head_lane = ((lane1 < HG_D).astype(F32), (lane1 >= HG_D).astype(F32))
    ones_c = jnp.ones((C, LANES), F32)
    tdims = (((0,), (0,)), ((), ()))
    nt = (((1,), (1,)), ((), ()))
    z_refs, lb_refs = (ff_ref, fb_ref), (lbf_ref, lbb_ref)

    off_blocks = (tuple(range(1, n_sub)), tuple(range(0, n_sub - 1)))
    n_off = n_sub - 1
    sr = lax.broadcasted_iota(jnp.int32, (n_off * 2 * SB, n_off * C), 0) // (2 * SB)
    sc = lax.broadcasted_iota(jnp.int32, (n_off * 2 * SB, n_off * C), 1) // C
    own_block = sr == sc

    units = [(ci, d) for ci in range(HG_UNROLL) for d in range(2)]

    def prep_body(it, carry):
        cs = [it * HG_UNROLL + ci for ci in range(HG_UNROLL)]
        rows = [pl.ds(pl.multiple_of(c * C, C), C) for c in cs]
        q = [q_ref[r, :] * (HG_D ** -0.5) for r in rows]
        v = [v_ref[r, :] for r in rows]
        v_bf = [x.astype(BF16) for x in v]
        v_rep = [jnp.concatenate([x] * n_off, axis=0) for x in v_bf]
        gs, ks = [], []
        for ci, d in units:
            lb = lb_refs[d][...]
            f = jnp.maximum(lb + (1.0 - lb) * jax.nn.sigmoid(z_refs[d][rows[ci], :]), FORGET_FLOOR)
            gs.append(jnp.log(f))
            ks.append(1.0 - f)
        bs = [jnp.dot(tri[d], gs[u], precision=HIGHEST, preferred_element_type=F32) for u, (ci, d) in enumerate(units)]
        totals = [lax.dot_general(g, ones_c, tdims, precision=HIGHEST, preferred_element_type=F32) for g in gs]
        incs, sums, scores = [], [], []
        for u, (ci, d) in enumerate(units):
            b_end = bs[u][C - 1:C, :] if d == 0 else bs[u][0:1, :]
            kk = (ks[u] * jnp.exp(b_end - bs[u])).astype(BF16)
            incs.append(lax.dot_general(kk, v_bf[ci], tdims, preferred_element_type=F32))
        for u, (ci, d) in enumerate(units):
            blocks = []
            for blk in range(n_sub):
                sl = slice(blk * SB, (blk + 1) * SB)
                bi = bs[u][sl]
                diff = bi[None, :, :] - bi[:, None, :]
                e = jnp.where(diag_mask[d], jnp.exp(jnp.where(diag_mask[d], diff, 0.0)), 0.0)
                blocks.append((e * q[ci][sl][None, :, :] * ks[u][sl][:, None, :]).reshape(SB * SB, LANES))
            sums.append(jnp.dot(jnp.concatenate(blocks, axis=0).astype(BF16), sel_bf, preferred_element_type=F32))
        for u, (ci, d) in enumerate(units):
            lhs, khs = [], []
            for blk in off_blocks[d]:
                sl = slice(blk * SB, (blk + 1) * SB)
                if d == 0:
                    r = bs[u][blk * SB - 1:blk * SB, :]
                    keys = key_row < blk * SB
                else:
                    r = bs[u][(blk + 1) * SB:(blk + 1) * SB + 1, :]
                    keys = key_row >= (blk + 1) * SB
                qh = q[ci][sl] * jnp.exp(bs[u][sl] - r)
                lhs += [qh * head_lane[0], qh * head_lane[1]]
                khs.append(jnp.where(keys, ks[u] * jnp.exp(jnp.minimum(r - bs[u], 0.0)), 0.0))
            scores.append(lax.dot_general(jnp.concatenate(lhs, axis=0).astype(BF16),
                                          jnp.concatenate(khs, axis=0).astype(BF16), nt,
                                          preferred_element_type=F32))
        o2s = [jnp.dot(jnp.where(own_block, scores[u], 0.0).astype(BF16), v_rep[ci], preferred_element_type=F32)
               for u, (ci, d) in enumerate(units)]
        for u, (ci, d) in enumerate(units):
            outs = []
            for blk in range(n_sub):
                sl = slice(blk * SB, (blk + 1) * SB)
                a_blk = sums[u][blk * SB * SB:(blk + 1) * SB * SB].reshape(SB, SB, LANES)
                o_blk = jnp.sum(a_blk * v[ci][sl][:, None, :], axis=0)
                if blk in off_blocks[d]:
                    j = off_blocks[d].index(blk) * 2 * SB
                    o_blk = o_blk + o2s[u][j:j + SB] * head_lane[0] + o2s[u][j + SB:j + 2 * SB] * head_lane[1]
                outs.append(o_blk)
            qt_ref[d, rows[ci], :] = (q[ci] * jnp.exp(bs[u])).astype(BF16)
            u_ref[d, cs[ci]] = jnp.where(same_head, incs[u], 0.0)
            d_ref[d, cs[ci]] = jnp.exp(totals[u])
            acc_ref[d, rows[ci], :] = jnp.concatenate(outs, axis=0)
        return carry

    lax.fori_loop(0, n_chunks // HG_UNROLL, prep_body, 0)

    def block_diag(s_a, s_b):
        z = jnp.zeros((HG_D, HG_D), F32)
        return jnp.concatenate([jnp.concatenate([s_a, z], axis=1),
                                jnp.concatenate([z, s_b], axis=1)], axis=0)

    def scan_body(i, carry):
        s_f, s_b = carry
        cb = n_chunks - 1 - i
        inc_f, inc_b = u_ref[0, i], u_ref[1, cb]
        u_ref[0, i] = s_f
        u_ref[1, cb] = s_b
        return d_ref[0, i] * s_f + inc_f, d_ref[1, cb] * s_b + inc_b

    s_f, s_b = lax.fori_loop(0, n_chunks, scan_body,
                             (block_diag(s0_ref[0, 0, 0], s0_ref[0, 0, 1]),
                              block_diag(s0_ref[0, 1, 0], s0_ref[0, 1, 1])))
    sfin_ref[0, 0, 0] = s_f[0:HG_D, 0:HG_D]
    sfin_ref[0, 0, 1] = s_f[HG_D:, HG_D:]
    sfin_ref[0, 1, 0] = s_b[0:HG_D, 0:HG_D]
    sfin_ref[0, 1, 1] = s_b[HG_D:, HG_D:]

    bdims = (((2,), (1,)), ((0,), (0,)))
    o = acc_ref[0] + acc_ref[1]
    for d in range(2):
        inter = lax.dot_general(qt_ref[d].reshape(n_chunks, C, LANES), u_ref[d].astype(BF16), bdims,
                                preferred_element_type=F32)
        o = o + inter.reshape(seq_len, LANES)
    ms = _dot_sel(o * o, sel_bf) * (1.0 / HG_D)
    gt = gt_ref[...]
    out_ref[...] = (o * lax.rsqrt(ms + EPS) * gn_ref[...] * (gt * jax.nn.sigmoid(gt))).astype(BF16)


def _hgrn(hg, lb_f, lb_b, gn2, s0, batch, seq_len):
    n = hg.shape[0]
    pairs = HG_W // LANES
    part = lambda p: pl.BlockSpec((seq_len, LANES), lambda b, j, p=p: (b, p * pairs + j))
    lane_row = pl.BlockSpec((1, LANES), lambda b, j: (0, j))
    st = pl.BlockSpec((1, 2, 2, HG_D, HG_D), lambda b, j: (b, 0, j, 0, 0))
    return pl.pallas_call(
        functools.partial(_hgrn_kernel, seq_len=seq_len),
        grid=(batch, pairs),
        in_specs=[part(0), part(1), part(2), part(3), part(4), lane_row, lane_row,
                  pl.BlockSpec((1, LANES), lambda b, j: (0, 0)), st],
        out_specs=[pl.BlockSpec((seq_len, LANES), lambda b, j: (b, j)), st],
        out_shape=[jax.ShapeDtypeStruct((n, HG_W), BF16),
                   jax.ShapeDtypeStruct((batch, 2, HG_HEADS, HG_D, HG_D), F32)],
        scratch_shapes=[pltpu.VMEM((2, seq_len, LANES), BF16),
                        pltpu.VMEM((2, seq_len, LANES), F32),
                        pltpu.VMEM((2, seq_len // HG_CHUNK, LANES, LANES), F32),
                        pltpu.VMEM((2, seq_len // HG_CHUNK, LANES, LANES), F32)],
        compiler_params=_cparams(("parallel", "parallel")),
        name="hgrn2",
    )(hg, hg, hg, hg, hg, lb_f, lb_b, gn2, s0)


def _rope_lanes(x, cos, sin):
    return x * cos + pltpu.roll(x, 96, 1) * sin


def _qproj_kernel(cq_ref, w_ref, cos_ref, sin_ref, q_ref):
    cq = cq_ref[...]
    for h in range(MLA_HEADS):
        qh = jnp.dot(cq, w_ref[h], preferred_element_type=F32)
        q_ref[h] = (_rope_lanes(qh, cos_ref[...], sin_ref[...]) * MLA_SCALE).astype(BF16)


def _kvproj_kernel(ckv_ref, kpe_ref, w_ref, cos_ref, sin_ref, k_ref, kv_ref):
    ckv = ckv_ref[...].astype(BF16)
    kpe = _rope_lanes(kpe_ref[...], cos_ref[...], sin_ref[...])
    nope = lax.broadcasted_iota(jnp.int32, kpe.shape, 1) < MLA_NOPE
    for h in range(MLA_HEADS):
        kvh = jnp.dot(ckv, w_ref[h], preferred_element_type=F32)
        kv_ref[h] = kvh.astype(BF16)
        k_ref[h] = jnp.where(nope, kvh, kpe).astype(BF16)


def _qproj(cqn, wq, cos, sin, tm):
    n = cqn.shape[0]
    pos_blocks = cos.shape[0] // tm
    return pl.pallas_call(
        _qproj_kernel,
        grid=(n // tm,),
        in_specs=[pl.BlockSpec((tm, MLA_Q_LORA), lambda i: (i, 0)),
                  pl.BlockSpec((MLA_HEADS, MLA_Q_LORA, LANES), lambda i: (0, 0, 0)),
                  pl.BlockSpec((tm, LANES), lambda i: (i % pos_blocks, 0)),
                  pl.BlockSpec((tm, LANES), lambda i: (i % pos_blocks, 0))],
        out_specs=pl.BlockSpec((MLA_HEADS, tm, LANES), lambda i: (0, i, 0)),
        out_shape=jax.ShapeDtypeStruct((MLA_HEADS, n, LANES), BF16),
        compiler_params=_cparams(("parallel",)),
        name="mla_q_proj",
    )(cqn, wq, cos, sin)


def _kvproj(ckvn, kpe128, wkv, cos, sin, tm):
    n = ckvn.shape[0]
    pos_blocks = cos.shape[0] // tm
    hm = pl.BlockSpec((MLA_HEADS, tm, LANES), lambda i: (0, i, 0))
    return pl.pallas_call(
        _kvproj_kernel,
        grid=(n // tm,),
        in_specs=[pl.BlockSpec((tm, MLA_KV_LORA), lambda i: (i, 0)),
                  pl.BlockSpec((tm, LANES), lambda i: (i, 0)),
                  pl.BlockSpec((MLA_HEADS, MLA_KV_LORA, LANES), lambda i: (0, 0, 0)),
                  pl.BlockSpec((tm, LANES), lambda i: (i % pos_blocks, 0)),
                  pl.BlockSpec((tm, LANES), lambda i: (i % pos_blocks, 0))],
        out_specs=[hm, hm],
        out_shape=[jax.ShapeDtypeStruct((MLA_HEADS, n, LANES), BF16)] * 2,
        compiler_params=_cparams(("parallel",)),
        name="mla_kv_proj",
    )(ckvn, kpe128, wkv, cos, sin)


def _attn_kernel(q_ref, k_ref, kv_ref, o_ref):
    low = lax.broadcasted_iota(jnp.int32, (q_ref.shape[1], LANES), 1) < MLA_V
    nt = (((1,), (1,)), ((), ()))
    outs = []
    for h in range(MLA_HEADS):
        s = lax.dot_general(q_ref[h], k_ref[h], nt, preferred_element_type=F32)
        p = jnp.exp(s - jnp.max(s, axis=-1, keepdims=True))
        denom = jnp.sum(p, axis=-1, keepdims=True)
        o = jnp.dot(p.astype(BF16), kv_ref[h], preferred_element_type=F32) / denom
        outs.append(o)
    for j in range(MLA_HEADS // 2):
        pair = jnp.where(low, pltpu.roll(outs[2 * j], MLA_V, 1), outs[2 * j + 1])
        o_ref[:, j * LANES:(j + 1) * LANES] = pair.astype(BF16)


def _attention(q128, k128, kv128, batch, lq, lk, tq):
    n = q128.shape[1]
    qb = lq // tq
    kspec = pl.BlockSpec((MLA_HEADS, lk, LANES), lambda b, i: (0, b, 0))
    return pl.pallas_call(
        _attn_kernel,
        grid=(batch, qb),
        in_specs=[pl.BlockSpec((MLA_HEADS, tq, LANES), lambda b, i: (0, b * qb + i, 0)), kspec, kspec],
        out_specs=pl.BlockSpec((tq, MLA_HEADS * MLA_V), lambda b, i: (b * qb + i, 0)),
        out_shape=jax.ShapeDtypeStruct((n, MLA_HEADS * MLA_V), BF16),
        compiler_params=_cparams(("parallel", "parallel")),
        name="mla_attention",
    )(q128, k128, kv128)


def _outproj_kernel(x_ref, hg_ref, at_ref, w_ref, mod_ref, o_ref):
    gate = mod_ref[0, :, 2 * D_MODEL:3 * D_MODEL]
    mix = (jnp.dot(hg_ref[...], w_ref[0:HG_W, :], preferred_element_type=F32)
           + jnp.dot(at_ref[...], w_ref[HG_W:, :], preferred_element_type=F32))
    o_ref[...] = x_ref[...] + gate * mix


def _outproj(x, hg_out, attn, w_out, mod, seq_len, tm):
    n = x.shape[0]
    per_seq = seq_len // tm
    n_cond = mod.shape[0]
    cond_of = (lambda i: (0, 0, 0)) if n_cond == 1 else (lambda i: (i // per_seq, 0, 0))
    row = lambda i: (i, 0)
    return pl.pallas_call(
        _outproj_kernel,
        grid=(n // tm,),
        in_specs=[pl.BlockSpec((tm, D_MODEL), row),
                  pl.BlockSpec((tm, HG_W), row),
                  pl.BlockSpec((tm, MLA_HEADS * MLA_V), row),
                  pl.BlockSpec((2 * HG_W, D_MODEL), lambda i: (0, 0)),
                  pl.BlockSpec((1, 1, 6 * D_MODEL), cond_of)],
        out_specs=pl.BlockSpec((tm, D_MODEL), row),
        out_shape=jax.ShapeDtypeStruct((n, D_MODEL), F32),
        compiler_params=_cparams(("parallel",)),
        name="out_proj",
    )(x, hg_out, attn, w_out, mod)


_CAND = [(a, b) for a in range(PEER_TOPK) for b in range(PEER_TOPK) if (a + 1) * (b + 1) <= PEER_TOPK]
_CAND_PAD = -(-len(_CAND) // 8) * 8


def _topk_rows(work, k):
    rows = lax.broadcasted_iota(jnp.int32, work.shape, 0).astype(F32)
    n_rows = float(work.shape[0])
    vals, idxs = [], []
    for _ in range(k):
        m = jnp.max(work, axis=0, keepdims=True)
        sel = jnp.min(jnp.where(work == m, rows, n_rows), axis=0, keepdims=True)
        vals.append(m)
        idxs.append(sel)
        work = jnp.where(rows == sel, -jnp.inf, work)
    return jnp.concatenate(vals, axis=0), jnp.concatenate(idxs, axis=0)


def _peerq_kernel(x_ref, g_ref, mod_ref, w_ref, k1_ref, k2_ref, h2_ref, idx_ref, gate_ref, *, idx_offset):
    idx_rows, gate_rows = [], []
    x = x_ref[...]
    shift = mod_ref[0, :, 3 * D_MODEL:4 * D_MODEL]
    scale = mod_ref[0, :, 4 * D_MODEL:5 * D_MODEL]
    h2 = _rms_rows(x, g_ref[...]) * (1.0 + scale) + shift
    h2b = h2.astype(BF16)
    bits = lax.bitcast_convert_type(h2b.astype(F32), jnp.uint32)
    h2_ref[...] = (bits[:, :PACK_W] >> 16) | bits[:, PACK_W:]
    qp = jnp.dot(h2b, w_ref[...], preferred_element_type=F32)
    nt = (((1,), (1,)), ((), ()))
    half = PEER_DQ // 2
    for h in range(PEER_HEADS):
        q1 = qp[:, h * PEER_DQ:h * PEER_DQ + half]
        q2 = qp[:, h * PEER_DQ + half:(h + 1) * PEER_DQ]
        s1 = _dot3(k1_ref[h], q1, nt)
        s2 = _dot3(k2_ref[h], q2, nt)
        v1, i1 = _topk_rows(s1, PEER_TOPK)
        v2, i2 = _topk_rows(s2, PEER_TOPK)
        cs = [v1[a:a + 1, :] + v2[b:b + 1, :] for a, b in _CAND]
        ci = [i1[a:a + 1, :] * PEER_NKEYS + i2[b:b + 1, :] for a, b in _CAND]
        n_pad = _CAND_PAD - len(_CAND)
        if n_pad:
            cs.append(jnp.full((n_pad, x.shape[0]), -jnp.inf, F32))
            ci.append(jnp.zeros((n_pad, x.shape[0]), F32))
        cand_s = jnp.concatenate(cs, axis=0)
        cand_i = jnp.concatenate(ci, axis=0)
        best_s, best_pos = _topk_rows(cand_s, PEER_TOPK)
        rows = lax.broadcasted_iota(jnp.int32, cand_s.shape, 0).astype(F32)
        picked = [jnp.sum(jnp.where(rows == best_pos[r:r + 1, :], cand_i, 0.0), axis=0, keepdims=True)
                  for r in range(PEER_TOPK)]
        idx_rows.extend(picked)
        ex = jnp.exp(best_s - best_s[0:1, :])
        gate_rows.append(ex / jnp.sum(ex, axis=0, keepdims=True))
    idx_ref[...] = jnp.concatenate(idx_rows, axis=0).T.astype(jnp.int32) + idx_offset
    gate_ref[...] = jnp.concatenate(gate_rows, axis=0).T


def _peerq(x, g, mod, wq, k1, k2, seq_len, tm, idx_offset):
    n = x.shape[0]
    per_seq = seq_len // tm
    n_cond = mod.shape[0]
    cond_of = (lambda i: (0, 0, 0)) if n_cond == 1 else (lambda i: (i // per_seq, 0, 0))
    kspec = pl.BlockSpec((PEER_HEADS, PEER_NKEYS, PEER_DQ // 2), lambda i: (0, 0, 0))
    tk = PEER_HEADS * PEER_TOPK
    return pl.pallas_call(
        functools.partial(_peerq_kernel, idx_offset=idx_offset),
        grid=(n // tm,),
        in_specs=[pl.BlockSpec((tm, D_MODEL), lambda i: (i, 0)),
                  pl.BlockSpec((1, D_MODEL), lambda i: (0, 0)),
                  pl.BlockSpec((1, 1, 6 * D_MODEL), cond_of),
                  pl.BlockSpec((D_MODEL, PEER_HEADS * PEER_DQ), lambda i: (0, 0)),
                  kspec, kspec],
        out_specs=[pl.BlockSpec((tm, D_MODEL // 2), lambda i: (i, 0)),
                   pl.BlockSpec((tm, tk), lambda i: (i, 0)),
                   pl.BlockSpec((tm, tk), lambda i: (i, 0))],
        out_shape=[jax.ShapeDtypeStruct((n, D_MODEL // 2), jnp.uint32),
                   jax.ShapeDtypeStruct((n, tk), jnp.int32),
                   jax.ShapeDtypeStruct((n, tk), F32)],
        compiler_params=_cparams(("parallel",)),
        name="peer_retrieve",
    )(x, g, mod, wq, k1, k2)


SC_CORES = 2
SC_SUBCORES = 16
SC_LANES = 16
SC_TILES = SC_CORES * SC_SUBCORES
SC_NBUF = 4
PEER_TK = PEER_HEADS * PEER_TOPK
D_VREGS = D_MODEL // SC_LANES
BF_RUN = 4


PACK_W = D_MODEL // 2


def _pack_kernel(t_ref, o_ref):
    bits = lax.bitcast_convert_type(t_ref[...].astype(BF16).astype(F32), jnp.uint32)
    o_ref[...] = (bits[:, :PACK_W] >> 16) | bits[:, PACK_W:]


def _pack_bf16_pairs(table, tm=1024):
    n = table.shape[0]
    return pl.pallas_call(
        _pack_kernel,
        grid=(n // tm,),
        in_specs=[pl.BlockSpec((tm, D_MODEL), lambda i: (i, 0))],
        out_specs=pl.BlockSpec((tm, PACK_W), lambda i: (i, 0)),
        out_shape=jax.ShapeDtypeStruct((n, PACK_W), jnp.uint32),
        compiler_params=_cparams(("parallel",)),
        name="pack_expert_table",
    )(table)


def _unpack_pair(word):
    lo = lax.bitcast_convert_type(word << 16, F32)
    hi = lax.bitcast_convert_type(word & jnp.uint32(0xFFFF0000), F32)
    return lo, hi


def _sc_mesh():
    return plsc.VectorSubcoreMesh(core_axis_name="c", subcore_axis_name="s")


def _sc_params():
    return pltpu.CompilerParams(needs_layout_passes=False)


def _sc_token_base(tokens_per_tile):
    return (lax.axis_index("s") * SC_CORES + lax.axis_index("c")) * tokens_per_tile


def _peer_u_body(h2_hbm, idx_hbm, u_hbm, act_hbm, idx_v, act_v, x_v, rows_v, sem_x, sem_g, *, tpt):
    base = _sc_token_base(tpt)
    n_items = tpt * PEER_HEADS
    pltpu.sync_copy(idx_hbm.at[pl.ds(base, tpt)], idx_v)

    def gather(i, slot):
        t, h = i // PEER_HEADS, i % PEER_HEADS
        return pltpu.make_async_copy(u_hbm.at[idx_v.at[t, pl.ds(h * PEER_TOPK, PEER_TOPK)]],
                                     rows_v.at[slot], sem_g.at[slot])

    def x_copy(t):
        return pltpu.make_async_copy(h2_hbm.at[base + t], x_v.at[t % 2], sem_x.at[t % 2])

    x_copy(0).start()
    for b in range(SC_NBUF - 1):
        gather(b, b).start()
    lane = lax.iota(jnp.int32, SC_LANES)
    zero = jnp.zeros((SC_LANES,), F32)

    @pl.loop(0, n_items, step=SC_NBUF)
    def _(i0):
        for b in range(SC_NBUF):
            i = i0 + b
            t, h = i // PEER_HEADS, i % PEER_HEADS

            @pl.when(h == 0)
            def _():
                x_copy(t).wait()

                @pl.when(t + 1 < tpt)
                def _():
                    x_copy(t + 1).start()

            @pl.when(i + SC_NBUF - 1 < n_items)
            def _():
                gather(i + SC_NBUF - 1, (b + SC_NBUF - 1) % SC_NBUF).start()

            gather(i, b).wait()
            xs = t % 2

            def cbody(g, accs):
                parts = [None] * PEER_TOPK
                for cc in range(BF_RUN):
                    col = pl.ds((g * BF_RUN + cc) * SC_LANES, SC_LANES)
                    xb = plsc.bitcast(x_v[xs, col], BF16)
                    for j in range(PEER_TOPK):
                        p = xb * plsc.bitcast(rows_v[b, j, col], BF16)
                        parts[j] = p if cc == 0 else parts[j] + p
                out = []
                for j, a in enumerate(accs):
                    lo, hi = _unpack_pair(plsc.bitcast(parts[j], jnp.uint32))
                    out.append(a + lo + hi)
                return tuple(out)

            accs = plsc.parallel_loop(0, PACK_W // SC_LANES // BF_RUN, carry=(zero,) * PEER_TOPK)(cbody)
            vec = zero
            for j in range(PEER_TOPK):
                vec = jnp.where(lane == j, jnp.sum(accs[j]), vec)
            act_v[t, pl.ds(h * PEER_TOPK, PEER_TOPK)] = vec

    pltpu.sync_copy(act_v, act_hbm.at[pl.ds(base, tpt)])


def _peer_v_body(w_hbm, idx_hbm, v_hbm, y_hbm, idx_v, w_v, y_v, rows_v, sem_y, sem_g, *, tpt):
    base = _sc_token_base(tpt)
    n_items = tpt * PEER_HEADS
    pltpu.sync_copy(idx_hbm.at[pl.ds(base, tpt)], idx_v)
    pltpu.sync_copy(w_hbm.at[pl.ds(base, tpt)], w_v)

    def gather(i, slot):
        t, h = i // PEER_HEADS, i % PEER_HEADS
        return pltpu.make_async_copy(v_hbm.at[idx_v.at[t, pl.ds(h * PEER_TOPK, PEER_TOPK)]],
                                     rows_v.at[slot], sem_g.at[slot])

    def y_copy(t):
        return pltpu.make_async_copy(y_v.at[t % 2], y_hbm.at[base + t], sem_y.at[t % 2])

    for b in range(SC_NBUF - 1):
        gather(b, b).start()
    lane = lax.iota(jnp.int32, SC_LANES)
    zero = jnp.zeros((SC_LANES,), F32)

    @pl.loop(0, n_items, step=SC_NBUF)
    def _(i0):
        for b in range(SC_NBUF):
            i = i0 + b
            t, h = i // PEER_HEADS, i % PEER_HEADS
            ys = t % 2

            @pl.when(i + SC_NBUF - 1 < n_items)
            def _():
                gather(i + SC_NBUF - 1, (b + SC_NBUF - 1) % SC_NBUF).start()

            @pl.when(h == 0)
            def _():
                @pl.when(t >= 2)
                def _():
                    y_copy(t - 2).wait()

                @pl.loop(0, D_VREGS)
                def _(c):
                    y_v[ys, pl.ds(c * SC_LANES, SC_LANES)] = zero

            gather(i, b).wait()
            wv = w_v[t, pl.ds(h * PEER_TOPK, PEER_TOPK)]
            ws = [plsc.bitcast(jnp.broadcast_to(jnp.sum(jnp.where(lane == j, wv, 0)), (SC_LANES,)), BF16)
                  for j in range(PEER_TOPK)]

            @plsc.parallel_loop(0, PACK_W // SC_LANES)
            def _(c):
                col = pl.ds(c * SC_LANES, SC_LANES)
                col_hi = pl.ds(PACK_W + c * SC_LANES, SC_LANES)
                terms = [ws[j] * plsc.bitcast(rows_v[b, j, col], BF16) for j in range(PEER_TOPK)]
                while len(terms) > 1:
                    terms = [terms[k] + terms[k + 1] for k in range(0, len(terms), 2)]
                lo, hi = _unpack_pair(plsc.bitcast(terms[0], jnp.uint32))
                y_v[ys, col] = y_v[ys, col] + lo
                y_v[ys, col_hi] = y_v[ys, col_hi] + hi

            @pl.when(h == PEER_HEADS - 1)
            def _():
                y_copy(t).start()

    y_copy(tpt - 2).wait()
    y_copy(tpt - 1).wait()


def _sc_scratch(tpt, side_dtype, token_buf):
    return [pltpu.VMEM((tpt, PEER_TK), jnp.int32),
            pltpu.VMEM((tpt, PEER_TK), side_dtype),
            token_buf,
            pltpu.VMEM((SC_NBUF, PEER_TOPK, PACK_W), jnp.uint32),
            pltpu.SemaphoreType.DMA((2,)),
            pltpu.SemaphoreType.DMA((SC_NBUF,))]


def _peer_u(h2, idx, u_all):
    n = h2.shape[0]
    tpt = n // SC_TILES
    return pl.kernel(
        functools.partial(_peer_u_body, tpt=tpt),
        out_type=jax.ShapeDtypeStruct((n, PEER_TK), F32),
        mesh=_sc_mesh(),
        scratch_types=_sc_scratch(tpt, F32, pltpu.VMEM((2, PACK_W), jnp.uint32)),
        compiler_params=_sc_params(),
        name="peer_u_sc",
    )(h2, idx, u_all)


def _peer_v(w, idx, v_all):
    n = w.shape[0]
    tpt = n // SC_TILES
    return pl.kernel(
        functools.partial(_peer_v_body, tpt=tpt),
        out_type=jax.ShapeDtypeStruct((n, D_MODEL), F32),
        mesh=_sc_mesh(),
        scratch_types=_sc_scratch(tpt, jnp.int32, pltpu.VMEM((2, D_MODEL), F32)),
        compiler_params=_sc_params(),
        name="peer_v_sc",
    )(w, idx, v_all)


def _peer_gelu_kernel(gate_ref, act_ref, w_ref):
    act = act_ref[...]
    w = gate_ref[...] * (0.5 * act * (1.0 + lax.erf(act * (2.0 ** -0.5))))
    bits = lax.bitcast_convert_type(w.astype(BF16).astype(F32), jnp.uint32)
    w_ref[...] = lax.bitcast_convert_type(bits | (bits >> 16), jnp.int32)


def _peer_gelu(gate, act, tm):
    n = gate.shape[0]
    spec = pl.BlockSpec((tm, PEER_TK), lambda i: (i, 0))
    return pl.pallas_call(
        _peer_gelu_kernel,
        grid=(n // tm,),
        in_specs=[spec, spec],
        out_specs=spec,
        out_shape=jax.ShapeDtypeStruct((n, PEER_TK), jnp.int32),
        compiler_params=_cparams(("parallel",)),
        name="peer_gelu",
    )(gate, act)


def _peer_residual_kernel(x_ref, y_ref, mod_ref, o_ref):
    o_ref[...] = x_ref[...] + mod_ref[0, :, 5 * D_MODEL:6 * D_MODEL] * y_ref[...]


def _peer_residual(x, y, mod, seq_len, tm):
    n = x.shape[0]
    per_seq = seq_len // tm
    n_cond = mod.shape[0]
    cond_of = (lambda i: (0, 0, 0)) if n_cond == 1 else (lambda i: (i // per_seq, 0, 0))
    row = pl.BlockSpec((tm, D_MODEL), lambda i: (i, 0))
    return pl.pallas_call(
        _peer_residual_kernel,
        grid=(n // tm,),
        in_specs=[row, row, pl.BlockSpec((1, 1, 6 * D_MODEL), cond_of)],
        out_specs=row,
        out_shape=jax.ShapeDtypeStruct((n, D_MODEL), F32),
        compiler_params=_cparams(("parallel",)),
        name="peer_residual",
    )(x, y, mod)


def _final_norm_kernel(x_ref, g_ref, o_ref):
    o_ref[...] = _rms_rows(x_ref[...], g_ref[...])


def _final_norm(x, g, tm):
    n = x.shape[0]
    return pl.pallas_call(
        _final_norm_kernel,
        grid=(n // tm,),
        in_specs=[pl.BlockSpec((tm, D_MODEL), lambda i: (i, 0)), pl.BlockSpec((1, D_MODEL), lambda i: (0, 0))],
        out_specs=pl.BlockSpec((tm, D_MODEL), lambda i: (i, 0)),
        out_shape=jax.ShapeDtypeStruct((n, D_MODEL), F32),
        compiler_params=_cparams(("parallel",)),
        name="final_norm",
    )(x, g)


def _rot_half_columns(w):
    half = ROPE_AXIS_DIM // 2
    w4 = w.reshape(w.shape[:-1] + (2, 2, half))
    return jnp.stack([-w4[..., 1, :], w4[..., 0, :]], axis=-2).reshape(w.shape)


def _rope_tables(seq_len, rotate):
    cos = np.zeros((seq_len, LANES), np.float32)
    sin = np.zeros((seq_len, LANES), np.float32)
    cos[:, :MLA_NOPE + MLA_ROPE] = 1.0
    if rotate:
        pos = np.arange(seq_len)
        half = ROPE_AXIS_DIM // 2
        inv = (1.0 / (ROPE_BASE ** (np.arange(0, ROPE_AXIS_DIM, 2, dtype=np.float32) / ROPE_AXIS_DIM))).astype(np.float32)
        for axis, p in enumerate(((pos // GRID_W).astype(np.float32), (pos % GRID_W).astype(np.float32))):
            ang = p[:, None] * inv[None, :]
            for part in range(2):
                lo = MLA_NOPE + axis * ROPE_AXIS_DIM + part * half
                cos[:, lo:lo + half] = np.cos(ang)
                sin[:, lo:lo + half] = np.sin(ang)
    return jnp.asarray(cos), jnp.asarray(sin)


def _prep_layer_weights(l, w_in, mla_wqb, mla_wkvb, w_out, peer_wq):
    wi = w_in[l]
    kpe_w = wi[:, IN_KPE0:IN_KPE0 + MLA_ROPE]
    w_pad = jnp.concatenate(
        [wi[:, :IN_KPE0], jnp.zeros((D_MODEL, MLA_NOPE), F32), kpe_w, _rot_half_columns(kpe_w)], axis=1).astype(BF16)
    wq = mla_wqb[l].reshape(MLA_Q_LORA, MLA_HEADS, MLA_NOPE + MLA_ROPE)
    wq_pe = wq[..., MLA_NOPE:]
    wq128 = jnp.concatenate([wq, _rot_half_columns(wq_pe)], axis=-1).transpose(1, 0, 2).astype(BF16)
    wkv128 = mla_wkvb[l].reshape(MLA_KV_LORA, MLA_HEADS, MLA_NOPE + MLA_V).transpose(1, 0, 2).astype(BF16)
    return w_pad, wq128, wkv128, w_out[l].astype(BF16), peer_wq[l].astype(BF16)


def _trunk_layer(x, mod, l, W, P, lb, batch, seq_len, ctx, rope):
    w_pad, wq128, wkv128, wo, wpq = W
    tm = 256
    hg, cqn, ckvn, kpe128 = _inproj(x, P['norm_mix'][l][None], mod, w_pad, P['mla_q_norm'][l][None],
                                    P['mla_kv_norm'][l][None], seq_len, tm)
    if ctx is None:
        s0 = jnp.zeros((batch, 2, HG_HEADS, HG_D, HG_D), F32)
    else:
        s0 = ctx[2]
    gn2 = jnp.tile(P['hgrn_norm'][l], 2)[None]
    hg_out, s_fin = _hgrn(hg, lb[0, l][None], lb[1, l][None], gn2, s0, batch, seq_len)

    cos, sin = rope
    q128 = _qproj(cqn, wq128, cos, sin, tm)
    k128, kv128 = _kvproj(ckvn, kpe128, wkv128, cos, sin, tm)
    lk = seq_len
    if ctx is not None:
        past = ctx[0].shape[1]
        cos_c, sin_c = _rope_tables(past, False)
        kpe_c = jnp.pad(ctx[1].reshape(batch * past, MLA_ROPE), ((0, 0), (MLA_NOPE, LANES - MLA_NOPE - MLA_ROPE)))
        k_c, kv_c = _kvproj(ctx[0].reshape(batch * past, MLA_KV_LORA), kpe_c, wkv128, cos_c, sin_c, past)
        join = lambda a, b: jnp.concatenate(
            [a.reshape(MLA_HEADS, batch, past, LANES), b.reshape(MLA_HEADS, batch, seq_len, LANES)],
            axis=2).reshape(MLA_HEADS, batch * (past + seq_len), LANES)
        k128, kv128 = join(k_c, k128), join(kv_c, kv128)
        lk = past + seq_len
    attn = _attention(q128, k128, kv128, batch, seq_len, lk, 256)
    x = _outproj(x, hg_out, attn, wo, mod, seq_len, tm)

    keys = P['peer_keys'][l]
    n_experts = P['peer_u'].shape[1]
    h2, idx, gate = _peerq(x, P['norm_ffn'][l][None], mod, wpq, keys[0], keys[1], seq_len, tm, l * n_experts)
    act = _peer_u(h2, idx, P['peer_u_packed'])
    w = _peer_gelu(gate, act, tm)
    y = _peer_v(w, idx, P['peer_v_packed'])
    x = _peer_residual(x, y, mod, seq_len, tm)
    return x, (ckvn, kpe128[:, MLA_NOPE:MLA_NOPE + MLA_ROPE], s_fin)


def kernel(x_prompt, x_sample, cache_ckv, cache_kpe, state_hgrn, c, c_ctx, w_ada, b_ada, norm_mix, w_in,
           hgrn_lb_logits, hgrn_norm, mla_q_norm, mla_wqb, mla_kv_norm, mla_wkvb, w_out, norm_ffn,
           peer_wq, peer_keys, peer_u, peer_v, final_norm):
    P = {'norm_mix': norm_mix, 'hgrn_norm': hgrn_norm, 'mla_q_norm': mla_q_norm, 'mla_kv_norm': mla_kv_norm,
         'norm_ffn': norm_ffn, 'peer_keys': peer_keys, 'peer_u': peer_u,
         'peer_u_packed': _pack_bf16_pairs(peer_u.reshape(-1, D_MODEL)),
         'peer_v_packed': _pack_bf16_pairs(peer_v.reshape(-1, D_MODEL))}
    depth = w_in.shape[0]
    bp, lp, _ = x_prompt.shape
    bs, ls, _ = x_sample.shape

    p = jax.nn.softmax(hgrn_lb_logits.astype(F32), axis=1)
    lb = jnp.cumsum(p, axis=1) - p[:, :1]

    cond8 = jnp.zeros((8, D_MODEL), F32).at[0].set(c_ctx).at[1:1 + bs].set(c)
    mod_all = _ada_mod(cond8, w_ada, b_ada)

    rope_p = _rope_tables(lp, False)
    rope_s = _rope_tables(ls, True)

    xp = x_prompt.reshape(bp * lp, D_MODEL)
    xs = x_sample.reshape(bs * ls, D_MODEL)
    ckvs, kpes, states = [], [], []
    for l in range(depth):
        W = _prep_layer_weights(l, w_in, mla_wqb, mla_wkvb, w_out, peer_wq)
        mod_p = mod_all[l, 0:1][:, None, :]
        mod_s = mod_all[l, 1:1 + bs][:, None, :]
        xp, (ckv_l, kpe_l, st_l) = _trunk_layer(xp, mod_p, l, W, P, lb, bp, lp, None, rope_p)
        ckvs.append(ckv_l.reshape(bp, lp, MLA_KV_LORA))
        kpes.append(kpe_l.reshape(bp, lp, MLA_ROPE))
        states.append(st_l)
        xs, _ = _trunk_layer(xs, mod_s, l, W, P, lb, bs, ls,
                             (cache_ckv[:, l], cache_kpe[:, l], state_hgrn[:, l]), rope_s)

    y_prompt = _final_norm(xp, final_norm[None], 256).reshape(bp, lp, D_MODEL)
    y_sample = _final_norm(xs, final_norm[None], 256).reshape(bs, ls, D_MODEL)
    return (y_prompt, y_sample, jnp.stack(ckvs, axis=1), jnp.stack(kpes, axis=1), jnp.stack(states, axis=1))
```

```python
import functools
import math

import numpy as np
import jax
import jax.numpy as jnp
from jax import lax
from jax.experimental import pallas as pl
from jax.experimental.pallas import tpu as pltpu
from jax.experimental.pallas import tpu_sc as plsc

F32 = jnp.float32
BF16 = jnp.bfloat16
HIGHEST = lax.Precision.HIGHEST

D_MODEL = 1024
EPS = 1e-6
GRID_W = 64
HG_HEADS = 8
HG_D = 64
HG_CHUNK = 32
HG_W = HG_HEADS * HG_D
FORGET_FLOOR = 1e-6
MLA_HEADS = 8
MLA_Q_LORA = 384
MLA_KV_LORA = 256
MLA_NOPE = 64
MLA_ROPE = 32
MLA_V = 64
MLA_SCALE = (MLA_NOPE + MLA_ROPE) ** -0.5
ROPE_AXIS_DIM = MLA_ROPE // 2
ROPE_BASE = 10000.0
PEER_HEADS = 8
PEER_NKEYS = 128
PEER_DQ = 256
PEER_TOPK = 16

LANES = 128
VMEM_LIMIT = 48 * 1024 * 1024

IN_HG = 5 * HG_W
IN_CQ0 = IN_HG
IN_CKV0 = IN_CQ0 + MLA_Q_LORA
IN_KPE0 = IN_CKV0 + MLA_KV_LORA
IN_PAD = IN_KPE0 + LANES


def _cparams(sem):
    return pltpu.CompilerParams(dimension_semantics=sem, vmem_limit_bytes=VMEM_LIMIT)


def _split_bf16(a):
    hi = a.astype(BF16)
    lo = (a - hi.astype(F32)).astype(BF16)
    return hi, lo


def _dot3(a, b, dims=(((1,), (0,)), ((), ()))):
    ah, al = _split_bf16(a)
    bh, bl = _split_bf16(b)
    d = functools.partial(lax.dot_general, dimension_numbers=dims, preferred_element_type=F32)
    return d(ah, bh) + d(al, bh) + d(ah, bl)


def _dot_sel(a, sel_bf):
    ah, al = _split_bf16(a)
    return (jnp.dot(ah, sel_bf, preferred_element_type=F32)
            + jnp.dot(al, sel_bf, preferred_element_type=F32))


def _rms_rows(x, g):
    return x * lax.rsqrt(jnp.mean(x * x, axis=-1, keepdims=True) + EPS) * g


def _ada_kernel(cond_ref, w_ref, b_ref, o_ref):
    cnd = cond_ref[...]
    s = cnd * jax.nn.sigmoid(cnd)
    o_ref[0] = jnp.dot(s, w_ref[0], precision=HIGHEST, preferred_element_type=F32) + b_ref[0]


def _ada_mod(cond8, w_ada, b_ada):
    depth, d, n6 = w_ada.shape
    tn = 1536
    return pl.pallas_call(
        _ada_kernel,
        grid=(depth, n6 // tn),
        in_specs=[pl.BlockSpec((8, d), lambda l, j: (0, 0)),
                  pl.BlockSpec((1, d, tn), lambda l, j: (l, 0, j)),
                  pl.BlockSpec((1, 1, tn), lambda l, j: (l, 0, j))],
        out_specs=pl.BlockSpec((1, 8, tn), lambda l, j: (l, 0, j)),
        out_shape=jax.ShapeDtypeStruct((depth, 8, n6), F32),
        compiler_params=_cparams(("parallel", "parallel")),
        name="ada_mod",
    )(cond8, w_ada, b_ada.reshape(depth, 1, n6))


def _inproj_kernel(x_ref, g_ref, mod_ref, w_ref, gq_ref, gkv_ref, hg_ref, cq_ref, ckv_ref, kpe_ref):
    x = x_ref[...]
    shift = mod_ref[0, :, 0:D_MODEL]
    scale = mod_ref[0, :, D_MODEL:2 * D_MODEL]
    h = (_rms_rows(x, g_ref[...]) * (1.0 + scale) + shift).astype(BF16)
    hg_ref[...] = jnp.dot(h, w_ref[:, 0:IN_HG], preferred_element_type=F32)
    cq = jnp.dot(h, w_ref[:, IN_CQ0:IN_CKV0], preferred_element_type=F32)
    cq_ref[...] = _rms_rows(cq, gq_ref[...]).astype(BF16)
    ckv = jnp.dot(h, w_ref[:, IN_CKV0:IN_KPE0], preferred_element_type=F32)
    ckv_ref[...] = _rms_rows(ckv, gkv_ref[...])
    kpe_ref[...] = jnp.dot(h, w_ref[:, IN_KPE0:IN_PAD], preferred_element_type=F32)


def _inproj(x, g, mod, w_pad, gq, gkv, seq_len, tm):
    n = x.shape[0]
    per_seq = seq_len // tm
    n_cond = mod.shape[0]
    cond_of = (lambda i: (0, 0, 0)) if n_cond == 1 else (lambda i: (i // per_seq, 0, 0))
    row = lambda i: (i, 0)
    const = lambda i: (0, 0)
    return pl.pallas_call(
        _inproj_kernel,
        grid=(n // tm,),
        in_specs=[pl.BlockSpec((tm, D_MODEL), row),
                  pl.BlockSpec((1, D_MODEL), const),
                  pl.BlockSpec((1, 1, 6 * D_MODEL), cond_of),
                  pl.BlockSpec((D_MODEL, IN_PAD), const),
                  pl.BlockSpec((1, MLA_Q_LORA), const),
                  pl.BlockSpec((1, MLA_KV_LORA), const)],
        out_specs=[pl.BlockSpec((tm, IN_HG), row),
                   pl.BlockSpec((tm, MLA_Q_LORA), row),
                   pl.BlockSpec((tm, MLA_KV_LORA), row),
                   pl.BlockSpec((tm, LANES), row)],
        out_shape=[jax.ShapeDtypeStruct((n, IN_HG), F32),
                   jax.ShapeDtypeStruct((n, MLA_Q_LORA), BF16),
                   jax.ShapeDtypeStruct((n, MLA_KV_LORA), F32),
                   jax.ShapeDtypeStruct((n, LANES), F32)],
        compiler_params=_cparams(("parallel",)),
        name="in_proj",
    )(x, g, mod, w_pad, gq, gkv)


HG_SUB = 8
HG_UNROLL = 2


def _hgrn_kernel(q_ref, ff_ref, fb_ref, v_ref, gt_ref, lbf_ref, lbb_ref, gn_ref, s0_ref,
                 out_ref, sfin_ref, qt_ref, acc_ref, u_ref, d_ref, *, seq_len):
    C, SB = HG_CHUNK, HG_SUB
    n_sub = C // SB
    n_chunks = seq_len // C
    r2 = lax.broadcasted_iota(jnp.int32, (LANES, LANES), 0) // HG_D
    c2 = lax.broadcasted_iota(jnp.int32, (LANES, LANES), 1) // HG_D
    same_head = r2 == c2
    sel_bf = same_head.astype(BF16)
    ti = lax.broadcasted_iota(jnp.int32, (C, C), 0)
    ui = lax.broadcasted_iota(jnp.int32, (C, C), 1)
    tri = ((ui <= ti).astype(F32), (ui >= ti).astype(F32))
    s3 = lax.broadcasted_iota(jnp.int32, (SB, SB, 1), 0)
    t3 = lax.broadcasted_iota(jnp.int32, (SB, SB, 1), 1)
    diag_mask = (t3 >= s3, t3 <= s3)
    key_row = lax.broadcasted_iota(jnp.int32, (C, 1), 0)
    lane1 = lax.broadcasted_iota(jnp.int32, (1, LANES), 1)
    head_lane = ((lane1 < HG_D).astype(F32), (lane1 >= HG_D).astype(F32))
    ones_c = jnp.ones((C, LANES), F32)
    tdims = (((0,), (0,)), ((), ()))
    nt = (((1,), (1,)), ((), ()))
    z_refs, lb_refs = (ff_ref, fb_ref), (lbf_ref, lbb_ref)

    off_blocks = (tuple(range(1, n_sub)), tuple(range(0, n_sub - 1)))
    n_off = n_sub - 1
    sr = lax.broadcasted_iota(jnp.int32, (n_off * 2 * SB, n_off * C), 0) // (2 * SB)
    sc = lax.broadcasted_iota(jnp.int32, (n_off * 2 * SB, n_off * C), 1) // C
    own_block = sr == sc

    units = [(ci, d) for ci in range(HG_UNROLL) for d in range(2)]

    def prep_body(it, carry):
        cs = [it * HG_UNROLL + ci for ci in range(HG_UNROLL)]
        rows = [pl.ds(pl.multiple_of(c * C, C), C) for c in cs]
        q = [q_ref[r, :] * (HG_D ** -0.5) for r in rows]
        v = [v_ref[r, :] for r in rows]
        v_bf = [x.astype(BF16) for x in v]
        v_rep = [jnp.concatenate([x] * n_off, axis=0) for x in v_bf]
        gs, ks = [], []
        for ci, d in units:
            lb = lb_refs[d][...]
            f = jnp.maximum(lb + (1.0 - lb) * jax.nn.sigmoid(z_refs[d][rows[ci], :]), FORGET_FLOOR)
            gs.append(jnp.log(f))
            ks.append(1.0 - f)
        bs = [jnp.dot(tri[d], gs[u], precision=HIGHEST, preferred_element_type=F32) for u, (ci, d) in enumerate(units)]
        totals = [lax.dot_general(g, ones_c, tdims, precision=HIGHEST, preferred_element_type=F32) for g in gs]
        incs, sums, scores = [], [], []
        for u, (ci, d) in enumerate(units):
            b_end = bs[u][C - 1:C, :] if d == 0 else bs[u][0:1, :]
            kk = (ks[u] * jnp.exp(b_end - bs[u])).astype(BF16)
            incs.append(lax.dot_general(kk, v_bf[ci], tdims, preferred_element_type=F32))
        for u, (ci, d) in enumerate(units):
            blocks = []
            for blk in range(n_sub):
                sl = slice(blk * SB, (blk + 1) * SB)
                bi = bs[u][sl]
                diff = bi[None, :, :] - bi[:, None, :]
                e = jnp.where(diag_mask[d], jnp.exp(jnp.where(diag_mask[d], diff, 0.0)), 0.0)
                blocks.append((e * q[ci][sl][None, :, :] * ks[u][sl][:, None, :]).reshape(SB * SB, LANES))
            sums.append(jnp.dot(jnp.concatenate(blocks, axis=0).astype(BF16), sel_bf, preferred_element_type=F32))
        for u, (ci, d) in enumerate(units):
            lhs, khs = [], []
            for blk in off_blocks[d]:
                sl = slice(blk * SB, (blk + 1) * SB)
                if d == 0:
                    r = bs[u][blk * SB - 1:blk * SB, :]
                    keys = key_row < blk * SB
                else:
                    r = bs[u][(blk + 1) * SB:(blk + 1) * SB + 1, :]
                    keys = key_row >= (blk + 1) * SB
                qh = q[ci][sl] * jnp.exp(bs[u][sl] - r)
                lhs += [qh * head_lane[0], qh * head_lane[1]]
                khs.append(jnp.where(keys, ks[u] * jnp.exp(jnp.minimum(r - bs[u], 0.0)), 0.0))
            scores.append(lax.dot_general(jnp.concatenate(lhs, axis=0).astype(BF16),
                                          jnp.concatenate(khs, axis=0).astype(BF16), nt,
                                          preferred_element_type=F32))
        o2s = [jnp.dot(jnp.where(own_block, scores[u], 0.0).astype(BF16), v_rep[ci], preferred_element_type=F32)
               for u, (ci, d) in enumerate(units)]
        for u, (ci, d) in enumerate(units):
            outs = []
            for blk in range(n_sub):
                sl = slice(blk * SB, (blk + 1) * SB)
                a_blk = sums[u][blk * SB * SB:(blk + 1) * SB * SB].reshape(SB, SB, LANES)
                o_blk = jnp.sum(a_blk * v[ci][sl][:, None, :], axis=0)
                if blk in off_blocks[d]:
                    j = off_blocks[d].index(blk) * 2 * SB
                    o_blk = o_blk + o2s[u][j:j + SB] * head_lane[0] + o2s[u][j + SB:j + 2 * SB] * head_lane[1]
                outs.append(o_blk)
            qt_ref[d, rows[ci], :] = (q[ci] * jnp.exp(bs[u])).astype(BF16)
            u_ref[d, cs[ci]] = jnp.where(same_head, incs[u], 0.0)
            d_ref[d, cs[ci]] = jnp.exp(totals[u])
            acc_ref[d, rows[ci], :] = jnp.concatenate(outs, axis=0)
        return carry

    lax.fori_loop(0, n_chunks // HG_UNROLL, prep_body, 0)

    def block_diag(s_a, s_b):
        z = jnp.zeros((HG_D, HG_D), F32)
        return jnp.concatenate([jnp.concatenate([s_a, z], axis=1),
                                jnp.concatenate([z, s_b], axis=1)], axis=0)

    def scan_body(i, carry):
        s_f, s_b = carry
        cb = n_chunks - 1 - i
        inc_f, inc_b = u_ref[0, i], u_ref[1, cb]
        u_ref[0, i] = s_f
        u_ref[1, cb] = s_b
        return d_ref[0, i] * s_f + inc_f, d_ref[1, cb] * s_b + inc_b

    s_f, s_b = lax.fori_loop(0, n_chunks, scan_body,
                             (block_diag(s0_ref[0, 0, 0], s0_ref[0, 0, 1]),
                              block_diag(s0_ref[0, 1, 0], s0_ref[0, 1, 1])))
    sfin_ref[0, 0, 0] = s_f[0:HG_D, 0:HG_D]
    sfin_ref[0, 0, 1] = s_f[HG_D:, HG_D:]
    sfin_ref[0, 1, 0] = s_b[0:HG_D, 0:HG_D]
    sfin_ref[0, 1, 1] = s_b[HG_D:, HG_D:]

    bdims = (((2,), (1,)), ((0,), (0,)))
    o = acc_ref[0] + acc_ref[1]
    for d in range(2):
        inter = lax.dot_general(qt_ref[d].reshape(n_chunks, C, LANES), u_ref[d].astype(BF16), bdims,
                                preferred_element_type=F32)
        o = o + inter.reshape(seq_len, LANES)
    ms = _dot_sel(o * o, sel_bf) * (1.0 / HG_D)
    gt = gt_ref[...]
    out_ref[...] = (o * lax.rsqrt(ms + EPS) * gn_ref[...] * (gt * jax.nn.sigmoid(gt))).astype(BF16)


def _hgrn(hg, lb_f, lb_b, gn2, s0, batch, seq_len):
    n = hg.shape[0]
    pairs = HG_W // LANES
    part = lambda p: pl.BlockSpec((seq_len, LANES), lambda b, j, p=p: (b, p * pairs + j))
    lane_row = pl.BlockSpec((1, LANES), lambda b, j: (0, j))
    st = pl.BlockSpec((1, 2, 2, HG_D, HG_D), lambda b, j: (b, 0, j, 0, 0))
    return pl.pallas_call(
        functools.partial(_hgrn_kernel, seq_len=seq_len),
        grid=(batch, pairs),
        in_specs=[part(0), part(1), part(2), part(3), part(4), lane_row, lane_row,
                  pl.BlockSpec((1, LANES), lambda b, j: (0, 0)), st],
        out_specs=[pl.BlockSpec((seq_len, LANES), lambda b, j: (b, j)), st],
        out_shape=[jax.ShapeDtypeStruct((n, HG_W), BF16),
                   jax.ShapeDtypeStruct((batch, 2, HG_HEADS, HG_D, HG_D), F32)],
        scratch_shapes=[pltpu.VMEM((2, seq_len, LANES), BF16),
                        pltpu.VMEM((2, seq_len, LANES), F32),
                        pltpu.VMEM((2, seq_len // HG_CHUNK, LANES, LANES), F32),
                        pltpu.VMEM((2, seq_len // HG_CHUNK, LANES, LANES), F32)],
        compiler_params=_cparams(("parallel", "parallel")),
        name="hgrn2",
    )(hg, hg, hg, hg, hg, lb_f, lb_b, gn2, s0)


def _rope_lanes(x, cos, sin):
    return x * cos + pltpu.roll(x, 96, 1) * sin


def _qproj_kernel(cq_ref, w_ref, cos_ref, sin_ref, q_ref):
    cq = cq_ref[...]
    for h in range(MLA_HEADS):
        qh = jnp.dot(cq, w_ref[h], preferred_element_type=F32)
        q_ref[h] = (_rope_lanes(qh, cos_ref[...], sin_ref[...]) * MLA_SCALE).astype(BF16)


def _kvproj_kernel(ckv_ref, kpe_ref, w_ref, cos_ref, sin_ref, k_ref, kv_ref):
    ckv = ckv_ref[...].astype(BF16)
    kpe = _rope_lanes(kpe_ref[...], cos_ref[...], sin_ref[...])
    nope = lax.broadcasted_iota(jnp.int32, kpe.shape, 1) < MLA_NOPE
    for h in range(MLA_HEADS):
        kvh = jnp.dot(ckv, w_ref[h], preferred_element_type=F32)
        kv_ref[h] = kvh.astype(BF16)
        k_ref[h] = jnp.where(nope, kvh, kpe).astype(BF16)


def _qproj(cqn, wq, cos, sin, tm):
    n = cqn.shape[0]
    pos_blocks = cos.shape[0] // tm
    return pl.pallas_call(
        _qproj_kernel,
        grid=(n // tm,),
        in_specs=[pl.BlockSpec((tm, MLA_Q_LORA), lambda i: (i, 0)),
                  pl.BlockSpec((MLA_HEADS, MLA_Q_LORA, LANES), lambda i: (0, 0, 0)),
                  pl.BlockSpec((tm, LANES), lambda i: (i % pos_blocks, 0)),
                  pl.BlockSpec((tm, LANES), lambda i: (i % pos_blocks, 0))],
        out_specs=pl.BlockSpec((MLA_HEADS, tm, LANES), lambda i: (0, i, 0)),
        out_shape=jax.ShapeDtypeStruct((MLA_HEADS, n, LANES), BF16),
        compiler_params=_cparams(("parallel",)),
        name="mla_q_proj",
    )(cqn, wq, cos, sin)


def _kvproj(ckvn, kpe128, wkv, cos, sin, tm):
    n = ckvn.shape[0]
    pos_blocks = cos.shape[0] // tm
    hm = pl.BlockSpec((MLA_HEADS, tm, LANES), lambda i: (0, i, 0))
    return pl.pallas_call(
        _kvproj_kernel,
        grid=(n // tm,),
        in_specs=[pl.BlockSpec((tm, MLA_KV_LORA), lambda i: (i, 0)),
                  pl.BlockSpec((tm, LANES), lambda i: (i, 0)),
                  pl.BlockSpec((MLA_HEADS, MLA_KV_LORA, LANES), lambda i: (0, 0, 0)),
                  pl.BlockSpec((tm, LANES), lambda i: (i % pos_blocks, 0)),
                  pl.BlockSpec((tm, LANES), lambda i: (i % pos_blocks, 0))],
        out_specs=[hm, hm],
        out_shape=[jax.ShapeDtypeStruct((MLA_HEADS, n, LANES), BF16)] * 2,
        compiler_params=_cparams(("parallel",)),
        name="mla_kv_proj",
    )(ckvn, kpe128, wkv, cos, sin)


def _attn_kernel(q_ref, k_ref, kv_ref, o_ref):
    low = lax.broadcasted_iota(jnp.int32, (q_ref.shape[1], LANES), 1) < MLA_V
    nt = (((1,), (1,)), ((), ()))
    outs = []
    for h in range(MLA_HEADS):
        s = lax.dot_general(q_ref[h], k_ref[h], nt, preferred_element_type=F32)
        p = jnp.exp(s - jnp.max(s, axis=-1, keepdims=True))
        denom = jnp.sum(p, axis=-1, keepdims=True)
        o = jnp.dot(p.astype(BF16), kv_ref[h], preferred_element_type=F32) / denom
        outs.append(o)
    for j in range(MLA_HEADS // 2):
        pair = jnp.where(low, pltpu.roll(outs[2 * j], MLA_V, 1), outs[2 * j + 1])
        o_ref[:, j * LANES:(j + 1) * LANES] = pair.astype(BF16)


def _attention(q128, k128, kv128, batch, lq, lk, tq):
    n = q128.shape[1]
    qb = lq // tq
    kspec = pl.BlockSpec((MLA_HEADS, lk, LANES), lambda b, i: (0, b, 0))
    return pl.pallas_call(
        _attn_kernel,
        grid=(batch, qb),
        in_specs=[pl.BlockSpec((MLA_HEADS, tq, LANES), lambda b, i: (0, b * qb + i, 0)), kspec, kspec],
        out_specs=pl.BlockSpec((tq, MLA_HEADS * MLA_V), lambda b, i: (b * qb + i, 0)),
        out_shape=jax.ShapeDtypeStruct((n, MLA_HEADS * MLA_V), BF16),
        compiler_params=_cparams(("parallel", "parallel")),
        name="mla_attention",
    )(q128, k128, kv128)


def _outproj_kernel(x_ref, hg_ref, at_ref, w_ref, mod_ref, o_ref):
    gate = mod_ref[0, :, 2 * D_MODEL:3 * D_MODEL]
    mix = (jnp.dot(hg_ref[...], w_ref[0:HG_W, :], preferred_element_type=F32)
           + jnp.dot(at_ref[...], w_ref[HG_W:, :], preferred_element_type=F32))
    o_ref[...] = x_ref[...] + gate * mix


def _outproj(x, hg_out, attn, w_out, mod, seq_len, tm):
    n = x.shape[0]
    per_seq = seq_len // tm
    n_cond = mod.shape[0]
    cond_of = (lambda i: (0, 0, 0)) if n_cond == 1 else (lambda i: (i // per_seq, 0, 0))
    row = lambda i: (i, 0)
    return pl.pallas_call(
        _outproj_kernel,
        grid=(n // tm,),
        in_specs=[pl.BlockSpec((tm, D_MODEL), row),
                  pl.BlockSpec((tm, HG_W), row),
                  pl.BlockSpec((tm, MLA_HEADS * MLA_V), row),
                  pl.BlockSpec((2 * HG_W, D_MODEL), lambda i: (0, 0)),
                  pl.BlockSpec((1, 1, 6 * D_MODEL), cond_of)],
        out_specs=pl.BlockSpec((tm, D_MODEL), row),
        out_shape=jax.ShapeDtypeStruct((n, D_MODEL), F32),
        compiler_params=_cparams(("parallel",)),
        name="out_proj",
    )(x, hg_out, attn, w_out, mod)


_CAND = [(a, b) for a in range(PEER_TOPK) for b in range(PEER_TOPK) if (a + 1) * (b + 1) <= PEER_TOPK]
_CAND_PAD = -(-len(_CAND) // 8) * 8


def _topk_rows(work, k):
    rows = lax.broadcasted_iota(jnp.int32, work.shape, 0).astype(F32)
    n_rows = float(work.shape[0])
    vals, idxs = [], []
    for _ in range(k):
        m = jnp.max(work, axis=0, keepdims=True)
        sel = jnp.min(jnp.where(work == m, rows, n_rows), axis=0, keepdims=True)
        vals.append(m)
        idxs.append(sel)
        work = jnp.where(rows == sel, -jnp.inf, work)
    return jnp.concatenate(vals, axis=0), jnp.concatenate(idxs, axis=0)


def _peerq_kernel(x_ref, g_ref, mod_ref, w_ref, k1_ref, k2_ref, after_ref, h2_ref, idx_ref, gate_ref, *,
                  idx_offset):
    del after_ref
    idx_rows, gate_rows = [], []
    x = x_ref[...]
    shift = mod_ref[0, :, 3 * D_MODEL:4 * D_MODEL]
    scale = mod_ref[0, :, 4 * D_MODEL:5 * D_MODEL]
    h2 = _rms_rows(x, g_ref[...]) * (1.0 + scale) + shift
    h2b = h2.astype(BF16)
    bits = lax.bitcast_convert_type(h2b.astype(F32), jnp.uint32)
    h2_ref[...] = (bits[:, :PACK_W] >> 16) | bits[:, PACK_W:]
    qp = jnp.dot(h2b, w_ref[...], preferred_element_type=F32)
    nt = (((1,), (1,)), ((), ()))
    half = PEER_DQ // 2
    for h in range(PEER_HEADS):
        q1 = qp[:, h * PEER_DQ:h * PEER_DQ + half]
        q2 = qp[:, h * PEER_DQ + half:(h + 1) * PEER_DQ]
        s1 = _dot3(k1_ref[h], q1, nt)
        s2 = _dot3(k2_ref[h], q2, nt)
        v1, i1 = _topk_rows(s1, PEER_TOPK)
        v2, i2 = _topk_rows(s2, PEER_TOPK)
        cs = [v1[a:a + 1, :] + v2[b:b + 1, :] for a, b in _CAND]
        ci = [i1[a:a + 1, :] * PEER_NKEYS + i2[b:b + 1, :] for a, b in _CAND]
        n_pad = _CAND_PAD - len(_CAND)
        if n_pad:
            cs.append(jnp.full((n_pad, x.shape[0]), -jnp.inf, F32))
            ci.append(jnp.zeros((n_pad, x.shape[0]), F32))
        cand_s = jnp.concatenate(cs, axis=0)
        cand_i = jnp.concatenate(ci, axis=0)
        best_s, best_pos = _topk_rows(cand_s, PEER_TOPK)
        rows = lax.broadcasted_iota(jnp.int32, cand_s.shape, 0).astype(F32)
        picked = [jnp.sum(jnp.where(rows == best_pos[r:r + 1, :], cand_i, 0.0), axis=0, keepdims=True)
                  for r in range(PEER_TOPK)]
        idx_rows.extend(picked)
        ex = jnp.exp(best_s - best_s[0:1, :])
        gate_rows.append(ex / jnp.sum(ex, axis=0, keepdims=True))
    idx_ref[...] = jnp.concatenate(idx_rows, axis=0).T.astype(jnp.int32) + idx_offset
    gate_ref[...] = jnp.concatenate(gate_rows, axis=0).T


def _peerq(x, g, mod, wq, k1, k2, seq_len, tm, idx_offset, after):
    n = x.shape[0]
    per_seq = seq_len // tm
    n_cond = mod.shape[0]
    cond_of = (lambda i: (0, 0, 0)) if n_cond == 1 else (lambda i: (i // per_seq, 0, 0))
    kspec = pl.BlockSpec((PEER_HEADS, PEER_NKEYS, PEER_DQ // 2), lambda i: (0, 0, 0))
    tk = PEER_HEADS * PEER_TOPK
    return pl.pallas_call(
        functools.partial(_peerq_kernel, idx_offset=idx_offset),
        grid=(n // tm,),
        in_specs=[pl.BlockSpec((tm, D_MODEL), lambda i: (i, 0)),
                  pl.BlockSpec((1, D_MODEL), lambda i: (0, 0)),
                  pl.BlockSpec((1, 1, 6 * D_MODEL), cond_of),
                  pl.BlockSpec((D_MODEL, PEER_HEADS * PEER_DQ), lambda i: (0, 0)),
                  kspec, kspec,
                  pl.BlockSpec(memory_space=pl.ANY)],
        out_specs=[pl.BlockSpec((tm, D_MODEL // 2), lambda i: (i, 0)),
                   pl.BlockSpec((tm, tk), lambda i: (i, 0)),
                   pl.BlockSpec((tm, tk), lambda i: (i, 0))],
        out_shape=[jax.ShapeDtypeStruct((n, D_MODEL // 2), jnp.uint32),
                   jax.ShapeDtypeStruct((n, tk), jnp.int32),
                   jax.ShapeDtypeStruct((n, tk), F32)],
        compiler_params=_cparams(("parallel",)),
        name="peer_retrieve",
    )(x, g, mod, wq, k1, k2, after)


SC_CORES = 2
SC_SUBCORES = 16
SC_LANES = 16
SC_TILES = SC_CORES * SC_SUBCORES
SC_NBUF_U = 4
SC_NBUF_V = 8
PEER_TK = PEER_HEADS * PEER_TOPK
D_VREGS = D_MODEL // SC_LANES
BF_RUN = 4


PACK_W = D_MODEL // 2


def _pack_kernel(t_ref, o_ref):
    bits = lax.bitcast_convert_type(t_ref[...].astype(BF16).astype(F32), jnp.uint32)
    o_ref[...] = (bits[:, :PACK_W] >> 16) | bits[:, PACK_W:]


def _pack_bf16_pairs(table, tm=1024):
    n = table.shape[0]
    return pl.pallas_call(
        _pack_kernel,
        grid=(n // tm,),
        in_specs=[pl.BlockSpec((tm, D_MODEL), lambda i: (i, 0))],
        out_specs=pl.BlockSpec((tm, PACK_W), lambda i: (i, 0)),
        out_shape=jax.ShapeDtypeStruct((n, PACK_W), jnp.uint32),
        compiler_params=_cparams(("parallel",)),
        name="pack_expert_table",
    )(table)


def _unpack_pair(word):
    lo = lax.bitcast_convert_type(word << 16, F32)
    hi = lax.bitcast_convert_type(word & jnp.uint32(0xFFFF0000), F32)
    return lo, hi


def _sc_mesh():
    return plsc.VectorSubcoreMesh(core_axis_name="c", subcore_axis_name="s")


def _sc_params():
    return pltpu.CompilerParams(needs_layout_passes=False)


def _sc_token_base(tokens_per_tile):
    return (lax.axis_index("s") * SC_CORES + lax.axis_index("c")) * tokens_per_tile


def _peer_u_body(h2_hbm, idx_hbm, u_hbm, after_hbm, act_hbm, idx_v, act_v, x_v, rows_v, sem_x, sem_g, *, tpt,
                 nbuf):
    del after_hbm
    base = _sc_token_base(tpt)
    n_items = tpt * PEER_HEADS
    pltpu.sync_copy(idx_hbm.at[pl.ds(base, tpt)], idx_v)

    def gather(i, slot):
        t, h = i // PEER_HEADS, i % PEER_HEADS
        return pltpu.make_async_copy(u_hbm.at[idx_v.at[t, pl.ds(h * PEER_TOPK, PEER_TOPK)]],
                                     rows_v.at[slot], sem_g.at[slot])

    def x_copy(t):
        return pltpu.make_async_copy(h2_hbm.at[base + t], x_v.at[t % 2], sem_x.at[t % 2])

    x_copy(0).start()
    for b in range(nbuf - 1):
        gather(b, b).start()
    lane = lax.iota(jnp.int32, SC_LANES)
    zero = jnp.zeros((SC_LANES,), F32)

    @pl.loop(0, n_items, step=nbuf)
    def _(i0):
        for b in range(nbuf):
            i = i0 + b
            t, h = i // PEER_HEADS, i % PEER_HEADS

            @pl.when(h == 0)
            def _():
                x_copy(t).wait()

                @pl.when(t + 1 < tpt)
                def _():
                    x_copy(t + 1).start()

            @pl.when(i + nbuf - 1 < n_items)
            def _():
                gather(i + nbuf - 1, (b + nbuf - 1) % nbuf).start()

            gather(i, b).wait()
            xs = t % 2

            def cbody(g, accs):
                parts = [None] * PEER_TOPK
                for cc in range(BF_RUN):
                    col = pl.ds((g * BF_RUN + cc) * SC_LANES, SC_LANES)
                    xb = plsc.bitcast(x_v[xs, col], BF16)
                    for j in range(PEER_TOPK):
                        p = xb * plsc.bitcast(rows_v[b, j, col], BF16)
                        parts[j] = p if cc == 0 else parts[j] + p
                out = []
                for j, a in enumerate(accs):
                    lo, hi = _unpack_pair(plsc.bitcast(parts[j], jnp.uint32))
                    out.append(a + lo + hi)
                return tuple(out)

            accs = plsc.parallel_loop(0, PACK_W // SC_LANES // BF_RUN, carry=(zero,) * PEER_TOPK)(cbody)
            vec = zero
            for j in range(PEER_TOPK):
                vec = jnp.where(lane == j, jnp.sum(accs[j]), vec)
            act_v[t, pl.ds(h * PEER_TOPK, PEER_TOPK)] = vec

    pltpu.sync_copy(act_v, act_hbm.at[pl.ds(base, tpt)])


def _peer_v_body(w_hbm, idx_hbm, v_hbm, after_hbm, y_hbm, idx_v, w_v, y_v, rows_v, sem_y, sem_g, *, tpt,
                 nbuf):
    del after_hbm
    base = _sc_token_base(tpt)
    n_items = tpt * PEER_HEADS
    pltpu.sync_copy(idx_hbm.at[pl.ds(base, tpt)], idx_v)
    pltpu.sync_copy(w_hbm.at[pl.ds(base, tpt)], w_v)

    def gather(i, slot):
        t, h = i // PEER_HEADS, i % PEER_HEADS
        return pltpu.make_async_copy(v_hbm.at[idx_v.at[t, pl.ds(h * PEER_TOPK, PEER_TOPK)]],
                                     rows_v.at[slot], sem_g.at[slot])

    def y_copy(t):
        return pltpu.make_async_copy(y_v.at[t % 2], y_hbm.at[base + t], sem_y.at[t % 2])

    for b in range(nbuf - 1):
        gather(b, b).start()
    lane = lax.iota(jnp.int32, SC_LANES)
    zero = jnp.zeros((SC_LANES,), F32)

    @pl.loop(0, n_items, step=nbuf)
    def _(i0):
        for b in range(nbuf):
            i = i0 + b
            t, h = i // PEER_HEADS, i % PEER_HEADS
            ys = t % 2

            @pl.when(i + nbuf - 1 < n_items)
            def _():
                gather(i + nbuf - 1, (b + nbuf - 1) % nbuf).start()

            @pl.when(h == 0)
            def _():
                @pl.when(t >= 2)
                def _():
                    y_copy(t - 2).wait()

                @pl.loop(0, D_VREGS)
                def _(c):
                    y_v[ys, pl.ds(c * SC_LANES, SC_LANES)] = zero

            gather(i, b).wait()
            wv = w_v[t, pl.ds(h * PEER_TOPK, PEER_TOPK)]
            ws = [plsc.bitcast(jnp.broadcast_to(jnp.sum(jnp.where(lane == j, wv, 0)), (SC_LANES,)), BF16)
                  for j in range(PEER_TOPK)]

            @plsc.parallel_loop(0, PACK_W // SC_LANES)
            def _(c):
                col = pl.ds(c * SC_LANES, SC_LANES)
                col_hi = pl.ds(PACK_W + c * SC_LANES, SC_LANES)
                terms = [ws[j] * plsc.bitcast(rows_v[b, j, col], BF16) for j in range(PEER_TOPK)]
                while len(terms) > 1:
                    terms = [terms[k] + terms[k + 1] for k in range(0, len(terms), 2)]
                lo, hi = _unpack_pair(plsc.bitcast(terms[0], jnp.uint32))
                y_v[ys, col] = y_v[ys, col] + lo
                y_v[ys, col_hi] = y_v[ys, col_hi] + hi

            @pl.when(h == PEER_HEADS - 1)
            def _():
                y_copy(t).start()

    y_copy(tpt - 2).wait()
    y_copy(tpt - 1).wait()


def _sc_scratch(tpt, side_dtype, token_buf, nbuf):
    return [pltpu.VMEM((tpt, PEER_TK), jnp.int32),
            pltpu.VMEM((tpt, PEER_TK), side_dtype),
            token_buf,
            pltpu.VMEM((nbuf, PEER_TOPK, PACK_W), jnp.uint32),
            pltpu.SemaphoreType.DMA((2,)),
            pltpu.SemaphoreType.DMA((nbuf,))]


def _peer_u(h2, idx, u_all, after):
    n = h2.shape[0]
    tpt = n // SC_TILES
    return pl.kernel(
        functools.partial(_peer_u_body, tpt=tpt, nbuf=SC_NBUF_U),
        out_type=jax.ShapeDtypeStruct((n, PEER_TK), F32),
        mesh=_sc_mesh(),
        scratch_types=_sc_scratch(tpt, F32, pltpu.VMEM((2, PACK_W), jnp.uint32), SC_NBUF_U),
        compiler_params=_sc_params(),
        name="peer_u_sc",
    )(h2, idx, u_all, after)


def _peer_v(w, idx, v_all, after):
    n = w.shape[0]
    tpt = n // SC_TILES
    return pl.kernel(
        functools.partial(_peer_v_body, tpt=tpt, nbuf=SC_NBUF_V),
        out_type=jax.ShapeDtypeStruct((n, D_MODEL), F32),
        mesh=_sc_mesh(),
        scratch_types=_sc_scratch(tpt, jnp.int32, pltpu.VMEM((2, D_MODEL), F32), SC_NBUF_V),
        compiler_params=_sc_params(),
        name="peer_v_sc",
    )(w, idx, v_all, after)


def _peer_gelu_kernel(gate_ref, act_ref, w_ref):
    act = act_ref[...]
    w = gate_ref[...] * (0.5 * act * (1.0 + lax.erf(act * (2.0 ** -0.5))))
    bits = lax.bitcast_convert_type(w.astype(BF16).astype(F32), jnp.uint32)
    w_ref[...] = lax.bitcast_convert_type(bits | (bits >> 16), jnp.int32)


def _peer_gelu(gate, act, tm):
    n = gate.shape[0]
    spec = pl.BlockSpec((tm, PEER_TK), lambda i: (i, 0))
    return pl.pallas_call(
        _peer_gelu_kernel,
        grid=(n // tm,),
        in_specs=[spec, spec],
        out_specs=spec,
        out_shape=jax.ShapeDtypeStruct((n, PEER_TK), jnp.int32),
        compiler_params=_cparams(("parallel",)),
        name="peer_gelu",
    )(gate, act)


def _peer_residual_kernel(x_ref, y_ref, mod_ref, o_ref):
    o_ref[...] = x_ref[...] + mod_ref[0, :, 5 * D_MODEL:6 * D_MODEL] * y_ref[...]


def _peer_residual(x, y, mod, seq_len, tm):
    n = x.shape[0]
    per_seq = seq_len // tm
    n_cond = mod.shape[0]
    cond_of = (lambda i: (0, 0, 0)) if n_cond == 1 else (lambda i: (i // per_seq, 0, 0))
    row = pl.BlockSpec((tm, D_MODEL), lambda i: (i, 0))
    return pl.pallas_call(
        _peer_residual_kernel,
        grid=(n // tm,),
        in_specs=[row, row, pl.BlockSpec((1, 1, 6 * D_MODEL), cond_of)],
        out_specs=row,
        out_shape=jax.ShapeDtypeStruct((n, D_MODEL), F32),
        compiler_params=_cparams(("parallel",)),
        name="peer_residual",
    )(x, y, mod)


def _final_norm_kernel(x_ref, g_ref, o_ref):
    o_ref[...] = _rms_rows(x_ref[...], g_ref[...])


def _final_norm(x, g, tm):
    n = x.shape[0]
    return pl.pallas_call(
        _final_norm_kernel,
        grid=(n // tm,),
        in_specs=[pl.BlockSpec((tm, D_MODEL), lambda i: (i, 0)), pl.BlockSpec((1, D_MODEL), lambda i: (0, 0))],
        out_specs=pl.BlockSpec((tm, D_MODEL), lambda i: (i, 0)),
        out_shape=jax.ShapeDtypeStruct((n, D_MODEL), F32),
        compiler_params=_cparams(("parallel",)),
        name="final_norm",
    )(x, g)


def _rot_half_columns(w):
    half = ROPE_AXIS_DIM // 2
    w4 = w.reshape(w.shape[:-1] + (2, 2, half))
    return jnp.stack([-w4[..., 1, :], w4[..., 0, :]], axis=-2).reshape(w.shape)


def _rope_tables(seq_len, rotate):
    cos = np.zeros((seq_len, LANES), np.float32)
    sin = np.zeros((seq_len, LANES), np.float32)
    cos[:, :MLA_NOPE + MLA_ROPE] = 1.0
    if rotate:
        pos = np.arange(seq_len)
        half = ROPE_AXIS_DIM // 2
        inv = (1.0 / (ROPE_BASE ** (np.arange(0, ROPE_AXIS_DIM, 2, dtype=np.float32) / ROPE_AXIS_DIM))).astype(np.float32)
        for axis, p in enumerate(((pos // GRID_W).astype(np.float32), (pos % GRID_W).astype(np.float32))):
            ang = p[:, None] * inv[None, :]
            for part in range(2):
                lo = MLA_NOPE + axis * ROPE_AXIS_DIM + part * half
                cos[:, lo:lo + half] = np.cos(ang)
                sin[:, lo:lo + half] = np.sin(ang)
    return jnp.asarray(cos), jnp.asarray(sin)


def _prep_layer_weights(l, w_in, mla_wqb, mla_wkvb, w_out, peer_wq):
    wi = w_in[l]
    kpe_w = wi[:, IN_KPE0:IN_KPE0 + MLA_ROPE]
    w_pad = jnp.concatenate(
        [wi[:, :IN_KPE0], jnp.zeros((D_MODEL, MLA_NOPE), F32), kpe_w, _rot_half_columns(kpe_w)], axis=1).astype(BF16)
    wq = mla_wqb[l].reshape(MLA_Q_LORA, MLA_HEADS, MLA_NOPE + MLA_ROPE)
    wq_pe = wq[..., MLA_NOPE:]
    wq128 = jnp.concatenate([wq, _rot_half_columns(wq_pe)], axis=-1).transpose(1, 0, 2).astype(BF16)
    wkv128 = mla_wkvb[l].reshape(MLA_KV_LORA, MLA_HEADS, MLA_NOPE + MLA_V).transpose(1, 0, 2).astype(BF16)
    return w_pad, wq128, wkv128, w_out[l].astype(BF16), peer_wq[l].astype(BF16)


TOK_TILE = 256


def _mixer_stage(x, mod, l, W, P, lb, batch, seq_len, ctx, rope):
    w_pad, wq128, wkv128, wo, _ = W
    tm = TOK_TILE
    hg, cqn, ckvn, kpe128 = _inproj(x, P['norm_mix'][l][None], mod, w_pad, P['mla_q_norm'][l][None],
                                    P['mla_kv_norm'][l][None], seq_len, tm)
    if ctx is None:
        s0 = jnp.zeros((batch, 2, HG_HEADS, HG_D, HG_D), F32)
    else:
        s0 = ctx[2]
    gn2 = jnp.tile(P['hgrn_norm'][l], 2)[None]
    hg_out, s_fin = _hgrn(hg, lb[0, l][None], lb[1, l][None], gn2, s0, batch, seq_len)

    cos, sin = rope
    q128 = _qproj(cqn, wq128, cos, sin, tm)
    k128, kv128 = _kvproj(ckvn, kpe128, wkv128, cos, sin, tm)
    lk = seq_len
    if ctx is not None:
        past = ctx[0].shape[1]
        cos_c, sin_c = _rope_tables(past, False)
        kpe_c = jnp.pad(ctx[1].reshape(batch * past, MLA_ROPE), ((0, 0), (MLA_NOPE, LANES - MLA_NOPE - MLA_ROPE)))
        k_c, kv_c = _kvproj(ctx[0].reshape(batch * past, MLA_KV_LORA), kpe_c, wkv128, cos_c, sin_c, past)
        join = lambda a, b: jnp.concatenate(
            [a.reshape(MLA_HEADS, batch, past, LANES), b.reshape(MLA_HEADS, batch, seq_len, LANES)],
            axis=2).reshape(MLA_HEADS, batch * (past + seq_len), LANES)
        k128, kv128 = join(k_c, k128), join(kv_c, kv128)
        lk = past + seq_len
    attn = _attention(q128, k128, kv128, batch, seq_len, lk, 256)
    x = _outproj(x, hg_out, attn, wo, mod, seq_len, tm)
    return x, (ckvn, kpe128[:, MLA_NOPE:MLA_NOPE + MLA_ROPE], s_fin)


def _retrieve_stage(x, mod, l, W, P, seq_len, after):
    keys = P['peer_keys'][l]
    n_experts = P['peer_u'].shape[1]
    return _peerq(x, P['norm_ffn'][l][None], mod, W[4], keys[0], keys[1], seq_len, TOK_TILE, l * n_experts, after)


def kernel(x_prompt, x_sample, cache_ckv, cache_kpe, state_hgrn, c, c_ctx, w_ada, b_ada, norm_mix, w_in,
           hgrn_lb_logits, hgrn_norm, mla_q_norm, mla_wqb, mla_kv_norm, mla_wkvb, w_out, norm_ffn,
           peer_wq, peer_keys, peer_u, peer_v, final_norm):
    P = {'norm_mix': norm_mix, 'hgrn_norm': hgrn_norm, 'mla_q_norm': mla_q_norm, 'mla_kv_norm': mla_kv_norm,
         'norm_ffn': norm_ffn, 'peer_keys': peer_keys, 'peer_u': peer_u,
         'peer_u_packed': _pack_bf16_pairs(peer_u.reshape(-1, D_MODEL)),
         'peer_v_packed': _pack_bf16_pairs(peer_v.reshape(-1, D_MODEL))}
    depth = w_in.shape[0]
    bp, lp, _ = x_prompt.shape
    bs, ls, _ = x_sample.shape

    p = jax.nn.softmax(hgrn_lb_logits.astype(F32), axis=1)
    lb = jnp.cumsum(p, axis=1) - p[:, :1]

    cond8 = jnp.zeros((8, D_MODEL), F32).at[0].set(c_ctx).at[1:1 + bs].set(c)
    mod_all = _ada_mod(cond8, w_ada, b_ada)

    rope_p = _rope_tables(lp, False)
    rope_s = _rope_tables(ls, True)

    xp = x_prompt.reshape(bp * lp, D_MODEL)
    xs = x_sample.reshape(bs * ls, D_MODEL)
    ckvs, kpes, states = [], [], []
    Ws = [_prep_layer_weights(l, w_in, mla_wqb, mla_wkvb, w_out, peer_wq) for l in range(depth)]
    mods_p = [mod_all[l, 0:1][:, None, :] for l in range(depth)]
    mods_s = [mod_all[l, 1:1 + bs][:, None, :] for l in range(depth)]

    def mixer_s(x, l):
        return _mixer_stage(x, mods_s[l], l, Ws[l], P, lb, bs, ls,
                            (cache_ckv[:, l], cache_kpe[:, l], state_hgrn[:, l]), rope_s)[0]

    def mixer_p(x, l):
        x, (ckv_l, kpe_l, st_l) = _mixer_stage(x, mods_p[l], l, Ws[l], P, lb, bp, lp, None, rope_p)
        ckvs.append(ckv_l.reshape(bp, lp, MLA_KV_LORA))
        kpes.append(kpe_l.reshape(bp, lp, MLA_ROPE))
        states.append(st_l)
        return x

    tm = TOK_TILE
    none_yet = jnp.zeros((8, D_MODEL), F32)
    xs = mixer_s(xs, 0)
    h2_s, idx_s, gate_s = _retrieve_stage(xs, mods_s[0], 0, Ws[0], P, ls, jnp.zeros((8, PEER_TK), jnp.int32))
    act_s = _peer_u(h2_s, idx_s, P['peer_u_packed'], none_yet)
    for l in range(depth):
        xp = mixer_p(xp, l)
        w_s = _peer_gelu(gate_s, act_s, tm)
        y_s = _peer_v(w_s, idx_s, P['peer_v_packed'], xp)
        h2_p, idx_p, gate_p = _retrieve_stage(xp, mods_p[l], l, Ws[l], P, lp, w_s)
        act_p = _peer_u(h2_p, idx_p, P['peer_u_packed'], y_s)
        xs = _peer_residual(xs, y_s, mods_s[l], ls, tm)
        if l + 1 < depth:
            xs = mixer_s(xs, l + 1)
        w_p = _peer_gelu(gate_p, act_p, tm)
        y_p = _peer_v(w_p, idx_p, P['peer_v_packed'], xs if l + 1 < depth else none_yet)
        if l + 1 < depth:
            h2_s, idx_s, gate_s = _retrieve_stage(xs, mods_s[l + 1], l + 1, Ws[l + 1], P, ls, w_p)
            act_s = _peer_u(h2_s, idx_s, P['peer_u_packed'], y_p)
        xp = _peer_residual(xp, y_p, mods_p[l], lp, tm)

    y_prompt = _final_norm(xp, final_norm[None], 256).reshape(bp, lp, D_MODEL)
    y_sample = _final_norm(xs, final_norm[None], 256).reshape(bs, ls, D_MODEL)
    return (y_prompt, y_sample, jnp.stack(ckvs, axis=1), jnp.stack(kpes, axis=1), jnp.stack(states, axis=1))
```

```python
import functools
import math

import numpy as np
import jax
import jax.numpy as jnp
from jax import lax
from jax.experimental import pallas as pl
from jax.experimental.pallas import tpu as pltpu
from jax.experimental.pallas import tpu_sc as plsc

F32 = jnp.float32
BF16 = jnp.bfloat16
HIGHEST = lax.Precision.HIGHEST

D_MODEL = 1024
EPS = 1e-6
GRID_W = 64
HG_HEADS = 8
HG_D = 64
HG_CHUNK = 32
HG_W = HG_HEADS * HG_D
FORGET_FLOOR = 1e-6
MLA_HEADS = 8
MLA_Q_LORA = 384
MLA_KV_LORA = 256
MLA_NOPE = 64
MLA_ROPE = 32
MLA_V = 64
MLA_SCALE = (MLA_NOPE + MLA_ROPE) ** -0.5
ROPE_AXIS_DIM = MLA_ROPE // 2
ROPE_BASE = 10000.0
PEER_HEADS = 8
PEER_NKEYS = 128
PEER_DQ = 256
PEER_TOPK = 16

LANES = 128
VMEM_LIMIT = 48 * 1024 * 1024

IN_HG = 5 * HG_W
IN_CQ0 = IN_HG
IN_CKV0 = IN_CQ0 + MLA_Q_LORA
IN_KPE0 = IN_CKV0 + MLA_KV_LORA
IN_PAD = IN_KPE0 + LANES


def _cparams(sem):
    return pltpu.CompilerParams(dimension_semantics=sem, vmem_limit_bytes=VMEM_LIMIT)


def _split_bf16(a):
    hi = a.astype(BF16)
    lo = (a - hi.astype(F32)).astype(BF16)
    return hi, lo


def _dot3(a, b, dims=(((1,), (0,)), ((), ()))):
    ah, al = _split_bf16(a)
    bh, bl = _split_bf16(b)
    d = functools.partial(lax.dot_general, dimension_numbers=dims, preferred_element_type=F32)
    return d(ah, bh) + d(al, bh) + d(ah, bl)


def _dot_sel(a, sel_bf):
    ah, al = _split_bf16(a)
    return (jnp.dot(ah, sel_bf, preferred_element_type=F32)
            + jnp.dot(al, sel_bf, preferred_element_type=F32))


def _rms_rows(x, g):
    return x * lax.rsqrt(jnp.mean(x * x, axis=-1, keepdims=True) + EPS) * g


def _ada_kernel(cond_ref, w_ref, b_ref, o_ref):
    cnd = cond_ref[...]
    s = cnd * jax.nn.sigmoid(cnd)
    o_ref[0] = jnp.dot(s, w_ref[0], precision=HIGHEST, preferred_element_type=F32) + b_ref[0]


def _ada_mod(cond8, w_ada, b_ada, l):
    depth, d, n6 = w_ada.shape
    tn = 1536
    return pl.pallas_call(
        _ada_kernel,
        grid=(n6 // tn,),
        in_specs=[pl.BlockSpec((8, d), lambda j: (0, 0)),
                  pl.BlockSpec((1, d, tn), lambda j: (l, 0, j)),
                  pl.BlockSpec((1, 1, tn), lambda j: (l, 0, j))],
        out_specs=pl.BlockSpec((1, 8, tn), lambda j: (0, 0, j)),
        out_shape=jax.ShapeDtypeStruct((1, 8, n6), F32),
        compiler_params=_cparams(("parallel",)),
        name="ada_mod",
    )(cond8, w_ada, b_ada.reshape(depth, 1, n6))[0]


def _inproj_kernel(x_ref, g_ref, mod_ref, w_ref, gq_ref, gkv_ref, hg_ref, cq_ref, ckv_ref, kpe_ref):
    x = x_ref[...]
    shift = mod_ref[0, :, 0:D_MODEL]
    scale = mod_ref[0, :, D_MODEL:2 * D_MODEL]
    h = (_rms_rows(x, g_ref[...]) * (1.0 + scale) + shift).astype(BF16)
    hg_ref[...] = jnp.dot(h, w_ref[:, 0:IN_HG], preferred_element_type=F32)
    cq = jnp.dot(h, w_ref[:, IN_CQ0:IN_CKV0], preferred_element_type=F32)
    cq_ref[...] = _rms_rows(cq, gq_ref[...]).astype(BF16)
    ckv = jnp.dot(h, w_ref[:, IN_CKV0:IN_KPE0], preferred_element_type=F32)
    ckv_ref[...] = _rms_rows(ckv, gkv_ref[...])
    kpe_ref[...] = jnp.dot(h, w_ref[:, IN_KPE0:IN_PAD], preferred_element_type=F32)


def _inproj(x, g, mod, w_pad, gq, gkv, seq_len, tm):
    n = x.shape[0]
    per_seq = seq_len // tm
    n_cond = mod.shape[0]
    cond_of = (lambda i: (0, 0, 0)) if n_cond == 1 else (lambda i: (i // per_seq, 0, 0))
    row = lambda i: (i, 0)
    const = lambda i: (0, 0)
    return pl.pallas_call(
        _inproj_kernel,
        grid=(n // tm,),
        in_specs=[pl.BlockSpec((tm, D_MODEL), row),
                  pl.BlockSpec((1, D_MODEL), const),
                  pl.BlockSpec((1, 1, 6 * D_MODEL), cond_of),
                  pl.BlockSpec((D_MODEL, IN_PAD), const),
                  pl.BlockSpec((1, MLA_Q_LORA), const),
                  pl.BlockSpec((1, MLA_KV_LORA), const)],
        out_specs=[pl.BlockSpec((tm, IN_HG), row),
                   pl.BlockSpec((tm, MLA_Q_LORA), row),
                   pl.BlockSpec((tm, MLA_KV_LORA), row),
                   pl.BlockSpec((tm, LANES), row)],
        out_shape=[jax.ShapeDtypeStruct((n, IN_HG), F32),
                   jax.ShapeDtypeStruct((n, MLA_Q_LORA), BF16),
                   jax.ShapeDtypeStruct((n, MLA_KV_LORA), F32),
                   jax.ShapeDtypeStruct((n, LANES), F32)],
        compiler_params=_cparams(("parallel",)),
        name="in_proj",
    )(x, g, mod, w_pad, gq, gkv)


HG_SUB = 8
HG_UNROLL = 2


def _hgrn_kernel(q_ref, ff_ref, fb_ref, v_ref, gt_ref, lbf_ref, lbb_ref, gn_ref, s0_ref,
                 out_ref, sfin_ref, qt_ref, acc_ref, u_ref, d_ref, *, seq_len):
    C, SB = HG_CHUNK, HG_SUB
    n_sub = C // SB
    n_chunks = seq_len // C
    r2 = lax.broadcasted_iota(jnp.int32, (LANES, LANES), 0) // HG_D
    c2 = lax.broadcasted_iota(jnp.int32, (LANES, LANES), 1) // HG_D
    same_head = r2 == c2
    sel_bf = same_head.astype(BF16)
    ti = lax.broadcasted_iota(jnp.int32, (C, C), 0)
    ui = lax.broadcasted_iota(jnp.int32, (C, C), 1)
    tri = ((ui <= ti).astype(F32), (ui >= ti).astype(F32))
    s3 = lax.broadcasted_iota(jnp.int32, (SB, SB, 1), 0)
    t3 = lax.broadcasted_iota(jnp.int32, (SB, SB, 1), 1)
    diag_mask = (t3 >= s3, t3 <= s3)
    key_row = lax.broadcasted_iota(jnp.int32, (C, 1), 0)
    lane1 = lax.broadcasted_iota(jnp.int32, (1, LANES), 1)
    head_lane = ((lane1 < HG_D).astype(F32), (lane1 >= HG_D).astype(F32))
    ones_c = jnp.ones((C, LANES), F32)
    tdims = (((0,), (0,)), ((), ()))
    nt = (((1,), (1,)), ((), ()))
    z_refs, lb_refs = (ff_ref, fb_ref), (lbf_ref, lbb_ref)

    off_blocks = (tuple(range(1, n_sub)), tuple(range(0, n_sub - 1)))
    n_off = n_sub - 1
    sr = lax.broadcasted_iota(jnp.int32, (n_off * 2 * SB, n_off * C), 0) // (2 * SB)
    sc = lax.broadcasted_iota(jnp.int32, (n_off * 2 * SB, n_off * C), 1) // C
    own_block = sr == sc

    units = [(ci, d) for ci in range(HG_UNROLL) for d in range(2)]

    def prep_body(it, carry):
        cs = [it * HG_UNROLL + ci for ci in range(HG_UNROLL)]
        rows = [pl.ds(pl.multiple_of(c * C, C), C) for c in cs]
        q = [q_ref[r, :] * (HG_D ** -0.5) for r in rows]
        v = [v_ref[r, :] for r in rows]
        v_bf = [x.astype(BF16) for x in v]
        v_rep = [jnp.concatenate([x] * n_off, axis=0) for x in v_bf]
        gs, ks = [], []
        for ci, d in units:
            lb = lb_refs[d][...]
            f = jnp.maximum(lb + (1.0 - lb) * jax.nn.sigmoid(z_refs[d][rows[ci], :]), FORGET_FLOOR)
            gs.append(jnp.log(f))
            ks.append(1.0 - f)
        bs = [jnp.dot(tri[d], gs[u], precision=HIGHEST, preferred_element_type=F32) for u, (ci, d) in enumerate(units)]
        totals = [lax.dot_general(g, ones_c, tdims, precision=HIGHEST, preferred_element_type=F32) for g in gs]
        incs, sums, scores = [], [], []
        for u, (ci, d) in enumerate(units):
            b_end = bs[u][C - 1:C, :] if d == 0 else bs[u][0:1, :]
            kk = (ks[u] * jnp.exp(b_end - bs[u])).astype(BF16)
            incs.append(lax.dot_general(kk, v_bf[ci], tdims, preferred_element_type=F32))
        for u, (ci, d) in enumerate(units):
            blocks = []
            for blk in range(n_sub):
                sl = slice(blk * SB, (blk + 1) * SB)
                bi = bs[u][sl]
                diff = bi[None, :, :] - bi[:, None, :]
                e = jnp.where(diag_mask[d], jnp.exp(jnp.where(diag_mask[d], diff, 0.0)), 0.0)
                blocks.append((e * q[ci][sl][None, :, :] * ks[u][sl][:, None, :]).reshape(SB * SB, LANES))
            sums.append(jnp.dot(jnp.concatenate(blocks, axis=0).astype(BF16), sel_bf, preferred_element_type=F32))
        for u, (ci, d) in enumerate(units):
            lhs, khs = [], []
            for blk in off_blocks[d]:
                sl = slice(blk * SB, (blk + 1) * SB)
                if d == 0:
                    r = bs[u][blk * SB - 1:blk * SB, :]
                    keys = key_row < blk * SB
                else:
                    r = bs[u][(blk + 1) * SB:(blk + 1) * SB + 1, :]
                    keys = key_row >= (blk + 1) * SB
                qh = q[ci][sl] * jnp.exp(bs[u][sl] - r)
                lhs += [qh * head_lane[0], qh * head_lane[1]]
                khs.append(jnp.where(keys, ks[u] * jnp.exp(jnp.minimum(r - bs[u], 0.0)), 0.0))
            scores.append(lax.dot_general(jnp.concatenate(lhs, axis=0).astype(BF16),
                                          jnp.concatenate(khs, axis=0).astype(BF16), nt,
                                          preferred_element_type=F32))
        o2s = [jnp.dot(jnp.where(own_block, scores[u], 0.0).astype(BF16), v_rep[ci], preferred_element_type=F32)
               for u, (ci, d) in enumerate(units)]
        for u, (ci, d) in enumerate(units):
            outs = []
            for blk in range(n_sub):
                sl = slice(blk * SB, (blk + 1) * SB)
                a_blk = sums[u][blk * SB * SB:(blk + 1) * SB * SB].reshape(SB, SB, LANES)
                o_blk = jnp.sum(a_blk * v[ci][sl][:, None, :], axis=0)
                if blk in off_blocks[d]:
                    j = off_blocks[d].index(blk) * 2 * SB
                    o_blk = o_blk + o2s[u][j:j + SB] * head_lane[0] + o2s[u][j + SB:j + 2 * SB] * head_lane[1]
                outs.append(o_blk)
            qt_ref[d, rows[ci], :] = (q[ci] * jnp.exp(bs[u])).astype(BF16)
            u_ref[d, cs[ci]] = jnp.where(same_head, incs[u], 0.0)
            d_ref[d, cs[ci]] = jnp.exp(totals[u])
            acc_ref[d, rows[ci], :] = jnp.concatenate(outs, axis=0)
        return carry

    lax.fori_loop(0, n_chunks // HG_UNROLL, prep_body, 0)

    def block_diag(s_a, s_b):
        z = jnp.zeros((HG_D, HG_D), F32)
        return jnp.concatenate([jnp.concatenate([s_a, z], axis=1),
                                jnp.concatenate([z, s_b], axis=1)], axis=0)

    def scan_body(i, carry):
        s_f, s_b = carry
        cb = n_chunks - 1 - i
        inc_f, inc_b = u_ref[0, i], u_ref[1, cb]
        u_ref[0, i] = s_f
        u_ref[1, cb] = s_b
        return d_ref[0, i] * s_f + inc_f, d_ref[1, cb] * s_b + inc_b

    s_f, s_b = lax.fori_loop(0, n_chunks, scan_body,
                             (block_diag(s0_ref[0, 0, 0], s0_ref[0, 0, 1]),
                              block_diag(s0_ref[0, 1, 0], s0_ref[0, 1, 1])))
    sfin_ref[0, 0, 0] = s_f[0:HG_D, 0:HG_D]
    sfin_ref[0, 0, 1] = s_f[HG_D:, HG_D:]
    sfin_ref[0, 1, 0] = s_b[0:HG_D, 0:HG_D]
    sfin_ref[0, 1, 1] = s_b[HG_D:, HG_D:]

    bdims = (((2,), (1,)), ((0,), (0,)))
    o = acc_ref[0] + acc_ref[1]
    for d in range(2):
        inter = lax.dot_general(qt_ref[d].reshape(n_chunks, C, LANES), u_ref[d].astype(BF16), bdims,
                                preferred_element_type=F32)
        o = o + inter.reshape(seq_len, LANES)
    ms = _dot_sel(o * o, sel_bf) * (1.0 / HG_D)
    gt = gt_ref[...]
    out_ref[...] = (o * lax.rsqrt(ms + EPS) * gn_ref[...] * (gt * jax.nn.sigmoid(gt))).astype(BF16)


def _hgrn(hg, lb_f, lb_b, gn2, s0, batch, seq_len):
    n = hg.shape[0]
    pairs = HG_W // LANES
    part = lambda p: pl.BlockSpec((seq_len, LANES), lambda b, j, p=p: (b, p * pairs + j))
    lane_row = pl.BlockSpec((1, LANES), lambda b, j: (0, j))
    st = pl.BlockSpec((1, 2, 2, HG_D, HG_D), lambda b, j: (b, 0, j, 0, 0))
    return pl.pallas_call(
        functools.partial(_hgrn_kernel, seq_len=seq_len),
        grid=(batch, pairs),
        in_specs=[part(0), part(1), part(2), part(3), part(4), lane_row, lane_row,
                  pl.BlockSpec((1, LANES), lambda b, j: (0, 0)), st],
        out_specs=[pl.BlockSpec((seq_len, LANES), lambda b, j: (b, j)), st],
        out_shape=[jax.ShapeDtypeStruct((n, HG_W), BF16),
                   jax.ShapeDtypeStruct((batch, 2, HG_HEADS, HG_D, HG_D), F32)],
        scratch_shapes=[pltpu.VMEM((2, seq_len, LANES), BF16),
                        pltpu.VMEM((2, seq_len, LANES), F32),
                        pltpu.VMEM((2, seq_len // HG_CHUNK, LANES, LANES), F32),
                        pltpu.VMEM((2, seq_len // HG_CHUNK, LANES, LANES), F32)],
        compiler_params=_cparams(("parallel", "parallel")),
        name="hgrn2",
    )(hg, hg, hg, hg, hg, lb_f, lb_b, gn2, s0)


def _rope_lanes(x, cos, sin):
    return x * cos + pltpu.roll(x, 96, 1) * sin


def _qproj_kernel(cq_ref, w_ref, cos_ref, sin_ref, q_ref):
    cq = cq_ref[...]
    for h in range(MLA_HEADS):
        qh = jnp.dot(cq, w_ref[h], preferred_element_type=F32)
        q_ref[h] = (_rope_lanes(qh, cos_ref[...], sin_ref[...]) * MLA_SCALE).astype(BF16)


def _kvproj_kernel(ckv_ref, kpe_ref, w_ref, cos_ref, sin_ref, k_ref, kv_ref):
    ckv = ckv_ref[...].astype(BF16)
    kpe = _rope_lanes(kpe_ref[...], cos_ref[...], sin_ref[...])
    nope = lax.broadcasted_iota(jnp.int32, kpe.shape, 1) < MLA_NOPE
    for h in range(MLA_HEADS):
        kvh = jnp.dot(ckv, w_ref[h], preferred_element_type=F32)
        kv_ref[h] = kvh.astype(BF16)
        k_ref[h] = jnp.where(nope, kvh, kpe).astype(BF16)


def _qproj(cqn, wq, cos, sin, tm):
    n = cqn.shape[0]
    pos_blocks = cos.shape[0] // tm
    return pl.pallas_call(
        _qproj_kernel,
        grid=(n // tm,),
        in_specs=[pl.BlockSpec((tm, MLA_Q_LORA), lambda i: (i, 0)),
                  pl.BlockSpec((MLA_HEADS, MLA_Q_LORA, LANES), lambda i: (0, 0, 0)),
                  pl.BlockSpec((tm, LANES), lambda i: (i % pos_blocks, 0)),
                  pl.BlockSpec((tm, LANES), lambda i: (i % pos_blocks, 0))],
        out_specs=pl.BlockSpec((MLA_HEADS, tm, LANES), lambda i: (0, i, 0)),
        out_shape=jax.ShapeDtypeStruct((MLA_HEADS, n, LANES), BF16),
        compiler_params=_cparams(("parallel",)),
        name="mla_q_proj",
    )(cqn, wq, cos, sin)


def _kvproj(ckvn, kpe128, wkv, cos, sin, tm):
    n = ckvn.shape[0]
    pos_blocks = cos.shape[0] // tm
    hm = pl.BlockSpec((MLA_HEADS, tm, LANES), lambda i: (0, i, 0))
    return pl.pallas_call(
        _kvproj_kernel,
        grid=(n // tm,),
        in_specs=[pl.BlockSpec((tm, MLA_KV_LORA), lambda i: (i, 0)),
                  pl.BlockSpec((tm, LANES), lambda i: (i, 0)),
                  pl.BlockSpec((MLA_HEADS, MLA_KV_LORA, LANES), lambda i: (0, 0, 0)),
                  pl.BlockSpec((tm, LANES), lambda i: (i % pos_blocks, 0)),
                  pl.BlockSpec((tm, LANES), lambda i: (i % pos_blocks, 0))],
        out_specs=[hm, hm],
        out_shape=[jax.ShapeDtypeStruct((MLA_HEADS, n, LANES), BF16)] * 2,
        compiler_params=_cparams(("parallel",)),
        name="mla_kv_proj",
    )(ckvn, kpe128, wkv, cos, sin)


def _attn_kernel(q_ref, k_ref, kv_ref, o_ref):
    low = lax.broadcasted_iota(jnp.int32, (q_ref.shape[1], LANES), 1) < MLA_V
    nt = (((1,), (1,)), ((), ()))
    outs = []
    for h in range(MLA_HEADS):
        s = lax.dot_general(q_ref[h], k_ref[h], nt, preferred_element_type=F32)
        p = jnp.exp(s - jnp.max(s, axis=-1, keepdims=True))
        denom = jnp.sum(p, axis=-1, keepdims=True)
        o = jnp.dot(p.astype(BF16), kv_ref[h], preferred_element_type=F32) / denom
        outs.append(o)
    for j in range(MLA_HEADS // 2):
        pair = jnp.where(low, pltpu.roll(outs[2 * j], MLA_V, 1), outs[2 * j + 1])
        o_ref[:, j * LANES:(j + 1) * LANES] = pair.astype(BF16)


def _attention(q128, k128, kv128, batch, lq, lk, tq):
    n = q128.shape[1]
    qb = lq // tq
    kspec = pl.BlockSpec((MLA_HEADS, lk, LANES), lambda b, i: (0, b, 0))
    return pl.pallas_call(
        _attn_kernel,
        grid=(batch, qb),
        in_specs=[pl.BlockSpec((MLA_HEADS, tq, LANES), lambda b, i: (0, b * qb + i, 0)), kspec, kspec],
        out_specs=pl.BlockSpec((tq, MLA_HEADS * MLA_V), lambda b, i: (b * qb + i, 0)),
        out_shape=jax.ShapeDtypeStruct((n, MLA_HEADS * MLA_V), BF16),
        compiler_params=_cparams(("parallel", "parallel")),
        name="mla_attention",
    )(q128, k128, kv128)


def _outproj_kernel(x_ref, hg_ref, at_ref, w_ref, mod_ref, o_ref):
    gate = mod_ref[0, :, 2 * D_MODEL:3 * D_MODEL]
    mix = (jnp.dot(hg_ref[...], w_ref[0:HG_W, :], preferred_element_type=F32)
           + jnp.dot(at_ref[...], w_ref[HG_W:, :], preferred_element_type=F32))
    o_ref[...] = x_ref[...] + gate * mix


def _outproj(x, hg_out, attn, w_out, mod, seq_len, tm):
    n = x.shape[0]
    per_seq = seq_len // tm
    n_cond = mod.shape[0]
    cond_of = (lambda i: (0, 0, 0)) if n_cond == 1 else (lambda i: (i // per_seq, 0, 0))
    row = lambda i: (i, 0)
    return pl.pallas_call(
        _outproj_kernel,
        grid=(n // tm,),
        in_specs=[pl.BlockSpec((tm, D_MODEL), row),
                  pl.BlockSpec((tm, HG_W), row),
                  pl.BlockSpec((tm, MLA_HEADS * MLA_V), row),
                  pl.BlockSpec((2 * HG_W, D_MODEL), lambda i: (0, 0)),
                  pl.BlockSpec((1, 1, 6 * D_MODEL), cond_of)],
        out_specs=pl.BlockSpec((tm, D_MODEL), row),
        out_shape=jax.ShapeDtypeStruct((n, D_MODEL), F32),
        compiler_params=_cparams(("parallel",)),
        name="out_proj",
    )(x, hg_out, attn, w_out, mod)


_CAND = [(a, b) for a in range(PEER_TOPK) for b in range(PEER_TOPK) if (a + 1) * (b + 1) <= PEER_TOPK]
_CAND_PAD = -(-len(_CAND) // 8) * 8


def _topk_rows(work, k):
    rows = lax.broadcasted_iota(jnp.int32, work.shape, 0).astype(F32)
    n_rows = float(work.shape[0])
    vals, idxs = [], []
    for _ in range(k):
        m = jnp.max(work, axis=0, keepdims=True)
        sel = jnp.min(jnp.where(work == m, rows, n_rows), axis=0, keepdims=True)
        vals.append(m)
        idxs.append(sel)
        work = jnp.where(rows == sel, -jnp.inf, work)
    return jnp.concatenate(vals, axis=0), jnp.concatenate(idxs, axis=0)


def _peerq_kernel(x_ref, g_ref, mod_ref, w_ref, k1_ref, k2_ref, after_ref, h2_ref, idx_ref, gate_ref, *,
                  idx_offset):
    del after_ref
    idx_rows, gate_rows = [], []
    x = x_ref[...]
    shift = mod_ref[0, :, 3 * D_MODEL:4 * D_MODEL]
    scale = mod_ref[0, :, 4 * D_MODEL:5 * D_MODEL]
    h2 = _rms_rows(x, g_ref[...]) * (1.0 + scale) + shift
    h2b = h2.astype(BF16)
    bits = lax.bitcast_convert_type(h2b.astype(F32), jnp.uint32)
    h2_ref[...] = (bits[:, :PACK_W] >> 16) | bits[:, PACK_W:]
    qp = jnp.dot(h2b, w_ref[...], preferred_element_type=F32)
    nt = (((1,), (1,)), ((), ()))
    half = PEER_DQ // 2
    for h in range(PEER_HEADS):
        q1 = qp[:, h * PEER_DQ:h * PEER_DQ + half]
        q2 = qp[:, h * PEER_DQ + half:(h + 1) * PEER_DQ]
        s1 = _dot3(k1_ref[h], q1, nt)
        s2 = _dot3(k2_ref[h], q2, nt)
        v1, i1 = _topk_rows(s1, PEER_TOPK)
        v2, i2 = _topk_rows(s2, PEER_TOPK)
        cs = [v1[a:a + 1, :] + v2[b:b + 1, :] for a, b in _CAND]
        ci = [i1[a:a + 1, :] * PEER_NKEYS + i2[b:b + 1, :] for a, b in _CAND]
        n_pad = _CAND_PAD - len(_CAND)
        if n_pad:
            cs.append(jnp.full((n_pad, x.shape[0]), -jnp.inf, F32))
            ci.append(jnp.zeros((n_pad, x.shape[0]), F32))
        cand_s = jnp.concatenate(cs, axis=0)
        cand_i = jnp.concatenate(ci, axis=0)
        best_s, best_pos = _topk_rows(cand_s, PEER_TOPK)
        rows = lax.broadcasted_iota(jnp.int32, cand_s.shape, 0).astype(F32)
        picked = [jnp.sum(jnp.where(rows == best_pos[r:r + 1, :], cand_i, 0.0), axis=0, keepdims=True)
                  for r in range(PEER_TOPK)]
        idx_rows.extend(picked)
        ex = jnp.exp(best_s - best_s[0:1, :])
        gate_rows.append(ex / jnp.sum(ex, axis=0, keepdims=True))
    idx_ref[...] = jnp.concatenate(idx_rows, axis=0).T.astype(jnp.int32) + idx_offset
    gate_ref[...] = jnp.concatenate(gate_rows, axis=0).T


def _peerq(x, g, mod, wq, k1, k2, seq_len, tm, idx_offset, after):
    n = x.shape[0]
    per_seq = seq_len // tm
    n_cond = mod.shape[0]
    cond_of = (lambda i: (0, 0, 0)) if n_cond == 1 else (lambda i: (i // per_seq, 0, 0))
    kspec = pl.BlockSpec((PEER_HEADS, PEER_NKEYS, PEER_DQ // 2), lambda i: (0, 0, 0))
    tk = PEER_HEADS * PEER_TOPK
    return pl.pallas_call(
        functools.partial(_peerq_kernel, idx_offset=idx_offset),
        grid=(n // tm,),
        in_specs=[pl.BlockSpec((tm, D_MODEL), lambda i: (i, 0)),
                  pl.BlockSpec((1, D_MODEL), lambda i: (0, 0)),
                  pl.BlockSpec((1, 1, 6 * D_MODEL), cond_of),
                  pl.BlockSpec((D_MODEL, PEER_HEADS * PEER_DQ), lambda i: (0, 0)),
                  kspec, kspec,
                  pl.BlockSpec(memory_space=pl.ANY)],
        out_specs=[pl.BlockSpec((tm, D_MODEL // 2), lambda i: (i, 0)),
                   pl.BlockSpec((tm, tk), lambda i: (i, 0)),
                   pl.BlockSpec((tm, tk), lambda i: (i, 0))],
        out_shape=[jax.ShapeDtypeStruct((n, D_MODEL // 2), jnp.uint32),
                   jax.ShapeDtypeStruct((n, tk), jnp.int32),
                   jax.ShapeDtypeStruct((n, tk), F32)],
        compiler_params=_cparams(("parallel",)),
        name="peer_retrieve",
    )(x, g, mod, wq, k1, k2, after)


SC_CORES = 2
SC_SUBCORES = 16
SC_LANES = 16
SC_TILES = SC_CORES * SC_SUBCORES
SC_NBUF_U = 5
SC_NBUF_V = 5
ITEM_HEADS = 2
ITEM_ROWS = ITEM_HEADS * PEER_TOPK
ITEMS_PER_TOKEN = PEER_HEADS // ITEM_HEADS
PEER_TK = PEER_HEADS * PEER_TOPK
D_VREGS = D_MODEL // SC_LANES
BF_RUN = 4


PACK_W = D_MODEL // 2


def _pack_kernel(t_ref, o_ref):
    bits = lax.bitcast_convert_type(t_ref[...].astype(BF16).astype(F32), jnp.uint32)
    o_ref[...] = (bits[:, :PACK_W] >> 16) | bits[:, PACK_W:]


def _pack_bf16_pairs(tables, l, tm=1024):
    n = tables.shape[1]
    return pl.pallas_call(
        _pack_kernel,
        grid=(n // tm,),
        in_specs=[pl.BlockSpec((None, tm, D_MODEL), lambda i: (l, i, 0))],
        out_specs=pl.BlockSpec((tm, PACK_W), lambda i: (i, 0)),
        out_shape=jax.ShapeDtypeStruct((n, PACK_W), jnp.uint32),
        compiler_params=_cparams(("parallel",)),
        name="pack_expert_table",
    )(tables)


def _unpack_pair(word):
    lo = lax.bitcast_convert_type(word << 16, F32)
    hi = lax.bitcast_convert_type(word & jnp.uint32(0xFFFF0000), F32)
    return lo, hi


def _sc_mesh():
    return plsc.VectorSubcoreMesh(core_axis_name="c", subcore_axis_name="s")


def _sc_params():
    return pltpu.CompilerParams(needs_layout_passes=False)


def _sc_token_base(tokens_per_tile):
    return (lax.axis_index("s") * SC_CORES + lax.axis_index("c")) * tokens_per_tile


def _peer_u_body(h2_hbm, idx_hbm, u_hbm, after_hbm, act_hbm, idx_v, act_v, x_v, rows_v, sem_x, sem_g, *, tpt,
                 nbuf):
    del after_hbm
    base = _sc_token_base(tpt)
    n_items = tpt * ITEMS_PER_TOKEN
    pltpu.sync_copy(idx_hbm.at[pl.ds(base, tpt)], idx_v)

    def gather(i, slot):
        t, g = i // ITEMS_PER_TOKEN, i % ITEMS_PER_TOKEN
        return pltpu.make_async_copy(u_hbm.at[idx_v.at[t, pl.ds(g * ITEM_ROWS, ITEM_ROWS)]],
                                     rows_v.at[slot], sem_g.at[slot])

    def x_copy(t):
        return pltpu.make_async_copy(h2_hbm.at[base + t], x_v.at[t % 2], sem_x.at[t % 2])

    x_copy(0).start()
    for b in range(nbuf - 1):
        gather(b, b).start()
    lane = lax.iota(jnp.int32, SC_LANES)
    zero = jnp.zeros((SC_LANES,), F32)

    @pl.loop(0, n_items)
    def _(i):
        for b in (i % nbuf,):
            t, g = i // ITEMS_PER_TOKEN, i % ITEMS_PER_TOKEN

            @pl.when(g == 0)
            def _():
                x_copy(t).wait()

                @pl.when(t + 1 < tpt)
                def _():
                    x_copy(t + 1).start()

            @pl.when(i + nbuf - 1 < n_items)
            def _():
                gather(i + nbuf - 1, (b + nbuf - 1) % nbuf).start()

            gather(i, b).wait()
            xs = t % 2

            for sub in range(ITEM_HEADS):
                r0 = sub * PEER_TOPK

                def cbody(cg, accs, r0=r0):
                    parts = [None] * PEER_TOPK
                    for cc in range(BF_RUN):
                        col = pl.ds((cg * BF_RUN + cc) * SC_LANES, SC_LANES)
                        xb = plsc.bitcast(x_v[xs, col], BF16)
                        for j in range(PEER_TOPK):
                            p = xb * plsc.bitcast(rows_v[b, r0 + j, col], BF16)
                            parts[j] = p if cc == 0 else parts[j] + p
                    out = []
                    for j, a in enumerate(accs):
                        lo, hi = _unpack_pair(plsc.bitcast(parts[j], jnp.uint32))
                        out.append(a + lo + hi)
                    return tuple(out)

                accs = plsc.parallel_loop(0, PACK_W // SC_LANES // BF_RUN, carry=(zero,) * PEER_TOPK)(cbody)
                vec = zero
                for j in range(PEER_TOPK):
                    vec = jnp.where(lane == j, jnp.sum(accs[j]), vec)
                act_v[t, pl.ds(g * ITEM_ROWS + r0, PEER_TOPK)] = vec

    pltpu.sync_copy(act_v, act_hbm.at[pl.ds(base, tpt)])


def _peer_v_body(w_hbm, idx_hbm, v_hbm, after_hbm, y_hbm, idx_v, w_v, y_v, rows_v, sem_y, sem_g, *, tpt,
                 nbuf):
    del after_hbm
    base = _sc_token_base(tpt)
    n_items = tpt * ITEMS_PER_TOKEN
    pltpu.sync_copy(idx_hbm.at[pl.ds(base, tpt)], idx_v)
    pltpu.sync_copy(w_hbm.at[pl.ds(base, tpt)], w_v)

    def gather(i, slot):
        t, g = i // ITEMS_PER_TOKEN, i % ITEMS_PER_TOKEN
        return pltpu.make_async_copy(v_hbm.at[idx_v.at[t, pl.ds(g * ITEM_ROWS, ITEM_ROWS)]],
                                     rows_v.at[slot], sem_g.at[slot])

    def y_copy(t):
        return pltpu.make_async_copy(y_v.at[t % 2], y_hbm.at[base + t], sem_y.at[t % 2])

    for b in range(nbuf - 1):
        gather(b, b).start()
    lane = lax.iota(jnp.int32, SC_LANES)
    zero = jnp.zeros((SC_LANES,), F32)

    @pl.loop(0, n_items)
    def _(i):
        for b in (i % nbuf,):
            t, g = i // ITEMS_PER_TOKEN, i % ITEMS_PER_TOKEN
            ys = t % 2

            @pl.when(i + nbuf - 1 < n_items)
            def _():
                gather(i + nbuf - 1, (b + nbuf - 1) % nbuf).start()

            @pl.when(g == 0)
            def _():
                @pl.when(t >= 2)
                def _():
                    y_copy(t - 2).wait()

                @plsc.parallel_loop(0, D_VREGS, unroll=8)
                def _(c):
                    y_v[ys, pl.ds(c * SC_LANES, SC_LANES)] = zero

            gather(i, b).wait()
            for sub in range(ITEM_HEADS):
                r0 = sub * PEER_TOPK
                wv = w_v[t, pl.ds(g * ITEM_ROWS + r0, PEER_TOPK)]
                ws = [plsc.bitcast(jnp.broadcast_to(jnp.sum(jnp.where(lane == j, wv, 0)), (SC_LANES,)), BF16)
                      for j in range(PEER_TOPK)]

                @plsc.parallel_loop(0, PACK_W // SC_LANES)
                def _(c, r0=r0, ws=ws):
                    col = pl.ds(c * SC_LANES, SC_LANES)
                    col_hi = pl.ds(PACK_W + c * SC_LANES, SC_LANES)
                    terms = [ws[j] * plsc.bitcast(rows_v[b, r0 + j, col], BF16) for j in range(PEER_TOPK)]
                    while len(terms) > 1:
                        terms = [terms[k] + terms[k + 1] for k in range(0, len(terms), 2)]
                    lo, hi = _unpack_pair(plsc.bitcast(terms[0], jnp.uint32))
                    y_v[ys, col] = y_v[ys, col] + lo
                    y_v[ys, col_hi] = y_v[ys, col_hi] + hi

            @pl.when(g == ITEMS_PER_TOKEN - 1)
            def _():
                y_copy(t).start()

    y_copy(tpt - 2).wait()
    y_copy(tpt - 1).wait()


def _sc_scratch(tpt, side_dtype, token_buf, nbuf):
    return [pltpu.VMEM((tpt, PEER_TK), jnp.int32),
            pltpu.VMEM((tpt, PEER_TK), side_dtype),
            token_buf,
            pltpu.VMEM((nbuf, ITEM_ROWS, PACK_W), jnp.uint32),
            pltpu.SemaphoreType.DMA((2,)),
            pltpu.SemaphoreType.DMA((nbuf,))]


def _peer_u(h2, idx, u_all, after):
    n = h2.shape[0]
    tpt = n // SC_TILES
    return pl.kernel(
        functools.partial(_peer_u_body, tpt=tpt, nbuf=SC_NBUF_U),
        out_type=jax.ShapeDtypeStruct((n, PEER_TK), F32),
        mesh=_sc_mesh(),
        scratch_types=_sc_scratch(tpt, F32, pltpu.VMEM((2, PACK_W), jnp.uint32), SC_NBUF_U),
        compiler_params=_sc_params(),
        name="peer_u_sc",
    )(h2, idx, u_all, after)


def _peer_v(w, idx, v_all, after):
    n = w.shape[0]
    tpt = n // SC_TILES
    return pl.kernel(
        functools.partial(_peer_v_body, tpt=tpt, nbuf=SC_NBUF_V),
        out_type=jax.ShapeDtypeStruct((n, D_MODEL), F32),
        mesh=_sc_mesh(),
        scratch_types=_sc_scratch(tpt, jnp.int32, pltpu.VMEM((2, D_MODEL), F32), SC_NBUF_V),
        compiler_params=_sc_params(),
        name="peer_v_sc",
    )(w, idx, v_all, after)


def _peer_gelu_kernel(gate_ref, act_ref, w_ref):
    act = act_ref[...]
    w = gate_ref[...] * (0.5 * act * (1.0 + lax.erf(act * (2.0 ** -0.5))))
    bits = lax.bitcast_convert_type(w.astype(BF16).astype(F32), jnp.uint32)
    w_ref[...] = lax.bitcast_convert_type(bits | (bits >> 16), jnp.int32)


def _peer_gelu(gate, act, tm):
    n = gate.shape[0]
    spec = pl.BlockSpec((tm, PEER_TK), lambda i: (i, 0))
    return pl.pallas_call(
        _peer_gelu_kernel,
        grid=(n // tm,),
        in_specs=[spec, spec],
        out_specs=spec,
        out_shape=jax.ShapeDtypeStruct((n, PEER_TK), jnp.int32),
        compiler_params=_cparams(("parallel",)),
        name="peer_gelu",
    )(gate, act)


def _peer_residual_kernel(x_ref, y_ref, mod_ref, o_ref):
    o_ref[...] = x_ref[...] + mod_ref[0, :, 5 * D_MODEL:6 * D_MODEL] * y_ref[...]


def _peer_residual(x, y, mod, seq_len, tm):
    n = x.shape[0]
    per_seq = seq_len // tm
    n_cond = mod.shape[0]
    cond_of = (lambda i: (0, 0, 0)) if n_cond == 1 else (lambda i: (i // per_seq, 0, 0))
    row = pl.BlockSpec((tm, D_MODEL), lambda i: (i, 0))
    return pl.pallas_call(
        _peer_residual_kernel,
        grid=(n // tm,),
        in_specs=[row, row, pl.BlockSpec((1, 1, 6 * D_MODEL), cond_of)],
        out_specs=row,
        out_shape=jax.ShapeDtypeStruct((n, D_MODEL), F32),
        compiler_params=_cparams(("parallel",)),
        name="peer_residual",
    )(x, y, mod)


def _final_norm_kernel(x_ref, g_ref, o_ref):
    o_ref[...] = _rms_rows(x_ref[...], g_ref[...])


def _final_norm(x, g, tm):
    n = x.shape[0]
    return pl.pallas_call(
        _final_norm_kernel,
        grid=(n // tm,),
        in_specs=[pl.BlockSpec((tm, D_MODEL), lambda i: (i, 0)), pl.BlockSpec((1, D_MODEL), lambda i: (0, 0))],
        out_specs=pl.BlockSpec((tm, D_MODEL), lambda i: (i, 0)),
        out_shape=jax.ShapeDtypeStruct((n, D_MODEL), F32),
        compiler_params=_cparams(("parallel",)),
        name="final_norm",
    )(x, g)


def _rot_half_columns(w):
    half = ROPE_AXIS_DIM // 2
    w4 = w.reshape(w.shape[:-1] + (2, 2, half))
    return jnp.stack([-w4[..., 1, :], w4[..., 0, :]], axis=-2).reshape(w.shape)


def _rope_tables(seq_len, rotate):
    cos = np.zeros((seq_len, LANES), np.float32)
    sin = np.zeros((seq_len, LANES), np.float32)
    cos[:, :MLA_NOPE + MLA_ROPE] = 1.0
    if rotate:
        pos = np.arange(seq_len)
        half = ROPE_AXIS_DIM // 2
        inv = (1.0 / (ROPE_BASE ** (np.arange(0, ROPE_AXIS_DIM, 2, dtype=np.float32) / ROPE_AXIS_DIM))).astype(np.float32)
        for axis, p in enumerate(((pos // GRID_W).astype(np.float32), (pos % GRID_W).astype(np.float32))):
            ang = p[:, None] * inv[None, :]
            for part in range(2):
                lo = MLA_NOPE + axis * ROPE_AXIS_DIM + part * half
                cos[:, lo:lo + half] = np.cos(ang)
                sin[:, lo:lo + half] = np.sin(ang)
    return jnp.asarray(cos), jnp.asarray(sin)


def _prep_layer_weights(l, w_in, mla_wqb, mla_wkvb, w_out, peer_wq):
    wi = w_in[l]
    kpe_w = wi[:, IN_KPE0:IN_KPE0 + MLA_ROPE]
    w_pad = jnp.concatenate(
        [wi[:, :IN_KPE0], jnp.zeros((D_MODEL, MLA_NOPE), F32), kpe_w, _rot_half_columns(kpe_w)], axis=1).astype(BF16)
    wq = mla_wqb[l].reshape(MLA_Q_LORA, MLA_HEADS, MLA_NOPE + MLA_ROPE)
    wq_pe = wq[..., MLA_NOPE:]
    wq128 = jnp.concatenate([wq, _rot_half_columns(wq_pe)], axis=-1).transpose(1, 0, 2).astype(BF16)
    wkv128 = mla_wkvb[l].reshape(MLA_KV_LORA, MLA_HEADS, MLA_NOPE + MLA_V).transpose(1, 0, 2).astype(BF16)
    return w_pad, wq128, wkv128, w_out[l].astype(BF16), peer_wq[l].astype(BF16)


TOK_TILE = 256


def _mixer_stage(x, mod, l, W, P, lb, batch, seq_len, ctx, rope):
    w_pad, wq128, wkv128, wo, _ = W
    tm = TOK_TILE
    hg, cqn, ckvn, kpe128 = _inproj(x, P['norm_mix'][l][None], mod, w_pad, P['mla_q_norm'][l][None],
                                    P['mla_kv_norm'][l][None], seq_len, tm)
    if ctx is None:
        s0 = jnp.zeros((batch, 2, HG_HEADS, HG_D, HG_D), F32)
    else:
        s0 = ctx[2]
    gn2 = jnp.tile(P['hgrn_norm'][l], 2)[None]
    hg_out, s_fin = _hgrn(hg, lb[0, l][None], lb[1, l][None], gn2, s0, batch, seq_len)

    cos, sin = rope
    q128 = _qproj(cqn, wq128, cos, sin, tm)
    k128, kv128 = _kvproj(ckvn, kpe128, wkv128, cos, sin, tm)
    lk = seq_len
    if ctx is not None:
        past = ctx[0].shape[1]
        cos_c, sin_c = _rope_tables(past, False)
        kpe_c = jnp.pad(ctx[1].reshape(batch * past, MLA_ROPE), ((0, 0), (MLA_NOPE, LANES - MLA_NOPE - MLA_ROPE)))
        k_c, kv_c = _kvproj(ctx[0].reshape(batch * past, MLA_KV_LORA), kpe_c, wkv128, cos_c, sin_c, past)
        join = lambda a, b: jnp.concatenate(
            [a.reshape(MLA_HEADS, batch, past, LANES), b.reshape(MLA_HEADS, batch, seq_len, LANES)],
            axis=2).reshape(MLA_HEADS, batch * (past + seq_len), LANES)
        k128, kv128 = join(k_c, k128), join(kv_c, kv128)
        lk = past + seq_len
    attn = _attention(q128, k128, kv128, batch, seq_len, lk, 256)
    x = _outproj(x, hg_out, attn, wo, mod, seq_len, tm)
    return x, (ckvn, kpe128[:, MLA_NOPE:MLA_NOPE + MLA_ROPE], s_fin)


def _retrieve_stage(x, mod, l, W, P, seq_len, after):
    keys = P['peer_keys'][l]
    return _peerq(x, P['norm_ffn'][l][None], mod, W[4], keys[0], keys[1], seq_len, TOK_TILE, 0, after)


def kernel(x_prompt, x_sample, cache_ckv, cache_kpe, state_hgrn, c, c_ctx, w_ada, b_ada, norm_mix, w_in,
           hgrn_lb_logits, hgrn_norm, mla_q_norm, mla_wqb, mla_kv_norm, mla_wkvb, w_out, norm_ffn,
           peer_wq, peer_keys, peer_u, peer_v, final_norm):
    P = {'norm_mix': norm_mix, 'hgrn_norm': hgrn_norm, 'mla_q_norm': mla_q_norm, 'mla_kv_norm': mla_kv_norm,
         'norm_ffn': norm_ffn, 'peer_keys': peer_keys}
    depth = w_in.shape[0]
    u_packed = [_pack_bf16_pairs(peer_u, l) for l in range(depth)]
    v_packed = [_pack_bf16_pairs(peer_v, l) for l in range(depth)]
    bp, lp, _ = x_prompt.shape
    bs, ls, _ = x_sample.shape

    p = jax.nn.softmax(hgrn_lb_logits.astype(F32), axis=1)
    lb = jnp.cumsum(p, axis=1) - p[:, :1]

    cond8 = jnp.zeros((8, D_MODEL), F32).at[0].set(c_ctx).at[1:1 + bs].set(c)
    mod_all = [_ada_mod(cond8, w_ada, b_ada, l) for l in range(depth)]

    rope_p = _rope_tables(lp, False)
    rope_s = _rope_tables(ls, True)

    ckvs, kpes, states = [], [], []
    Ws = [_prep_layer_weights(l, w_in, mla_wqb, mla_wkvb, w_out, peer_wq) for l in range(depth)]
    mods_p = [mod_all[l][0:1][:, None, :] for l in range(depth)]
    mods_s = [mod_all[l][1:1 + bs][:, None, :] for l in range(depth)]

    def mixer_s(x, l):
        return _mixer_stage(x, mods_s[l], l, Ws[l], P, lb, bs, ls,
                            (cache_ckv[:, l], cache_kpe[:, l], state_hgrn[:, l]), rope_s)[0]

    def mixer_p(x, l):
        x, (ckv_l, kpe_l, st_l) = _mixer_stage(x, mods_p[l], l, Ws[l], P, lb, bp, lp, None, rope_p)
        ckvs.append(ckv_l.reshape(bp, lp, MLA_KV_LORA))
        kpes.append(kpe_l.reshape(bp, lp, MLA_ROPE))
        states.append(st_l)
        return x

    tm = TOK_TILE
    mixers, mods, lens = (mixer_p, mixer_s), (mods_p, mods_s), (lp, ls)
    x = [x_prompt.reshape(bp * lp, D_MODEL), x_sample.reshape(bs * ls, D_MODEL)]
    none_yet = jnp.zeros((8, D_MODEL), F32)

    def retrieve(g, l, after):
        return _retrieve_stage(x[g], mods[g][l], l, Ws[l], P, lens[g], after)

    x[0] = mixers[0](x[0], 0)
    h2, idx, gate = retrieve(0, 0, jnp.zeros((8, PEER_TK), jnp.int32))
    q = [(idx, gate, _peer_u(h2, idx, u_packed[0], none_yet)), None]
    for l in range(depth):
        last = l + 1 == depth
        x[1] = mixers[1](x[1], l)
        idx, gate, act = q[0]
        w0 = _peer_gelu(gate, act, tm)
        y0 = _peer_v(w0, idx, v_packed[l], x[1])
        h2, idx, gate = retrieve(1, l, w0)
        q[1] = (idx, gate, _peer_u(h2, idx, u_packed[l], y0))
        x[0] = _peer_residual(x[0], y0, mods[0][l], lens[0], tm)
        if not last:
            x[0] = mixers[0](x[0], l + 1)
        idx, gate, act = q[1]
        w1 = _peer_gelu(gate, act, tm)
        y1 = _peer_v(w1, idx, v_packed[l], none_yet if last else x[0])
        if not last:
            h2, idx, gate = retrieve(0, l + 1, w1)
            q[0] = (idx, gate, _peer_u(h2, idx, u_packed[l + 1], y1))
        x[1] = _peer_residual(x[1], y1, mods[1][l], lens[1], tm)

    y_prompt = _final_norm(x[0], final_norm[None], 256).reshape(bp, lp, D_MODEL)
    y_sample = _final_norm(x[1], final_norm[None], 256).reshape(bs, ls, D_MODEL)
    return (y_prompt, y_sample, jnp.stack(ckvs, axis=1), jnp.stack(kpes, axis=1), jnp.stack(states, axis=1))
```

```python
import functools

import numpy as np
import jax
import jax.numpy as jnp
from jax import lax
from jax.experimental import pallas as pl
from jax.experimental.pallas import tpu as pltpu
from jax.experimental.pallas import tpu_sc as plsc

F32 = jnp.float32
BF16 = jnp.bfloat16
HIGHEST = lax.Precision.HIGHEST

D_MODEL = 1024
EPS = 1e-6
GRID_W = 64
HG_HEADS = 8
HG_D = 64
HG_CHUNK = 32
HG_W = HG_HEADS * HG_D
FORGET_FLOOR = 1e-6
MLA_HEADS = 8
MLA_Q_LORA = 384
MLA_KV_LORA = 256
MLA_NOPE = 64
MLA_ROPE = 32
MLA_V = 64
MLA_SCALE = (MLA_NOPE + MLA_ROPE) ** -0.5
ROPE_AXIS_DIM = MLA_ROPE // 2
ROPE_BASE = 10000.0
PEER_HEADS = 8
PEER_NKEYS = 128
PEER_DQ = 256
PEER_TOPK = 16

LANES = 128
VMEM_LIMIT = 48 * 1024 * 1024

IN_HG = 5 * HG_W
IN_CQ0 = IN_HG
IN_CKV0 = IN_CQ0 + MLA_Q_LORA
IN_KPE0 = IN_CKV0 + MLA_KV_LORA
IN_PAD = IN_KPE0 + LANES


def _cparams(sem):
    return pltpu.CompilerParams(dimension_semantics=sem, vmem_limit_bytes=VMEM_LIMIT)


def _split_bf16(a):
    hi = a.astype(BF16)
    lo = (a - hi.astype(F32)).astype(BF16)
    return hi, lo


def _dot3(a, b, dims=(((1,), (0,)), ((), ()))):
    ah, al = _split_bf16(a)
    bh, bl = _split_bf16(b)
    d = functools.partial(lax.dot_general, dimension_numbers=dims, preferred_element_type=F32)
    return d(ah, bh) + d(al, bh) + d(ah, bl)


def _dot_sel(a, sel_bf):
    ah, al = _split_bf16(a)
    return (jnp.dot(ah, sel_bf, preferred_element_type=F32)
            + jnp.dot(al, sel_bf, preferred_element_type=F32))


def _rms_rows(x, g):
    return x * lax.rsqrt(jnp.mean(x * x, axis=-1, keepdims=True) + EPS) * g


def _ada_kernel(cond_ref, w_ref, b_ref, o_ref):
    cnd = cond_ref[...]
    s = cnd * jax.nn.sigmoid(cnd)
    o_ref[0] = jnp.dot(s, w_ref[0], precision=HIGHEST, preferred_element_type=F32) + b_ref[0]


def _ada_mod(cond8, w_ada, b_ada, l):
    depth, d, n6 = w_ada.shape
    tn = 1536
    return pl.pallas_call(
        _ada_kernel,
        grid=(n6 // tn,),
        in_specs=[pl.BlockSpec((8, d), lambda j: (0, 0)),
                  pl.BlockSpec((1, d, tn), lambda j: (l, 0, j)),
                  pl.BlockSpec((1, 1, tn), lambda j: (l, 0, j))],
        out_specs=pl.BlockSpec((1, 8, tn), lambda j: (0, 0, j)),
        out_shape=jax.ShapeDtypeStruct((1, 8, n6), F32),
        compiler_params=_cparams(("parallel",)),
        name="ada_mod",
    )(cond8, w_ada, b_ada.reshape(depth, 1, n6))[0]


def _inproj_kernel(x_ref, g_ref, mod_ref, w_ref, gq_ref, gkv_ref, hg_ref, cq_ref, ckv_ref, kpe_ref):
    x = x_ref[...]
    shift = mod_ref[0, :, 0:D_MODEL]
    scale = mod_ref[0, :, D_MODEL:2 * D_MODEL]
    h = (_rms_rows(x, g_ref[...]) * (1.0 + scale) + shift).astype(BF16)
    hg_ref[...] = jnp.dot(h, w_ref[:, 0:IN_HG], preferred_element_type=F32)
    cq = jnp.dot(h, w_ref[:, IN_CQ0:IN_CKV0], preferred_element_type=F32)
    cq_ref[...] = _rms_rows(cq, gq_ref[...]).astype(BF16)
    ckv = jnp.dot(h, w_ref[:, IN_CKV0:IN_KPE0], preferred_element_type=F32)
    ckv_ref[...] = _rms_rows(ckv, gkv_ref[...])
    kpe_ref[...] = jnp.dot(h, w_ref[:, IN_KPE0:IN_PAD], preferred_element_type=F32)


def _inproj(x, g, mod, w_pad, gq, gkv, seq_len, tm):
    n = x.shape[0]
    per_seq = seq_len // tm
    n_cond = mod.shape[0]
    cond_of = (lambda i: (0, 0, 0)) if n_cond == 1 else (lambda i: (i // per_seq, 0, 0))
    row = lambda i: (i, 0)
    const = lambda i: (0, 0)
    return pl.pallas_call(
        _inproj_kernel,
        grid=(n // tm,),
        in_specs=[pl.BlockSpec((tm, D_MODEL), row),
                  pl.BlockSpec((1, D_MODEL), const),
                  pl.BlockSpec((1, 1, 6 * D_MODEL), cond_of),
                  pl.BlockSpec((D_MODEL, IN_PAD), const),
                  pl.BlockSpec((1, MLA_Q_LORA), const),
                  pl.BlockSpec((1, MLA_KV_LORA), const)],
        out_specs=[pl.BlockSpec((tm, IN_HG), row),
                   pl.BlockSpec((tm, MLA_Q_LORA), row),
                   pl.BlockSpec((tm, MLA_KV_LORA), row),
                   pl.BlockSpec((tm, LANES), row)],
        out_shape=[jax.ShapeDtypeStruct((n, IN_HG), F32),
                   jax.ShapeDtypeStruct((n, MLA_Q_LORA), BF16),
                   jax.ShapeDtypeStruct((n, MLA_KV_LORA), F32),
                   jax.ShapeDtypeStruct((n, LANES), F32)],
        compiler_params=_cparams(("parallel",)),
        name="in_proj",
    )(x, g, mod, w_pad, gq, gkv)


HG_SUB = 8
HG_UNROLL = 4


def _hgrn_kernel(q_ref, ff_ref, fb_ref, v_ref, gt_ref, lbf_ref, lbb_ref, gn_ref, s0_ref,
                 out_ref, sfin_ref, qt_ref, acc_ref, u_ref, d_ref, *, seq_len):
    C, SB = HG_CHUNK, HG_SUB
    n_sub = C // SB
    n_chunks = seq_len // C
    r2 = lax.broadcasted_iota(jnp.int32, (LANES, LANES), 0) // HG_D
    c2 = lax.broadcasted_iota(jnp.int32, (LANES, LANES), 1) // HG_D
    same_head = r2 == c2
    sel_bf = same_head.astype(BF16)
    ti = lax.broadcasted_iota(jnp.int32, (C, C), 0)
    ui = lax.broadcasted_iota(jnp.int32, (C, C), 1)
    tri = ((ui <= ti).astype(F32), (ui >= ti).astype(F32))
    s3 = lax.broadcasted_iota(jnp.int32, (SB, SB, 1), 0)
    t3 = lax.broadcasted_iota(jnp.int32, (SB, SB, 1), 1)
    diag_mask = (t3 >= s3, t3 <= s3)
    key_row = lax.broadcasted_iota(jnp.int32, (C, 1), 0)
    lane1 = lax.broadcasted_iota(jnp.int32, (1, LANES), 1)
    head_lane = ((lane1 < HG_D).astype(F32), (lane1 >= HG_D).astype(F32))
    ones_c = jnp.ones((C, LANES), F32)
    tdims = (((0,), (0,)), ((), ()))
    nt = (((1,), (1,)), ((), ()))
    z_refs, lb_refs = (ff_ref, fb_ref), (lbf_ref, lbb_ref)

    off_blocks = (tuple(range(1, n_sub)), tuple(range(0, n_sub - 1)))
    n_off = n_sub - 1
    sr = lax.broadcasted_iota(jnp.int32, (n_off * 2 * SB, n_off * C), 0) // (2 * SB)
    sc = lax.broadcasted_iota(jnp.int32, (n_off * 2 * SB, n_off * C), 1) // C
    own_block = sr == sc

    units = [(ci, d) for ci in range(HG_UNROLL) for d in range(2)]

    def prep_body(it, carry):
        cs = [it * HG_UNROLL + ci for ci in range(HG_UNROLL)]
        rows = [pl.ds(pl.multiple_of(c * C, C), C) for c in cs]
        q = [q_ref[r, :] * (HG_D ** -0.5) for r in rows]
        v = [v_ref[r, :] for r in rows]
        v_bf = [x.astype(BF16) for x in v]
        v_rep = [jnp.concatenate([x] * n_off, axis=0) for x in v_bf]
        gs, ks = [], []
        for ci, d in units:
            lb = lb_refs[d][...]
            f = jnp.maximum(lb + (1.0 - lb) * jax.nn.sigmoid(z_refs[d][rows[ci], :]), FORGET_FLOOR)
            gs.append(jnp.log(f))
            ks.append(1.0 - f)
        bs = [jnp.dot(tri[d], gs[u], precision=HIGHEST, preferred_element_type=F32) for u, (ci, d) in enumerate(units)]
        totals = [lax.dot_general(g, ones_c, tdims, precision=HIGHEST, preferred_element_type=F32) for g in gs]
        incs, sums, scores = [], [], []
        for u, (ci, d) in enumerate(units):
            b_end = bs[u][C - 1:C, :] if d == 0 else bs[u][0:1, :]
            kk = (ks[u] * jnp.exp(b_end - bs[u])).astype(BF16)
            incs.append(lax.dot_general(kk, v_bf[ci], tdims, preferred_element_type=F32))
        for u, (ci, d) in enumerate(units):
            blocks = []
            for blk in range(n_sub):
                sl = slice(blk * SB, (blk + 1) * SB)
                bi = bs[u][sl]
                diff = bi[None, :, :] - bi[:, None, :]
                e = jnp.where(diag_mask[d], jnp.exp(jnp.where(diag_mask[d], diff, 0.0)), 0.0)
                blocks.append((e * q[ci][sl][None, :, :] * ks[u][sl][:, None, :]).reshape(SB * SB, LANES))
            sums.append(jnp.dot(jnp.concatenate(blocks, axis=0).astype(BF16), sel_bf, preferred_element_type=F32))
        for u, (ci, d) in enumerate(units):
            lhs, khs = [], []
            for blk in off_blocks[d]:
                sl = slice(blk * SB, (blk + 1) * SB)
                if d == 0:
                    r = bs[u][blk * SB - 1:blk * SB, :]
                    keys = key_row < blk * SB
                else:
                    r = bs[u][(blk + 1) * SB:(blk + 1) * SB + 1, :]
                    keys = key_row >= (blk + 1) * SB
                qh = q[ci][sl] * jnp.exp(bs[u][sl] - r)
                lhs += [qh * head_lane[0], qh * head_lane[1]]
                khs.append(jnp.where(keys, ks[u] * jnp.exp(jnp.minimum(r - bs[u], 0.0)), 0.0))
            scores.append(lax.dot_general(jnp.concatenate(lhs, axis=0).astype(BF16),
                                          jnp.concatenate(khs, axis=0).astype(BF16), nt,
                                          preferred_element_type=F32))
        o2s = [jnp.dot(jnp.where(own_block, scores[u], 0.0).astype(BF16), v_rep[ci], preferred_element_type=F32)
               for u, (ci, d) in enumerate(units)]
        for u, (ci, d) in enumerate(units):
            outs = []
            for blk in range(n_sub):
                sl = slice(blk * SB, (blk + 1) * SB)
                a_blk = sums[u][blk * SB * SB:(blk + 1) * SB * SB].reshape(SB, SB, LANES)
                o_blk = jnp.sum(a_blk * v[ci][sl][:, None, :], axis=0)
                if blk in off_blocks[d]:
                    j = off_blocks[d].index(blk) * 2 * SB
                    o_blk = o_blk + o2s[u][j:j + SB] * head_lane[0] + o2s[u][j + SB:j + 2 * SB] * head_lane[1]
                outs.append(o_blk)
            qt_ref[d, rows[ci], :] = (q[ci] * jnp.exp(bs[u])).astype(BF16)
            u_ref[d, cs[ci]] = jnp.where(same_head, incs[u], 0.0)
            d_ref[d, cs[ci]] = jnp.exp(totals[u])
            acc_ref[d, rows[ci], :] = jnp.concatenate(outs, axis=0)
        return carry

    lax.fori_loop(0, n_chunks // HG_UNROLL, prep_body, 0)

    def block_diag(s_a, s_b):
        z = jnp.zeros((HG_D, HG_D), F32)
        return jnp.concatenate([jnp.concatenate([s_a, z], axis=1),
                                jnp.concatenate([z, s_b], axis=1)], axis=0)

    def scan_body(i, carry):
        s_f, s_b = carry
        cb = n_chunks - 1 - i
        inc_f, inc_b = u_ref[0, i], u_ref[1, cb]
        u_ref[0, i] = s_f
        u_ref[1, cb] = s_b
        return d_ref[0, i] * s_f + inc_f, d_ref[1, cb] * s_b + inc_b

    s_f, s_b = lax.fori_loop(0, n_chunks, scan_body,
                             (block_diag(s0_ref[0, 0, 0], s0_ref[0, 0, 1]),
                              block_diag(s0_ref[0, 1, 0], s0_ref[0, 1, 1])))
    sfin_ref[0, 0, 0] = s_f[0:HG_D, 0:HG_D]
    sfin_ref[0, 0, 1] = s_f[HG_D:, HG_D:]
    sfin_ref[0, 1, 0] = s_b[0:HG_D, 0:HG_D]
    sfin_ref[0, 1, 1] = s_b[HG_D:, HG_D:]

    bdims = (((2,), (1,)), ((0,), (0,)))
    o = acc_ref[0] + acc_ref[1]
    for d in range(2):
        inter = lax.dot_general(qt_ref[d].reshape(n_chunks, C, LANES), u_ref[d].astype(BF16), bdims,
                                preferred_element_type=F32)
        o = o + inter.reshape(seq_len, LANES)
    ms = _dot_sel(o * o, sel_bf) * (1.0 / HG_D)
    gt = gt_ref[...]
    out_ref[...] = (o * lax.rsqrt(ms + EPS) * gn_ref[...] * (gt * jax.nn.sigmoid(gt))).astype(BF16)


def _hgrn(hg, lb_f, lb_b, gn2, s0, batch, seq_len):
    n = hg.shape[0]
    pairs = HG_W // LANES
    part = lambda p: pl.BlockSpec((seq_len, LANES), lambda b, j, p=p: (b, p * pairs + j))
    lane_row = pl.BlockSpec((1, LANES), lambda b, j: (0, j))
    st = pl.BlockSpec((1, 2, 2, HG_D, HG_D), lambda b, j: (b, 0, j, 0, 0))
    return pl.pallas_call(
        functools.partial(_hgrn_kernel, seq_len=seq_len),
        grid=(batch, pairs),
        in_specs=[part(0), part(1), part(2), part(3), part(4), lane_row, lane_row,
                  pl.BlockSpec((1, LANES), lambda b, j: (0, 0)), st],
        out_specs=[pl.BlockSpec((seq_len, LANES), lambda b, j: (b, j)), st],
        out_shape=[jax.ShapeDtypeStruct((n, HG_W), BF16),
                   jax.ShapeDtypeStruct((batch, 2, HG_HEADS, HG_D, HG_D), F32)],
        scratch_shapes=[pltpu.VMEM((2, seq_len, LANES), BF16),
                        pltpu.VMEM((2, seq_len, LANES), F32),
                        pltpu.VMEM((2, seq_len // HG_CHUNK, LANES, LANES), F32),
                        pltpu.VMEM((2, seq_len // HG_CHUNK, LANES, LANES), F32)],
        compiler_params=_cparams(("parallel", "parallel")),
        name="hgrn2",
    )(hg, hg, hg, hg, hg, lb_f, lb_b, gn2, s0)


def _rope_lanes(x, cos, sin):
    return x * cos + pltpu.roll(x, 96, 1) * sin


def _qproj_kernel(cq_ref, w_ref, cos_ref, sin_ref, q_ref):
    cq = cq_ref[...]
    for h in range(MLA_HEADS):
        qh = jnp.dot(cq, w_ref[h], preferred_element_type=F32)
        q_ref[h] = (_rope_lanes(qh, cos_ref[...], sin_ref[...]) * MLA_SCALE).astype(BF16)


def _kvproj_kernel(ckv_ref, kpe_ref, w_ref, cos_ref, sin_ref, k_ref, kv_ref):
    ckv = ckv_ref[...].astype(BF16)
    kpe = _rope_lanes(kpe_ref[...], cos_ref[...], sin_ref[...])
    nope = lax.broadcasted_iota(jnp.int32, kpe.shape, 1) < MLA_NOPE
    for h in range(MLA_HEADS):
        kvh = jnp.dot(ckv, w_ref[h], preferred_element_type=F32)
        kv_ref[h] = kvh.astype(BF16)
        k_ref[h] = jnp.where(nope, kvh, kpe).astype(BF16)


def _qproj(cqn, wq, cos, sin, tm):
    n = cqn.shape[0]
    pos_blocks = cos.shape[0] // tm
    return pl.pallas_call(
        _qproj_kernel,
        grid=(n // tm,),
        in_specs=[pl.BlockSpec((tm, MLA_Q_LORA), lambda i: (i, 0)),
                  pl.BlockSpec((MLA_HEADS, MLA_Q_LORA, LANES), lambda i: (0, 0, 0)),
                  pl.BlockSpec((tm, LANES), lambda i: (i % pos_blocks, 0)),
                  pl.BlockSpec((tm, LANES), lambda i: (i % pos_blocks, 0))],
        out_specs=pl.BlockSpec((MLA_HEADS, tm, LANES), lambda i: (0, i, 0)),
        out_shape=jax.ShapeDtypeStruct((MLA_HEADS, n, LANES), BF16),
        compiler_params=_cparams(("parallel",)),
        name="mla_q_proj",
    )(cqn, wq, cos, sin)


def _kvproj(ckvn, kpe128, wkv, cos, sin, tm):
    n = ckvn.shape[0]
    pos_blocks = cos.shape[0] // tm
    hm = pl.BlockSpec((MLA_HEADS, tm, LANES), lambda i: (0, i, 0))
    return pl.pallas_call(
        _kvproj_kernel,
        grid=(n // tm,),
        in_specs=[pl.BlockSpec((tm, MLA_KV_LORA), lambda i: (i, 0)),
                  pl.BlockSpec((tm, LANES), lambda i: (i, 0)),
                  pl.BlockSpec((MLA_HEADS, MLA_KV_LORA, LANES), lambda i: (0, 0, 0)),
                  pl.BlockSpec((tm, LANES), lambda i: (i % pos_blocks, 0)),
                  pl.BlockSpec((tm, LANES), lambda i: (i % pos_blocks, 0))],
        out_specs=[hm, hm],
        out_shape=[jax.ShapeDtypeStruct((MLA_HEADS, n, LANES), BF16)] * 2,
        compiler_params=_cparams(("parallel",)),
        name="mla_kv_proj",
    )(ckvn, kpe128, wkv, cos, sin)


def _attn_kernel(q_ref, k_ref, kv_ref, o_ref):
    low = lax.broadcasted_iota(jnp.int32, (q_ref.shape[1], LANES), 1) < MLA_V
    nt = (((1,), (1,)), ((), ()))
    outs = []
    for h in range(MLA_HEADS):
        s = lax.dot_general(q_ref[h], k_ref[h], nt, preferred_element_type=F32)
        p = jnp.exp(s - jnp.max(s, axis=-1, keepdims=True))
        denom = jnp.sum(p, axis=-1, keepdims=True)
        o = jnp.dot(p.astype(BF16), kv_ref[h], preferred_element_type=F32) / denom
        outs.append(o)
    for j in range(MLA_HEADS // 2):
        pair = jnp.where(low, pltpu.roll(outs[2 * j], MLA_V, 1), outs[2 * j + 1])
        o_ref[:, j * LANES:(j + 1) * LANES] = pair.astype(BF16)


def _attention(q128, k128, kv128, batch, lq, lk, tq):
    n = q128.shape[1]
    qb = lq // tq
    kspec = pl.BlockSpec((MLA_HEADS, lk, LANES), lambda b, i: (0, b, 0))
    return pl.pallas_call(
        _attn_kernel,
        grid=(batch, qb),
        in_specs=[pl.BlockSpec((MLA_HEADS, tq, LANES), lambda b, i: (0, b * qb + i, 0)), kspec, kspec],
        out_specs=pl.BlockSpec((tq, MLA_HEADS * MLA_V), lambda b, i: (b * qb + i, 0)),
        out_shape=jax.ShapeDtypeStruct((n, MLA_HEADS * MLA_V), BF16),
        compiler_params=_cparams(("parallel", "parallel")),
        name="mla_attention",
    )(q128, k128, kv128)


def _outproj_kernel(x_ref, hg_ref, at_ref, w_ref, mod_ref, o_ref):
    gate = mod_ref[0, :, 2 * D_MODEL:3 * D_MODEL]
    mix = (jnp.dot(hg_ref[...], w_ref[0:HG_W, :], preferred_element_type=F32)
           + jnp.dot(at_ref[...], w_ref[HG_W:, :], preferred_element_type=F32))
    o_ref[...] = x_ref[...] + gate * mix


def _outproj(x, hg_out, attn, w_out, mod, seq_len, tm):
    n = x.shape[0]
    per_seq = seq_len // tm
    n_cond = mod.shape[0]
    cond_of = (lambda i: (0, 0, 0)) if n_cond == 1 else (lambda i: (i // per_seq, 0, 0))
    row = lambda i: (i, 0)
    return pl.pallas_call(
        _outproj_kernel,
        grid=(n // tm,),
        in_specs=[pl.BlockSpec((tm, D_MODEL), row),
                  pl.BlockSpec((tm, HG_W), row),
                  pl.BlockSpec((tm, MLA_HEADS * MLA_V), row),
                  pl.BlockSpec((2 * HG_W, D_MODEL), lambda i: (0, 0)),
                  pl.BlockSpec((1, 1, 6 * D_MODEL), cond_of)],
        out_specs=pl.BlockSpec((tm, D_MODEL), row),
        out_shape=jax.ShapeDtypeStruct((n, D_MODEL), F32),
        compiler_params=_cparams(("parallel",)),
        name="out_proj",
    )(x, hg_out, attn, w_out, mod)


_CAND = [(a, b) for a in range(PEER_TOPK) for b in range(PEER_TOPK) if (a + 1) * (b + 1) <= PEER_TOPK]
_CAND_PAD = -(-len(_CAND) // 8) * 8


def _topk_rows(work, k):
    rows = lax.broadcasted_iota(jnp.int32, work.shape, 0).astype(F32)
    n_rows = float(work.shape[0])
    vals, idxs = [], []
    for _ in range(k):
        m = jnp.max(work, axis=0, keepdims=True)
        sel = jnp.min(jnp.where(work == m, rows, n_rows), axis=0, keepdims=True)
        vals.append(m)
        idxs.append(sel)
        work = jnp.where(rows == sel, -jnp.inf, work)
    return jnp.concatenate(vals, axis=0), jnp.concatenate(idxs, axis=0)


def _peerq_kernel(x_ref, g_ref, mod_ref, w_ref, k1_ref, k2_ref, after_ref, h2_ref, idx_ref, gate_ref):
    del after_ref
    idx_rows, gate_rows = [], []
    x = x_ref[...]
    shift = mod_ref[0, :, 3 * D_MODEL:4 * D_MODEL]
    scale = mod_ref[0, :, 4 * D_MODEL:5 * D_MODEL]
    h2 = _rms_rows(x, g_ref[...]) * (1.0 + scale) + shift
    h2b = h2.astype(BF16)
    bits = lax.bitcast_convert_type(h2b.astype(F32), jnp.uint32)
    h2_ref[...] = (bits[:, :PACK_W] >> 16) | bits[:, PACK_W:]
    qp = jnp.dot(h2b, w_ref[...], preferred_element_type=F32)
    nt = (((1,), (1,)), ((), ()))
    half = PEER_DQ // 2
    for h in range(PEER_HEADS):
        q1 = qp[:, h * PEER_DQ:h * PEER_DQ + half]
        q2 = qp[:, h * PEER_DQ + half:(h + 1) * PEER_DQ]
        s1 = _dot3(k1_ref[h], q1, nt)
        s2 = _dot3(k2_ref[h], q2, nt)
        v1, i1 = _topk_rows(s1, PEER_TOPK)
        v2, i2 = _topk_rows(s2, PEER_TOPK)
        cs = [v1[a:a + 1, :] + v2[b:b + 1, :] for a, b in _CAND]
        ci = [i1[a:a + 1, :] * PEER_NKEYS + i2[b:b + 1, :] for a, b in _CAND]
        n_pad = _CAND_PAD - len(_CAND)
        if n_pad:
            cs.append(jnp.full((n_pad, x.shape[0]), -jnp.inf, F32))
            ci.append(jnp.zeros((n_pad, x.shape[0]), F32))
        cand_s = jnp.concatenate(cs, axis=0)
        cand_i = jnp.concatenate(ci, axis=0)
        best_s, best_pos = _topk_rows(cand_s, PEER_TOPK)
        rows = lax.broadcasted_iota(jnp.int32, cand_s.shape, 0).astype(F32)
        picked = [jnp.sum(jnp.where(rows == best_pos[r:r + 1, :], cand_i, 0.0), axis=0, keepdims=True)
                  for r in range(PEER_TOPK)]
        idx_rows.extend(picked)
        ex = jnp.exp(best_s - best_s[0:1, :])
        gate_rows.append(ex / jnp.sum(ex, axis=0, keepdims=True))
    idx_ref[...] = jnp.concatenate(idx_rows, axis=0).T.astype(jnp.int32)
    gate_ref[...] = jnp.concatenate(gate_rows, axis=0).T


def _peerq(x, g, mod, wq, k1, k2, seq_len, tm, after):
    n = x.shape[0]
    per_seq = seq_len // tm
    n_cond = mod.shape[0]
    cond_of = (lambda i: (0, 0, 0)) if n_cond == 1 else (lambda i: (i // per_seq, 0, 0))
    kspec = pl.BlockSpec((PEER_HEADS, PEER_NKEYS, PEER_DQ // 2), lambda i: (0, 0, 0))
    tk = PEER_HEADS * PEER_TOPK
    return pl.pallas_call(
        _peerq_kernel,
        grid=(n // tm,),
        in_specs=[pl.BlockSpec((tm, D_MODEL), lambda i: (i, 0)),
                  pl.BlockSpec((1, D_MODEL), lambda i: (0, 0)),
                  pl.BlockSpec((1, 1, 6 * D_MODEL), cond_of),
                  pl.BlockSpec((D_MODEL, PEER_HEADS * PEER_DQ), lambda i: (0, 0)),
                  kspec, kspec,
                  pl.BlockSpec(memory_space=pl.ANY)],
        out_specs=[pl.BlockSpec((tm, D_MODEL // 2), lambda i: (i, 0)),
                   pl.BlockSpec((tm, tk), lambda i: (i, 0)),
                   pl.BlockSpec((tm, tk), lambda i: (i, 0))],
        out_shape=[jax.ShapeDtypeStruct((n, D_MODEL // 2), jnp.uint32),
                   jax.ShapeDtypeStruct((n, tk), jnp.int32),
                   jax.ShapeDtypeStruct((n, tk), F32)],
        compiler_params=_cparams(("parallel",)),
        name="peer_retrieve",
    )(x, g, mod, wq, k1, k2, after)


SC_CORES = 2
SC_SUBCORES = 16
SC_LANES = 16
SC_TILES = SC_CORES * SC_SUBCORES
SC_NBUF_U = 4
SC_NBUF_V = 4
ITEM_HEADS = 2
ITEM_ROWS = ITEM_HEADS * PEER_TOPK
ITEMS_PER_TOKEN = PEER_HEADS // ITEM_HEADS
PEER_TK = PEER_HEADS * PEER_TOPK
D_VREGS = D_MODEL // SC_LANES
BF_RUN = 4


PACK_W = D_MODEL // 2


def _pack_kernel(t_ref, o_ref):
    bits = lax.bitcast_convert_type(t_ref[...].astype(BF16).astype(F32), jnp.uint32)
    o_ref[...] = (bits[:, :PACK_W] >> 16) | bits[:, PACK_W:]


def _pack_bf16_pairs(tables, l, tm=1024):
    n = tables.shape[1]
    return pl.pallas_call(
        _pack_kernel,
        grid=(n // tm,),
        in_specs=[pl.BlockSpec((None, tm, D_MODEL), lambda i: (l, i, 0))],
        out_specs=pl.BlockSpec((tm, PACK_W), lambda i: (i, 0)),
        out_shape=jax.ShapeDtypeStruct((n, PACK_W), jnp.uint32),
        compiler_params=_cparams(("parallel",)),
        name="pack_expert_table",
    )(tables)


def _unpack_pair(word):
    lo = lax.bitcast_convert_type(word << 16, F32)
    hi = lax.bitcast_convert_type(word & jnp.uint32(0xFFFF0000), F32)
    return lo, hi


def _sc_mesh():
    return plsc.VectorSubcoreMesh(core_axis_name="c", subcore_axis_name="s")


def _sc_params():
    return pltpu.CompilerParams(needs_layout_passes=False)


def _sc_token_base(tokens_per_tile):
    return (lax.axis_index("s") * SC_CORES + lax.axis_index("c")) * tokens_per_tile


def _peer_u_body(h2_hbm, idx_hbm, u_hbm, after_hbm, act_hbm, idx_v, act_v, x_v, rows_v, sem_x, sem_g, *, tpt,
                 nbuf):
    del after_hbm
    base = _sc_token_base(tpt)
    n_items = tpt * ITEMS_PER_TOKEN
    pltpu.sync_copy(idx_hbm.at[pl.ds(base, tpt)], idx_v)

    def gather(i, slot):
        t, g = i // ITEMS_PER_TOKEN, i % ITEMS_PER_TOKEN
        return pltpu.make_async_copy(u_hbm.at[idx_v.at[t, pl.ds(g * ITEM_ROWS, ITEM_ROWS)]],
                                     rows_v.at[slot], sem_g.at[slot])

    def x_copy(t):
        return pltpu.make_async_copy(h2_hbm.at[base + t], x_v.at[t % 2], sem_x.at[t % 2])

    x_copy(0).start()
    for b in range(nbuf - 1):
        gather(b, b).start()
    lane = lax.iota(jnp.int32, SC_LANES)
    zero = jnp.zeros((SC_LANES,), F32)

    @pl.loop(0, n_items)
    def _(i):
        for b in (i % nbuf,):
            t, g = i // ITEMS_PER_TOKEN, i % ITEMS_PER_TOKEN

            @pl.when(g == 0)
            def _():
                x_copy(t).wait()

                @pl.when(t + 1 < tpt)
                def _():
                    x_copy(t + 1).start()

            @pl.when(i + nbuf - 1 < n_items)
            def _():
                gather(i + nbuf - 1, (b + nbuf - 1) % nbuf).start()

            gather(i, b).wait()
            xs = t % 2

            @pl.loop(0, ITEM_HEADS)
            def _(sub):
                r0 = sub * PEER_TOPK

                def cbody(cg, accs):
                    parts = [None] * PEER_TOPK
                    for cc in range(BF_RUN):
                        col = pl.ds((cg * BF_RUN + cc) * SC_LANES, SC_LANES)
                        xb = plsc.bitcast(x_v[xs, col], BF16)
                        for j in range(PEER_TOPK):
                            p = xb * plsc.bitcast(rows_v[b, r0 + j, col], BF16)
                            parts[j] = p if cc == 0 else parts[j] + p
                    out = []
                    for j, a in enumerate(accs):
                        lo, hi = _unpack_pair(plsc.bitcast(parts[j], jnp.uint32))
                        out.append(a + lo + hi)
                    return tuple(out)

                accs = plsc.parallel_loop(0, PACK_W // SC_LANES // BF_RUN, carry=(zero,) * PEER_TOPK)(cbody)
                vec = zero
                for j in range(PEER_TOPK):
                    vec = jnp.where(lane == j, jnp.sum(accs[j]), vec)
                act_v[t, pl.ds(g * ITEM_ROWS + r0, PEER_TOPK)] = vec

    pltpu.sync_copy(act_v, act_hbm.at[pl.ds(base, tpt)])


def _peer_v_body(w_hbm, idx_hbm, v_hbm, after_hbm, y_hbm, idx_v, w_v, y_v, rows_v, sem_y, sem_g, *, tpt,
                 nbuf):
    del after_hbm
    base = _sc_token_base(tpt)
    n_items = tpt * ITEMS_PER_TOKEN
    pltpu.sync_copy(idx_hbm.at[pl.ds(base, tpt)], idx_v)
    pltpu.sync_copy(w_hbm.at[pl.ds(base, tpt)], w_v)

    def gather(i, slot):
        t, g = i // ITEMS_PER_TOKEN, i % ITEMS_PER_TOKEN
        return pltpu.make_async_copy(v_hbm.at[idx_v.at[t, pl.ds(g * ITEM_ROWS, ITEM_ROWS)]],
                                     rows_v.at[slot], sem_g.at[slot])

    def y_copy(t):
        return pltpu.make_async_copy(y_v.at[t % 2], y_hbm.at[base + t], sem_y.at[t % 2])

    for b in range(nbuf - 1):
        gather(b, b).start()
    lane = lax.iota(jnp.int32, SC_LANES)
    zero = jnp.zeros((SC_LANES,), F32)

    @pl.loop(0, n_items)
    def _(i):
        for b in (i % nbuf,):
            t, g = i // ITEMS_PER_TOKEN, i % ITEMS_PER_TOKEN
            ys = t % 2

            @pl.when(i + nbuf - 1 < n_items)
            def _():
                gather(i + nbuf - 1, (b + nbuf - 1) % nbuf).start()

            @pl.when(g == 0)
            def _():
                @pl.when(t >= 2)
                def _():
                    y_copy(t - 2).wait()

                @plsc.parallel_loop(0, D_VREGS, unroll=8)
                def _(c):
                    y_v[ys, pl.ds(c * SC_LANES, SC_LANES)] = zero

            gather(i, b).wait()
            for sub in range(ITEM_HEADS):
                r0 = sub * PEER_TOPK
                wv = w_v[t, pl.ds(g * ITEM_ROWS + r0, PEER_TOPK)]
                ws = [plsc.bitcast(jnp.broadcast_to(jnp.sum(jnp.where(lane == j, wv, 0)), (SC_LANES,)), BF16)
                      for j in range(PEER_TOPK)]

                @plsc.parallel_loop(0, PACK_W // SC_LANES)
                def _(c, r0=r0, ws=ws):
                    col = pl.ds(c * SC_LANES, SC_LANES)
                    col_hi = pl.ds(PACK_W + c * SC_LANES, SC_LANES)
                    terms = [ws[j] * plsc.bitcast(rows_v[b, r0 + j, col], BF16) for j in range(PEER_TOPK)]
                    while len(terms) > 1:
                        terms = [terms[k] + terms[k + 1] for k in range(0, len(terms), 2)]
                    lo, hi = _unpack_pair(plsc.bitcast(terms[0], jnp.uint32))
                    y_v[ys, col] = y_v[ys, col] + lo
                    y_v[ys, col_hi] = y_v[ys, col_hi] + hi

            @pl.when(g == ITEMS_PER_TOKEN - 1)
            def _():
                y_copy(t).start()

    y_copy(tpt - 2).wait()
    y_copy(tpt - 1).wait()


def _sc_scratch(tpt, side_dtype, token_buf, nbuf):
    return [pltpu.VMEM((tpt, PEER_TK), jnp.int32),
            pltpu.VMEM((tpt, PEER_TK), side_dtype),
            token_buf,
            pltpu.VMEM((nbuf, ITEM_ROWS, PACK_W), jnp.uint32),
            pltpu.SemaphoreType.DMA((2,)),
            pltpu.SemaphoreType.DMA((nbuf,))]


def _peer_u(h2, idx, u_all, after):
    n = h2.shape[0]
    tpt = n // SC_TILES
    return pl.kernel(
        functools.partial(_peer_u_body, tpt=tpt, nbuf=SC_NBUF_U),
        out_type=jax.ShapeDtypeStruct((n, PEER_TK), F32),
        mesh=_sc_mesh(),
        scratch_types=_sc_scratch(tpt, F32, pltpu.VMEM((2, PACK_W), jnp.uint32), SC_NBUF_U),
        compiler_params=_sc_params(),
        name="peer_u_sc",
    )(h2, idx, u_all, after)


def _peer_v(w, idx, v_all, after):
    n = w.shape[0]
    tpt = n // SC_TILES
    return pl.kernel(
        functools.partial(_peer_v_body, tpt=tpt, nbuf=SC_NBUF_V),
        out_type=jax.ShapeDtypeStruct((n, D_MODEL), F32),
        mesh=_sc_mesh(),
        scratch_types=_sc_scratch(tpt, jnp.int32, pltpu.VMEM((2, D_MODEL), F32), SC_NBUF_V),
        compiler_params=_sc_params(),
        name="peer_v_sc",
    )(w, idx, v_all, after)


def _peer_gelu_kernel(gate_ref, act_ref, w_ref):
    act = act_ref[...]
    w = gate_ref[...] * (0.5 * act * (1.0 + lax.erf(act * (2.0 ** -0.5))))
    bits = lax.bitcast_convert_type(w.astype(BF16).astype(F32), jnp.uint32)
    w_ref[...] = lax.bitcast_convert_type(bits | (bits >> 16), jnp.int32)


def _peer_gelu(gate, act, tm):
    n = gate.shape[0]
    spec = pl.BlockSpec((tm, PEER_TK), lambda i: (i, 0))
    return pl.pallas_call(
        _peer_gelu_kernel,
        grid=(n // tm,),
        in_specs=[spec, spec],
        out_specs=spec,
        out_shape=jax.ShapeDtypeStruct((n, PEER_TK), jnp.int32),
        compiler_params=_cparams(("parallel",)),
        name="peer_gelu",
    )(gate, act)


def _peer_residual_kernel(x_ref, y_ref, mod_ref, o_ref):
    o_ref[...] = x_ref[...] + mod_ref[0, :, 5 * D_MODEL:6 * D_MODEL] * y_ref[...]


def _peer_residual(x, y, mod, seq_len, tm):
    n = x.shape[0]
    per_seq = seq_len // tm
    n_cond = mod.shape[0]
    cond_of = (lambda i: (0, 0, 0)) if n_cond == 1 else (lambda i: (i // per_seq, 0, 0))
    row = pl.BlockSpec((tm, D_MODEL), lambda i: (i, 0))
    return pl.pallas_call(
        _peer_residual_kernel,
        grid=(n // tm,),
        in_specs=[row, row, pl.BlockSpec((1, 1, 6 * D_MODEL), cond_of)],
        out_specs=row,
        out_shape=jax.ShapeDtypeStruct((n, D_MODEL), F32),
        compiler_params=_cparams(("parallel",)),
        name="peer_residual",
    )(x, y, mod)


def _final_norm_kernel(x_ref, g_ref, o_ref):
    o_ref[...] = _rms_rows(x_ref[...], g_ref[...])


def _final_norm(x, g, tm):
    n = x.shape[0]
    return pl.pallas_call(
        _final_norm_kernel,
        grid=(n // tm,),
        in_specs=[pl.BlockSpec((tm, D_MODEL), lambda i: (i, 0)), pl.BlockSpec((1, D_MODEL), lambda i: (0, 0))],
        out_specs=pl.BlockSpec((tm, D_MODEL), lambda i: (i, 0)),
        out_shape=jax.ShapeDtypeStruct((n, D_MODEL), F32),
        compiler_params=_cparams(("parallel",)),
        name="final_norm",
    )(x, g)


def _rot_half_columns(w):
    half = ROPE_AXIS_DIM // 2
    w4 = w.reshape(w.shape[:-1] + (2, 2, half))
    return jnp.stack([-w4[..., 1, :], w4[..., 0, :]], axis=-2).reshape(w.shape)


def _rope_tables(seq_len, rotate):
    cos = np.zeros((seq_len, LANES), np.float32)
    sin = np.zeros((seq_len, LANES), np.float32)
    cos[:, :MLA_NOPE + MLA_ROPE] = 1.0
    if rotate:
        pos = np.arange(seq_len)
        half = ROPE_AXIS_DIM // 2
        inv = (1.0 / (ROPE_BASE ** (np.arange(0, ROPE_AXIS_DIM, 2, dtype=np.float32) / ROPE_AXIS_DIM))).astype(np.float32)
        for axis, p in enumerate(((pos // GRID_W).astype(np.float32), (pos % GRID_W).astype(np.float32))):
            ang = p[:, None] * inv[None, :]
            for part in range(2):
                lo = MLA_NOPE + axis * ROPE_AXIS_DIM + part * half
                cos[:, lo:lo + half] = np.cos(ang)
                sin[:, lo:lo + half] = np.sin(ang)
    return jnp.asarray(cos), jnp.asarray(sin)


def _prep_layer_weights(l, w_in, mla_wqb, mla_wkvb, w_out, peer_wq):
    wi = w_in[l]
    kpe_w = wi[:, IN_KPE0:IN_KPE0 + MLA_ROPE]
    w_pad = jnp.concatenate(
        [wi[:, :IN_KPE0], jnp.zeros((D_MODEL, MLA_NOPE), F32), kpe_w, _rot_half_columns(kpe_w)], axis=1).astype(BF16)
    wq = mla_wqb[l].reshape(MLA_Q_LORA, MLA_HEADS, MLA_NOPE + MLA_ROPE)
    wq_pe = wq[..., MLA_NOPE:]
    wq128 = jnp.concatenate([wq, _rot_half_columns(wq_pe)], axis=-1).transpose(1, 0, 2).astype(BF16)
    wkv128 = mla_wkvb[l].reshape(MLA_KV_LORA, MLA_HEADS, MLA_NOPE + MLA_V).transpose(1, 0, 2).astype(BF16)
    return w_pad, wq128, wkv128, w_out[l].astype(BF16), peer_wq[l].astype(BF16)


TOK_TILE = 256


def _mixer_stage(x, mod, l, W, P, lb, batch, seq_len, ctx, rope):
    w_pad, wq128, wkv128, wo, _ = W
    tm = TOK_TILE
    hg, cqn, ckvn, kpe128 = _inproj(x, P['norm_mix'][l][None], mod, w_pad, P['mla_q_norm'][l][None],
                                    P['mla_kv_norm'][l][None], seq_len, tm)
    if ctx is None:
        s0 = jnp.zeros((batch, 2, HG_HEADS, HG_D, HG_D), F32)
    else:
        s0 = ctx[2]
    gn2 = jnp.tile(P['hgrn_norm'][l], 2)[None]
    hg_out, s_fin = _hgrn(hg, lb[0, l][None], lb[1, l][None], gn2, s0, batch, seq_len)

    cos, sin = rope
    q128 = _qproj(cqn, wq128, cos, sin, tm)
    k128, kv128 = _kvproj(ckvn, kpe128, wkv128, cos, sin, tm)
    lk = seq_len
    if ctx is not None:
        past = ctx[0].shape[1]
        cos_c, sin_c = _rope_tables(past, False)
        kpe_c = jnp.pad(ctx[1].reshape(batch * past, MLA_ROPE), ((0, 0), (MLA_NOPE, LANES - MLA_NOPE - MLA_ROPE)))
        k_c, kv_c = _kvproj(ctx[0].reshape(batch * past, MLA_KV_LORA), kpe_c, wkv128, cos_c, sin_c, past)
        join = lambda a, b: jnp.concatenate(
            [a.reshape(MLA_HEADS, batch, past, LANES), b.reshape(MLA_HEADS, batch, seq_len, LANES)],
            axis=2).reshape(MLA_HEADS, batch * (past + seq_len), LANES)
        k128, kv128 = join(k_c, k128), join(kv_c, kv128)
        lk = past + seq_len
    attn = _attention(q128, k128, kv128, batch, seq_len, lk, tm)
    x = _outproj(x, hg_out, attn, wo, mod, seq_len, tm)
    return x, (ckvn, kpe128[:, MLA_NOPE:MLA_NOPE + MLA_ROPE], s_fin)


def _retrieve_stage(x, mod, l, W, P, seq_len, after):
    keys = P['peer_keys'][l]
    return _peerq(x, P['norm_ffn'][l][None], mod, W[4], keys[0], keys[1], seq_len, TOK_TILE, after)


def kernel(x_prompt, x_sample, cache_ckv, cache_kpe, state_hgrn, c, c_ctx, w_ada, b_ada, norm_mix, w_in,
           hgrn_lb_logits, hgrn_norm, mla_q_norm, mla_wqb, mla_kv_norm, mla_wkvb, w_out, norm_ffn,
           peer_wq, peer_keys, peer_u, peer_v, final_norm):
    P = {'norm_mix': norm_mix, 'hgrn_norm': hgrn_norm, 'mla_q_norm': mla_q_norm, 'mla_kv_norm': mla_kv_norm,
         'norm_ffn': norm_ffn, 'peer_keys': peer_keys}
    depth = w_in.shape[0]
    u_packed = [_pack_bf16_pairs(peer_u, l) for l in range(depth)]
    v_packed = [_pack_bf16_pairs(peer_v, l) for l in range(depth)]
    bp, lp, _ = x_prompt.shape
    bs, ls, _ = x_sample.shape

    p = jax.nn.softmax(hgrn_lb_logits.astype(F32), axis=1)
    lb = jnp.cumsum(p, axis=1) - p[:, :1]

    cond8 = jnp.zeros((8, D_MODEL), F32).at[0].set(c_ctx).at[1:1 + bs].set(c)
    mod_all = [_ada_mod(cond8, w_ada, b_ada, l) for l in range(depth)]

    rope_p = _rope_tables(lp, False)
    rope_s = _rope_tables(ls, True)

    ckvs, kpes, states = [], [], []
    Ws = [_prep_layer_weights(l, w_in, mla_wqb, mla_wkvb, w_out, peer_wq) for l in range(depth)]
    mods_p = [mod_all[l][0:1][:, None, :] for l in range(depth)]
    mods_s = [mod_all[l][1:1 + bs][:, None, :] for l in range(depth)]

    def mixer_s(x, l):
        return _mixer_stage(x, mods_s[l], l, Ws[l], P, lb, bs, ls,
                            (cache_ckv[:, l], cache_kpe[:, l], state_hgrn[:, l]), rope_s)[0]

    def mixer_p(x, l):
        x, (ckv_l, kpe_l, st_l) = _mixer_stage(x, mods_p[l], l, Ws[l], P, lb, bp, lp, None, rope_p)
        ckvs.append(ckv_l.reshape(bp, lp, MLA_KV_LORA))
        kpes.append(kpe_l.reshape(bp, lp, MLA_ROPE))
        states.append(st_l)
        return x

    tm = TOK_TILE
    mixers, mods, lens = (mixer_p, mixer_s), (mods_p, mods_s), (lp, ls)
    x = [x_prompt.reshape(bp * lp, D_MODEL), x_sample.reshape(bs * ls, D_MODEL)]
    none_yet = jnp.zeros((8, D_MODEL), F32)

    def retrieve(g, l, after):
        return _retrieve_stage(x[g], mods[g][l], l, Ws[l], P, lens[g], after)

    x[0] = mixers[0](x[0], 0)
    h2, idx, gate = retrieve(0, 0, jnp.zeros((8, PEER_TK), jnp.int32))
    q = [(idx, gate, _peer_u(h2, idx, u_packed[0], none_yet)), None]
    for l in range(depth):
        last = l + 1 == depth
        x[1] = mixers[1](x[1], l)
        idx, gate, act = q[0]
        w0 = _peer_gelu(gate, act, tm)
        y0 = _peer_v(w0, idx, v_packed[l], x[1])
        h2, idx, gate = retrieve(1, l, w0)
        q[1] = (idx, gate, _peer_u(h2, idx, u_packed[l], y0))
        x[0] = _peer_residual(x[0], y0, mods[0][l], lens[0], tm)
        if not last:
            x[0] = mixers[0](x[0], l + 1)
        idx, gate, act = q[1]
        w1 = _peer_gelu(gate, act, tm)
        y1 = _peer_v(w1, idx, v_packed[l], none_yet if last else x[0])
        if not last:
            h2, idx, gate = retrieve(0, l + 1, w1)
            q[0] = (idx, gate, _peer_u(h2, idx, u_packed[l + 1], y1))
        x[1] = _peer_residual(x[1], y1, mods[1][l], lens[1], tm)

    y_prompt = _final_norm(x[0], final_norm[None], tm).reshape(bp, lp, D_MODEL)
    y_sample = _final_norm(x[1], final_norm[None], tm).reshape(bs, ls, D_MODEL)
    return (y_prompt, y_sample, jnp.stack(ckvs, axis=1), jnp.stack(kpes, axis=1), jnp.stack(states, axis=1))
```

```python
import functools

import numpy as np
import jax
import jax.numpy as jnp
from jax import lax
from jax.experimental import pallas as pl
from jax.experimental.pallas import tpu as pltpu
from jax.experimental.pallas import tpu_sc as plsc

F32 = jnp.float32
BF16 = jnp.bfloat16
HIGHEST = lax.Precision.HIGHEST

D_MODEL = 1024
EPS = 1e-6
GRID_W = 64
HG_HEADS = 8
HG_D = 64
HG_CHUNK = 32
HG_W = HG_HEADS * HG_D
FORGET_FLOOR = 1e-6
MLA_HEADS = 8
MLA_Q_LORA = 384
MLA_KV_LORA = 256
MLA_NOPE = 64
MLA_ROPE = 32
MLA_V = 64
MLA_SCALE = (MLA_NOPE + MLA_ROPE) ** -0.5
ROPE_AXIS_DIM = MLA_ROPE // 2
ROPE_BASE = 10000.0
PEER_HEADS = 8
PEER_NKEYS = 128
PEER_DQ = 256
PEER_TOPK = 16

LANES = 128
VMEM_LIMIT = 48 * 1024 * 1024

IN_HG = 5 * HG_W
IN_CQ0 = IN_HG
IN_CKV0 = IN_CQ0 + MLA_Q_LORA
IN_KPE0 = IN_CKV0 + MLA_KV_LORA
IN_PAD = IN_KPE0 + LANES


def _cparams(sem):
    return pltpu.CompilerParams(dimension_semantics=sem, vmem_limit_bytes=VMEM_LIMIT)


def _split_bf16(a):
    hi = a.astype(BF16)
    lo = (a - hi.astype(F32)).astype(BF16)
    return hi, lo


def _dot3(a, b, dims=(((1,), (0,)), ((), ()))):
    ah, al = _split_bf16(a)
    bh, bl = _split_bf16(b)
    d = functools.partial(lax.dot_general, dimension_numbers=dims, preferred_element_type=F32)
    return d(ah, bh) + d(al, bh) + d(ah, bl)


def _dot_sel(a, sel_bf):
    ah, al = _split_bf16(a)
    return (jnp.dot(ah, sel_bf, preferred_element_type=F32)
            + jnp.dot(al, sel_bf, preferred_element_type=F32))


def _rms_rows(x, g):
    return x * lax.rsqrt(jnp.mean(x * x, axis=-1, keepdims=True) + EPS) * g


def _ada_kernel(cond_ref, w_ref, b_ref, o_ref):
    cnd = cond_ref[...]
    s = cnd * jax.nn.sigmoid(cnd)
    o_ref[0] = jnp.dot(s, w_ref[0], precision=HIGHEST, preferred_element_type=F32) + b_ref[0]


def _ada_mod(cond8, w_ada, b_ada, l):
    depth, d, n6 = w_ada.shape
    tn = 1536
    return pl.pallas_call(
        _ada_kernel,
        grid=(n6 // tn,),
        in_specs=[pl.BlockSpec((8, d), lambda j: (0, 0)),
                  pl.BlockSpec((1, d, tn), lambda j: (l, 0, j)),
                  pl.BlockSpec((1, 1, tn), lambda j: (l, 0, j))],
        out_specs=pl.BlockSpec((1, 8, tn), lambda j: (0, 0, j)),
        out_shape=jax.ShapeDtypeStruct((1, 8, n6), F32),
        compiler_params=_cparams(("parallel",)),
        name="ada_mod",
    )(cond8, w_ada, b_ada.reshape(depth, 1, n6))[0]


def _inproj_kernel(x_ref, g_ref, mod_ref, w_ref, gq_ref, gkv_ref, hg_ref, cq_ref, ckv_ref, kpe_ref):
    x = x_ref[...]
    shift = mod_ref[0, :, 0:D_MODEL]
    scale = mod_ref[0, :, D_MODEL:2 * D_MODEL]
    h = (_rms_rows(x, g_ref[...]) * (1.0 + scale) + shift).astype(BF16)
    hg_ref[...] = jnp.dot(h, w_ref[:, 0:IN_HG], preferred_element_type=F32)
    cq = jnp.dot(h, w_ref[:, IN_CQ0:IN_CKV0], preferred_element_type=F32)
    cq_ref[...] = _rms_rows(cq, gq_ref[...]).astype(BF16)
    ckv = jnp.dot(h, w_ref[:, IN_CKV0:IN_KPE0], preferred_element_type=F32)
    ckv_ref[...] = _rms_rows(ckv, gkv_ref[...])
    kpe_ref[...] = jnp.dot(h, w_ref[:, IN_KPE0:IN_PAD], preferred_element_type=F32)


def _inproj(x, g, mod, w_pad, gq, gkv, seq_len, tm):
    n = x.shape[0]
    per_seq = seq_len // tm
    n_cond = mod.shape[0]
    cond_of = (lambda i: (0, 0, 0)) if n_cond == 1 else (lambda i: (i // per_seq, 0, 0))
    row = lambda i: (i, 0)
    const = lambda i: (0, 0)
    return pl.pallas_call(
        _inproj_kernel,
        grid=(n // tm,),
        in_specs=[pl.BlockSpec((tm, D_MODEL), row),
                  pl.BlockSpec((1, D_MODEL), const),
                  pl.BlockSpec((1, 1, 6 * D_MODEL), cond_of),
                  pl.BlockSpec((D_MODEL, IN_PAD), const),
                  pl.BlockSpec((1, MLA_Q_LORA), const),
                  pl.BlockSpec((1, MLA_KV_LORA), const)],
        out_specs=[pl.BlockSpec((tm, IN_HG), row),
                   pl.BlockSpec((tm, MLA_Q_LORA), row),
                   pl.BlockSpec((tm, MLA_KV_LORA), row),
                   pl.BlockSpec((tm, LANES), row)],
        out_shape=[jax.ShapeDtypeStruct((n, IN_HG), F32),
                   jax.ShapeDtypeStruct((n, MLA_Q_LORA), BF16),
                   jax.ShapeDtypeStruct((n, MLA_KV_LORA), F32),
                   jax.ShapeDtypeStruct((n, LANES), F32)],
        compiler_params=_cparams(("parallel",)),
        name="in_proj",
    )(x, g, mod, w_pad, gq, gkv)


HG_SUB = 8
HG_UNROLL = 8


def _hgrn_kernel(q_ref, ff_ref, fb_ref, v_ref, gt_ref, lbf_ref, lbb_ref, gn_ref, s0_ref,
                 out_ref, sfin_ref, qt_ref, acc_ref, u_ref, d_ref, *, seq_len):
    C, SB = HG_CHUNK, HG_SUB
    n_sub = C // SB
    n_chunks = seq_len // C
    r2 = lax.broadcasted_iota(jnp.int32, (LANES, LANES), 0) // HG_D
    c2 = lax.broadcasted_iota(jnp.int32, (LANES, LANES), 1) // HG_D
    same_head = r2 == c2
    sel_bf = same_head.astype(BF16)
    ti = lax.broadcasted_iota(jnp.int32, (C, C), 0)
    ui = lax.broadcasted_iota(jnp.int32, (C, C), 1)
    tri = ((ui <= ti).astype(F32), (ui >= ti).astype(F32))
    s3 = lax.broadcasted_iota(jnp.int32, (SB, SB, 1), 0)
    t3 = lax.broadcasted_iota(jnp.int32, (SB, SB, 1), 1)
    diag_mask = (t3 >= s3, t3 <= s3)
    key_row = lax.broadcasted_iota(jnp.int32, (C, 1), 0)
    lane1 = lax.broadcasted_iota(jnp.int32, (1, LANES), 1)
    head_lane = ((lane1 < HG_D).astype(F32), (lane1 >= HG_D).astype(F32))
    ones_c = jnp.ones((C, LANES), F32)
    tdims = (((0,), (0,)), ((), ()))
    nt = (((1,), (1,)), ((), ()))
    z_refs, lb_refs = (ff_ref, fb_ref), (lbf_ref, lbb_ref)

    off_blocks = (tuple(range(1, n_sub)), tuple(range(0, n_sub - 1)))
    n_off = n_sub - 1
    sr = lax.broadcasted_iota(jnp.int32, (n_off * 2 * SB, n_off * C), 0) // (2 * SB)
    sc = lax.broadcasted_iota(jnp.int32, (n_off * 2 * SB, n_off * C), 1) // C
    own_block = sr == sc

    units = [(ci, d) for ci in range(HG_UNROLL) for d in range(2)]

    def prep_body(it, carry):
        cs = [it * HG_UNROLL + ci for ci in range(HG_UNROLL)]
        rows = [pl.ds(pl.multiple_of(c * C, C), C) for c in cs]
        q = [q_ref[r, :] * (HG_D ** -0.5) for r in rows]
        v = [v_ref[r, :] for r in rows]
        v_bf = [x.astype(BF16) for x in v]
        v_rep = [jnp.concatenate([x] * n_off, axis=0) for x in v_bf]
        gs, ks = [], []
        for ci, d in units:
            lb = lb_refs[d][...]
            f = jnp.maximum(lb + (1.0 - lb) * jax.nn.sigmoid(z_refs[d][rows[ci], :]), FORGET_FLOOR)
            gs.append(jnp.log(f))
            ks.append(1.0 - f)
        bs = [jnp.dot(tri[d], gs[u], precision=HIGHEST, preferred_element_type=F32) for u, (ci, d) in enumerate(units)]
        totals = [lax.dot_general(g, ones_c, tdims, precision=HIGHEST, preferred_element_type=F32) for g in gs]
        incs, sums, scores = [], [], []
        for u, (ci, d) in enumerate(units):
            b_end = bs[u][C - 1:C, :] if d == 0 else bs[u][0:1, :]
            kk = (ks[u] * jnp.exp(b_end - bs[u])).astype(BF16)
            incs.append(lax.dot_general(kk, v_bf[ci], tdims, preferred_element_type=F32))
        for u, (ci, d) in enumerate(units):
            blocks = []
            for blk in range(n_sub):
                sl = slice(blk * SB, (blk + 1) * SB)
                bi = bs[u][sl]
                diff = bi[None, :, :] - bi[:, None, :]
                e = jnp.where(diag_mask[d], jnp.exp(jnp.where(diag_mask[d], diff, 0.0)), 0.0)
                blocks.append((e * q[ci][sl][None, :, :] * ks[u][sl][:, None, :]).reshape(SB * SB, LANES))
            sums.append(jnp.dot(jnp.concatenate(blocks, axis=0).astype(BF16), sel_bf, preferred_element_type=F32))
        for u, (ci, d) in enumerate(units):
            lhs, khs = [], []
            for blk in off_blocks[d]:
                sl = slice(blk * SB, (blk + 1) * SB)
                if d == 0:
                    r = bs[u][blk * SB - 1:blk * SB, :]
                    keys = key_row < blk * SB
                else:
                    r = bs[u][(blk + 1) * SB:(blk + 1) * SB + 1, :]
                    keys = key_row >= (blk + 1) * SB
                qh = q[ci][sl] * jnp.exp(bs[u][sl] - r)
                lhs += [qh * head_lane[0], qh * head_lane[1]]
                khs.append(jnp.where(keys, ks[u] * jnp.exp(jnp.minimum(r - bs[u], 0.0)), 0.0))
            scores.append(lax.dot_general(jnp.concatenate(lhs, axis=0).astype(BF16),
                                          jnp.concatenate(khs, axis=0).astype(BF16), nt,
                                          preferred_element_type=F32))
        o2s = [jnp.dot(jnp.where(own_block, scores[u], 0.0).astype(BF16), v_rep[ci], preferred_element_type=F32)
               for u, (ci, d) in enumerate(units)]
        for u, (ci, d) in enumerate(units):
            outs = []
            for blk in range(n_sub):
                sl = slice(blk * SB, (blk + 1) * SB)
                a_blk = sums[u][blk * SB * SB:(blk + 1) * SB * SB].reshape(SB, SB, LANES)
                o_blk = jnp.sum(a_blk * v[ci][sl][:, None, :], axis=0)
                if blk in off_blocks[d]:
                    j = off_blocks[d].index(blk) * 2 * SB
                    o_blk = o_blk + o2s[u][j:j + SB] * head_lane[0] + o2s[u][j + SB:j + 2 * SB] * head_lane[1]
                outs.append(o_blk)
            qt_ref[d, rows[ci], :] = (q[ci] * jnp.exp(bs[u])).astype(BF16)
            u_ref[d, cs[ci]] = jnp.where(same_head, incs[u], 0.0)
            d_ref[d, cs[ci]] = jnp.exp(totals[u])
            acc_ref[d, rows[ci], :] = jnp.concatenate(outs, axis=0)
        return carry

    lax.fori_loop(0, n_chunks // HG_UNROLL, prep_body, 0)

    def block_diag(s_a, s_b):
        z = jnp.zeros((HG_D, HG_D), F32)
        return jnp.concatenate([jnp.concatenate([s_a, z], axis=1),
                                jnp.concatenate([z, s_b], axis=1)], axis=0)

    def scan_body(i, carry):
        s_f, s_b = carry
        cb = n_chunks - 1 - i
        inc_f, inc_b = u_ref[0, i], u_ref[1, cb]
        u_ref[0, i] = s_f
        u_ref[1, cb] = s_b
        return d_ref[0, i] * s_f + inc_f, d_ref[1, cb] * s_b + inc_b

    s_f, s_b = lax.fori_loop(0, n_chunks, scan_body,
                             (block_diag(s0_ref[0, 0, 0], s0_ref[0, 0, 1]),
                              block_diag(s0_ref[0, 1, 0], s0_ref[0, 1, 1])))
    sfin_ref[0, 0, 0] = s_f[0:HG_D, 0:HG_D]
    sfin_ref[0, 0, 1] = s_f[HG_D:, HG_D:]
    sfin_ref[0, 1, 0] = s_b[0:HG_D, 0:HG_D]
    sfin_ref[0, 1, 1] = s_b[HG_D:, HG_D:]

    bdims = (((2,), (1,)), ((0,), (0,)))
    o = acc_ref[0] + acc_ref[1]
    for d in range(2):
        inter = lax.dot_general(qt_ref[d].reshape(n_chunks, C, LANES), u_ref[d].astype(BF16), bdims,
                                preferred_element_type=F32)
        o = o + inter.reshape(seq_len, LANES)
    ms = _dot_sel(o * o, sel_bf) * (1.0 / HG_D)
    gt = gt_ref[...]
    out_ref[...] = (o * lax.rsqrt(ms + EPS) * gn_ref[...] * (gt * jax.nn.sigmoid(gt))).astype(BF16)


def _hgrn(hg, lb_f, lb_b, gn2, s0, batch, seq_len):
    n = hg.shape[0]
    pairs = HG_W // LANES
    part = lambda p: pl.BlockSpec((seq_len, LANES), lambda b, j, p=p: (b, p * pairs + j))
    lane_row = pl.BlockSpec((1, LANES), lambda b, j: (0, j))
    st = pl.BlockSpec((1, 2, 2, HG_D, HG_D), lambda b, j: (b, 0, j, 0, 0))
    return pl.pallas_call(
        functools.partial(_hgrn_kernel, seq_len=seq_len),
        grid=(batch, pairs),
        in_specs=[part(0), part(1), part(2), part(3), part(4), lane_row, lane_row,
                  pl.BlockSpec((1, LANES), lambda b, j: (0, 0)), st],
        out_specs=[pl.BlockSpec((seq_len, LANES), lambda b, j: (b, j)), st],
        out_shape=[jax.ShapeDtypeStruct((n, HG_W), BF16),
                   jax.ShapeDtypeStruct((batch, 2, HG_HEADS, HG_D, HG_D), F32)],
        scratch_shapes=[pltpu.VMEM((2, seq_len, LANES), BF16),
                        pltpu.VMEM((2, seq_len, LANES), F32),
                        pltpu.VMEM((2, seq_len // HG_CHUNK, LANES, LANES), F32),
                        pltpu.VMEM((2, seq_len // HG_CHUNK, LANES, LANES), F32)],
        compiler_params=_cparams(("parallel", "parallel")),
        name="hgrn2",
    )(hg, hg, hg, hg, hg, lb_f, lb_b, gn2, s0)


def _rope_lanes(x, cos, sin):
    return x * cos + pltpu.roll(x, 96, 1) * sin


def _qproj_kernel(cq_ref, w_ref, cos_ref, sin_ref, q_ref):
    cq = cq_ref[...]
    for h in range(MLA_HEADS):
        qh = jnp.dot(cq, w_ref[h], preferred_element_type=F32)
        q_ref[h] = (_rope_lanes(qh, cos_ref[...], sin_ref[...]) * MLA_SCALE).astype(BF16)


def _kvproj_kernel(ckv_ref, kpe_ref, w_ref, cos_ref, sin_ref, k_ref, kv_ref):
    ckv = ckv_ref[...].astype(BF16)
    kpe = _rope_lanes(kpe_ref[...], cos_ref[...], sin_ref[...])
    nope = lax.broadcasted_iota(jnp.int32, kpe.shape, 1) < MLA_NOPE
    for h in range(MLA_HEADS):
        kvh = jnp.dot(ckv, w_ref[h], preferred_element_type=F32)
        kv_ref[h] = kvh.astype(BF16)
        k_ref[h] = jnp.where(nope, kvh, kpe).astype(BF16)


def _qproj(cqn, wq, cos, sin, tm):
    n = cqn.shape[0]
    pos_blocks = cos.shape[0] // tm
    return pl.pallas_call(
        _qproj_kernel,
        grid=(n // tm,),
        in_specs=[pl.BlockSpec((tm, MLA_Q_LORA), lambda i: (i, 0)),
                  pl.BlockSpec((MLA_HEADS, MLA_Q_LORA, LANES), lambda i: (0, 0, 0)),
                  pl.BlockSpec((tm, LANES), lambda i: (i % pos_blocks, 0)),
                  pl.BlockSpec((tm, LANES), lambda i: (i % pos_blocks, 0))],
        out_specs=pl.BlockSpec((MLA_HEADS, tm, LANES), lambda i: (0, i, 0)),
        out_shape=jax.ShapeDtypeStruct((MLA_HEADS, n, LANES), BF16),
        compiler_params=_cparams(("parallel",)),
        name="mla_q_proj",
    )(cqn, wq, cos, sin)


def _kvproj(ckvn, kpe128, wkv, cos, sin, tm):
    n = ckvn.shape[0]
    pos_blocks = cos.shape[0] // tm
    hm = pl.BlockSpec((MLA_HEADS, tm, LANES), lambda i: (0, i, 0))
    return pl.pallas_call(
        _kvproj_kernel,
        grid=(n // tm,),
        in_specs=[pl.BlockSpec((tm, MLA_KV_LORA), lambda i: (i, 0)),
                  pl.BlockSpec((tm, LANES), lambda i: (i, 0)),
                  pl.BlockSpec((MLA_HEADS, MLA_KV_LORA, LANES), lambda i: (0, 0, 0)),
                  pl.BlockSpec((tm, LANES), lambda i: (i % pos_blocks, 0)),
                  pl.BlockSpec((tm, LANES), lambda i: (i % pos_blocks, 0))],
        out_specs=[hm, hm],
        out_shape=[jax.ShapeDtypeStruct((MLA_HEADS, n, LANES), BF16)] * 2,
        compiler_params=_cparams(("parallel",)),
        name="mla_kv_proj",
    )(ckvn, kpe128, wkv, cos, sin)


def _attn_kernel(q_ref, k_ref, kv_ref, o_ref):
    low = lax.broadcasted_iota(jnp.int32, (q_ref.shape[1], LANES), 1) < MLA_V
    nt = (((1,), (1,)), ((), ()))
    outs = []
    for h in range(MLA_HEADS):
        s = lax.dot_general(q_ref[h], k_ref[h], nt, preferred_element_type=F32)
        p = jnp.exp(s - jnp.max(s, axis=-1, keepdims=True))
        denom = jnp.sum(p, axis=-1, keepdims=True)
        o = jnp.dot(p.astype(BF16), kv_ref[h], preferred_element_type=F32) / denom
        outs.append(o)
    for j in range(MLA_HEADS // 2):
        pair = jnp.where(low, pltpu.roll(outs[2 * j], MLA_V, 1), outs[2 * j + 1])
        o_ref[:, j * LANES:(j + 1) * LANES] = pair.astype(BF16)


def _attention(q128, k128, kv128, batch, lq, lk, tq):
    n = q128.shape[1]
    qb = lq // tq
    kspec = pl.BlockSpec((MLA_HEADS, lk, LANES), lambda b, i: (0, b, 0))
    return pl.pallas_call(
        _attn_kernel,
        grid=(batch, qb),
        in_specs=[pl.BlockSpec((MLA_HEADS, tq, LANES), lambda b, i: (0, b * qb + i, 0)), kspec, kspec],
        out_specs=pl.BlockSpec((tq, MLA_HEADS * MLA_V), lambda b, i: (b * qb + i, 0)),
        out_shape=jax.ShapeDtypeStruct((n, MLA_HEADS * MLA_V), BF16),
        compiler_params=_cparams(("parallel", "parallel")),
        name="mla_attention",
    )(q128, k128, kv128)


def _outproj_kernel(x_ref, hg_ref, at_ref, w_ref, mod_ref, o_ref):
    gate = mod_ref[0, :, 2 * D_MODEL:3 * D_MODEL]
    mix = (jnp.dot(hg_ref[...], w_ref[0:HG_W, :], preferred_element_type=F32)
           + jnp.dot(at_ref[...], w_ref[HG_W:, :], preferred_element_type=F32))
    o_ref[...] = x_ref[...] + gate * mix


def _outproj(x, hg_out, attn, w_out, mod, seq_len, tm):
    n = x.shape[0]
    per_seq = seq_len // tm
    n_cond = mod.shape[0]
    cond_of = (lambda i: (0, 0, 0)) if n_cond == 1 else (lambda i: (i // per_seq, 0, 0))
    row = lambda i: (i, 0)
    return pl.pallas_call(
        _outproj_kernel,
        grid=(n // tm,),
        in_specs=[pl.BlockSpec((tm, D_MODEL), row),
                  pl.BlockSpec((tm, HG_W), row),
                  pl.BlockSpec((tm, MLA_HEADS * MLA_V), row),
                  pl.BlockSpec((2 * HG_W, D_MODEL), lambda i: (0, 0)),
                  pl.BlockSpec((1, 1, 6 * D_MODEL), cond_of)],
        out_specs=pl.BlockSpec((tm, D_MODEL), row),
        out_shape=jax.ShapeDtypeStruct((n, D_MODEL), F32),
        compiler_params=_cparams(("parallel",)),
        name="out_proj",
    )(x, hg_out, attn, w_out, mod)


_CAND = [(a, b) for a in range(PEER_TOPK) for b in range(PEER_TOPK) if (a + 1) * (b + 1) <= PEER_TOPK]
_CAND_PAD = -(-len(_CAND) // 8) * 8


def _topk_rows(work, k):
    rows = lax.broadcasted_iota(jnp.int32, work.shape, 0).astype(F32)
    n_rows = float(work.shape[0])
    vals, idxs = [], []
    for _ in range(k):
        m = jnp.max(work, axis=0, keepdims=True)
        sel = jnp.min(jnp.where(work == m, rows, n_rows), axis=0, keepdims=True)
        vals.append(m)
        idxs.append(sel)
        work = jnp.where(rows == sel, -jnp.inf, work)
    return jnp.concatenate(vals, axis=0), jnp.concatenate(idxs, axis=0)


def _peerq_kernel(x_ref, g_ref, mod_ref, w_ref, k1_ref, k2_ref, after_ref, h2_ref, idx_ref, gate_ref):
    del after_ref
    idx_rows, gate_rows = [], []
    x = x_ref[...]
    shift = mod_ref[0, :, 3 * D_MODEL:4 * D_MODEL]
    scale = mod_ref[0, :, 4 * D_MODEL:5 * D_MODEL]
    h2 = _rms_rows(x, g_ref[...]) * (1.0 + scale) + shift
    h2b = h2.astype(BF16)
    bits = lax.bitcast_convert_type(h2b.astype(F32), jnp.uint32)
    h2_ref[...] = (bits[:, :PACK_W] >> 16) | bits[:, PACK_W:]
    qp = jnp.dot(h2b, w_ref[...], preferred_element_type=F32)
    nt = (((1,), (1,)), ((), ()))
    half = PEER_DQ // 2
    for h in range(PEER_HEADS):
        q1 = qp[:, h * PEER_DQ:h * PEER_DQ + half]
        q2 = qp[:, h * PEER_DQ + half:(h + 1) * PEER_DQ]
        s1 = _dot3(k1_ref[h], q1, nt)
        s2 = _dot3(k2_ref[h], q2, nt)
        v1, i1 = _topk_rows(s1, PEER_TOPK)
        v2, i2 = _topk_rows(s2, PEER_TOPK)
        cs = [v1[a:a + 1, :] + v2[b:b + 1, :] for a, b in _CAND]
        ci = [i1[a:a + 1, :] * PEER_NKEYS + i2[b:b + 1, :] for a, b in _CAND]
        n_pad = _CAND_PAD - len(_CAND)
        if n_pad:
            cs.append(jnp.full((n_pad, x.shape[0]), -jnp.inf, F32))
            ci.append(jnp.zeros((n_pad, x.shape[0]), F32))
        cand_s = jnp.concatenate(cs, axis=0)
        cand_i = jnp.concatenate(ci, axis=0)
        best_s, best_pos = _topk_rows(cand_s, PEER_TOPK)
        rows = lax.broadcasted_iota(jnp.int32, cand_s.shape, 0).astype(F32)
        picked = [jnp.sum(jnp.where(rows == best_pos[r:r + 1, :], cand_i, 0.0), axis=0, keepdims=True)
                  for r in range(PEER_TOPK)]
        idx_rows.extend(picked)
        ex = jnp.exp(best_s - best_s[0:1, :])
        gate_rows.append(ex / jnp.sum(ex, axis=0, keepdims=True))
    idx_ref[...] = jnp.concatenate(idx_rows, axis=0).T.astype(jnp.int32)
    gate_ref[...] = jnp.concatenate(gate_rows, axis=0).T


def _peerq(x, g, mod, wq, k1, k2, seq_len, tm, after):
    n = x.shape[0]
    per_seq = seq_len // tm
    n_cond = mod.shape[0]
    cond_of = (lambda i: (0, 0, 0)) if n_cond == 1 else (lambda i: (i // per_seq, 0, 0))
    kspec = pl.BlockSpec((PEER_HEADS, PEER_NKEYS, PEER_DQ // 2), lambda i: (0, 0, 0))
    tk = PEER_HEADS * PEER_TOPK
    return pl.pallas_call(
        _peerq_kernel,
        grid=(n // tm,),
        in_specs=[pl.BlockSpec((tm, D_MODEL), lambda i: (i, 0)),
                  pl.BlockSpec((1, D_MODEL), lambda i: (0, 0)),
                  pl.BlockSpec((1, 1, 6 * D_MODEL), cond_of),
                  pl.BlockSpec((D_MODEL, PEER_HEADS * PEER_DQ), lambda i: (0, 0)),
                  kspec, kspec,
                  pl.BlockSpec(memory_space=pl.ANY)],
        out_specs=[pl.BlockSpec((tm, D_MODEL // 2), lambda i: (i, 0)),
                   pl.BlockSpec((tm, tk), lambda i: (i, 0)),
                   pl.BlockSpec((tm, tk), lambda i: (i, 0))],
        out_shape=[jax.ShapeDtypeStruct((n, D_MODEL // 2), jnp.uint32),
                   jax.ShapeDtypeStruct((n, tk), jnp.int32),
                   jax.ShapeDtypeStruct((n, tk), F32)],
        compiler_params=_cparams(("parallel",)),
        name="peer_retrieve",
    )(x, g, mod, wq, k1, k2, after)


SC_CORES = 2
SC_SUBCORES = 16
SC_LANES = 16
SC_TILES = SC_CORES * SC_SUBCORES
SC_NBUF_U = 4
SC_NBUF_V = 4
ITEM_HEADS = 2
ITEM_ROWS = ITEM_HEADS * PEER_TOPK
ITEMS_PER_TOKEN = PEER_HEADS // ITEM_HEADS
PEER_TK = PEER_HEADS * PEER_TOPK
D_VREGS = D_MODEL // SC_LANES
BF_RUN = 4


PACK_W = D_MODEL // 2


def _pack_kernel(t_ref, o_ref):
    bits = lax.bitcast_convert_type(t_ref[...].astype(BF16).astype(F32), jnp.uint32)
    o_ref[...] = (bits[:, :PACK_W] >> 16) | bits[:, PACK_W:]


def _pack_bf16_pairs(tables, l, tm=1024):
    n = tables.shape[1]
    return pl.pallas_call(
        _pack_kernel,
        grid=(n // tm,),
        in_specs=[pl.BlockSpec((None, tm, D_MODEL), lambda i: (l, i, 0))],
        out_specs=pl.BlockSpec((tm, PACK_W), lambda i: (i, 0)),
        out_shape=jax.ShapeDtypeStruct((n, PACK_W), jnp.uint32),
        compiler_params=_cparams(("parallel",)),
        name="pack_expert_table",
    )(tables)


def _unpack_pair(word):
    lo = lax.bitcast_convert_type(word << 16, F32)
    hi = lax.bitcast_convert_type(word & jnp.uint32(0xFFFF0000), F32)
    return lo, hi


def _sc_mesh():
    return plsc.VectorSubcoreMesh(core_axis_name="c", subcore_axis_name="s")


def _sc_params():
    return pltpu.CompilerParams(needs_layout_passes=False)


def _sc_token_base(tokens_per_tile):
    return (lax.axis_index("s") * SC_CORES + lax.axis_index("c")) * tokens_per_tile


def _peer_u_body(h2_hbm, idx_hbm, u_hbm, after_hbm, act_hbm, idx_v, act_v, x_v, rows_v, sem_x, sem_g, *, tpt,
                 nbuf):
    del after_hbm
    base = _sc_token_base(tpt)
    n_items = tpt * ITEMS_PER_TOKEN
    pltpu.sync_copy(idx_hbm.at[pl.ds(base, tpt)], idx_v)

    def gather(i, slot):
        t, g = i // ITEMS_PER_TOKEN, i % ITEMS_PER_TOKEN
        return pltpu.make_async_copy(u_hbm.at[idx_v.at[t, pl.ds(g * ITEM_ROWS, ITEM_ROWS)]],
                                     rows_v.at[slot], sem_g.at[slot])

    def x_copy(t):
        return pltpu.make_async_copy(h2_hbm.at[base + t], x_v.at[t % 2], sem_x.at[t % 2])

    x_copy(0).start()
    for b in range(nbuf - 1):
        gather(b, b).start()
    lane = lax.iota(jnp.int32, SC_LANES)
    zero = jnp.zeros((SC_LANES,), F32)

    @pl.loop(0, n_items)
    def _(i):
        for b in (i % nbuf,):
            t, g = i // ITEMS_PER_TOKEN, i % ITEMS_PER_TOKEN

            @pl.when(g == 0)
            def _():
                x_copy(t).wait()

                @pl.when(t + 1 < tpt)
                def _():
                    x_copy(t + 1).start()

            @pl.when(i + nbuf - 1 < n_items)
            def _():
                gather(i + nbuf - 1, (b + nbuf - 1) % nbuf).start()

            gather(i, b).wait()
            xs = t % 2

            @pl.loop(0, ITEM_HEADS)
            def _(sub):
                r0 = sub * PEER_TOPK

                def cbody(cg, accs):
                    parts = [None] * PEER_TOPK
                    for cc in range(BF_RUN):
                        col = pl.ds((cg * BF_RUN + cc) * SC_LANES, SC_LANES)
                        xb = plsc.bitcast(x_v[xs, col], BF16)
                        for j in range(PEER_TOPK):
                            p = xb * plsc.bitcast(rows_v[b, r0 + j, col], BF16)
                            parts[j] = p if cc == 0 else parts[j] + p
                    out = []
                    for j, a in enumerate(accs):
                        lo, hi = _unpack_pair(plsc.bitcast(parts[j], jnp.uint32))
                        out.append(a + lo + hi)
                    return tuple(out)

                accs = plsc.parallel_loop(0, PACK_W // SC_LANES // BF_RUN, carry=(zero,) * PEER_TOPK)(cbody)
                vecs = list(accs)
                s = SC_LANES // 2
                while s >= 1:
                    swap = lane ^ s
                    pick_first = (lane & s) == 0
                    folded = [v + v.at[swap].get(mode="promise_in_bounds") for v in vecs]
                    vecs = [jnp.where(pick_first, folded[j], folded[j + s]) for j in range(s)]
                    s //= 2
                act_v[t, pl.ds(g * ITEM_ROWS + r0, PEER_TOPK)] = vecs[0]

    pltpu.sync_copy(act_v, act_hbm.at[pl.ds(base, tpt)])


def _peer_v_body(w_hbm, idx_hbm, v_hbm, after_hbm, y_hbm, idx_v, w_v, y_v, rows_v, sem_y, sem_g, *, tpt,
                 nbuf):
    del after_hbm
    base = _sc_token_base(tpt)
    n_items = tpt * ITEMS_PER_TOKEN
    pltpu.sync_copy(idx_hbm.at[pl.ds(base, tpt)], idx_v)
    pltpu.sync_copy(w_hbm.at[pl.ds(base, tpt)], w_v)

    def gather(i, slot):
        t, g = i // ITEMS_PER_TOKEN, i % ITEMS_PER_TOKEN
        return pltpu.make_async_copy(v_hbm.at[idx_v.at[t, pl.ds(g * ITEM_ROWS, ITEM_ROWS)]],
                                     rows_v.at[slot], sem_g.at[slot])

    def y_copy(t):
        return pltpu.make_async_copy(y_v.at[t % 2], y_hbm.at[base + t], sem_y.at[t % 2])

    for b in range(nbuf - 1):
        gather(b, b).start()
    lane = lax.iota(jnp.int32, SC_LANES)
    zero = jnp.zeros((SC_LANES,), F32)

    @pl.loop(0, n_items)
    def _(i):
        for b in (i % nbuf,):
            t, g = i // ITEMS_PER_TOKEN, i % ITEMS_PER_TOKEN
            ys = t % 2

            @pl.when(i + nbuf - 1 < n_items)
            def _():
                gather(i + nbuf - 1, (b + nbuf - 1) % nbuf).start()

            @pl.when(g == 0)
            def _():
                @pl.when(t >= 2)
                def _():
                    y_copy(t - 2).wait()

                @plsc.parallel_loop(0, D_VREGS, unroll=8)
                def _(c):
                    y_v[ys, pl.ds(c * SC_LANES, SC_LANES)] = zero

            gather(i, b).wait()
            for sub in range(ITEM_HEADS):
                r0 = sub * PEER_TOPK
                wv = w_v[t, pl.ds(g * ITEM_ROWS + r0, PEER_TOPK)]
                ws = [plsc.bitcast(wv.at[jnp.full((SC_LANES,), j, jnp.int32)].get(mode="promise_in_bounds"), BF16)
                      for j in range(PEER_TOPK)]

                @plsc.parallel_loop(0, PACK_W // SC_LANES)
                def _(c, r0=r0, ws=ws):
                    col = pl.ds(c * SC_LANES, SC_LANES)
                    col_hi = pl.ds(PACK_W + c * SC_LANES, SC_LANES)
                    terms = [ws[j] * plsc.bitcast(rows_v[b, r0 + j, col], BF16) for j in range(PEER_TOPK)]
                    while len(terms) > 1:
                        terms = [terms[k] + terms[k + 1] for k in range(0, len(terms), 2)]
                    lo, hi = _unpack_pair(plsc.bitcast(terms[0], jnp.uint32))
                    y_v[ys, col] = y_v[ys, col] + lo
                    y_v[ys, col_hi] = y_v[ys, col_hi] + hi

            @pl.when(g == ITEMS_PER_TOKEN - 1)
            def _():
                y_copy(t).start()

    y_copy(tpt - 2).wait()
    y_copy(tpt - 1).wait()


def _sc_scratch(tpt, side_dtype, token_buf, nbuf):
    return [pltpu.VMEM((tpt, PEER_TK), jnp.int32),
            pltpu.VMEM((tpt, PEER_TK), side_dtype),
            token_buf,
            pltpu.VMEM((nbuf, ITEM_ROWS, PACK_W), jnp.uint32),
            pltpu.SemaphoreType.DMA((2,)),
            pltpu.SemaphoreType.DMA((nbuf,))]


def _peer_u(h2, idx, u_all, after):
    n = h2.shape[0]
    tpt = n // SC_TILES
    return pl.kernel(
        functools.partial(_peer_u_body, tpt=tpt, nbuf=SC_NBUF_U),
        out_type=jax.ShapeDtypeStruct((n, PEER_TK), F32),
        mesh=_sc_mesh(),
        scratch_types=_sc_scratch(tpt, F32, pltpu.VMEM((2, PACK_W), jnp.uint32), SC_NBUF_U),
        compiler_params=_sc_params(),
        name="peer_u_sc",
    )(h2, idx, u_all, after)


def _peer_v(w, idx, v_all, after):
    n = w.shape[0]
    tpt = n // SC_TILES
    return pl.kernel(
        functools.partial(_peer_v_body, tpt=tpt, nbuf=SC_NBUF_V),
        out_type=jax.ShapeDtypeStruct((n, D_MODEL), F32),
        mesh=_sc_mesh(),
        scratch_types=_sc_scratch(tpt, jnp.int32, pltpu.VMEM((2, D_MODEL), F32), SC_NBUF_V),
        compiler_params=_sc_params(),
        name="peer_v_sc",
    )(w, idx, v_all, after)


def _peer_gelu_kernel(gate_ref, act_ref, w_ref):
    act = act_ref[...]
    w = gate_ref[...] * (0.5 * act * (1.0 + lax.erf(act * (2.0 ** -0.5))))
    bits = lax.bitcast_convert_type(w.astype(BF16).astype(F32), jnp.uint32)
    w_ref[...] = lax.bitcast_convert_type(bits | (bits >> 16), jnp.int32)


def _peer_gelu(gate, act, tm):
    n = gate.shape[0]
    spec = pl.BlockSpec((tm, PEER_TK), lambda i: (i, 0))
    return pl.pallas_call(
        _peer_gelu_kernel,
        grid=(n // tm,),
        in_specs=[spec, spec],
        out_specs=spec,
        out_shape=jax.ShapeDtypeStruct((n, PEER_TK), jnp.int32),
        compiler_params=_cparams(("parallel",)),
        name="peer_gelu",
    )(gate, act)


def _peer_residual_kernel(x_ref, y_ref, mod_ref, o_ref):
    o_ref[...] = x_ref[...] + mod_ref[0, :, 5 * D_MODEL:6 * D_MODEL] * y_ref[...]


def _peer_residual(x, y, mod, seq_len, tm):
    n = x.shape[0]
    per_seq = seq_len // tm
    n_cond = mod.shape[0]
    cond_of = (lambda i: (0, 0, 0)) if n_cond == 1 else (lambda i: (i // per_seq, 0, 0))
    row = pl.BlockSpec((tm, D_MODEL), lambda i: (i, 0))
    return pl.pallas_call(
        _peer_residual_kernel,
        grid=(n // tm,),
        in_specs=[row, row, pl.BlockSpec((1, 1, 6 * D_MODEL), cond_of)],
        out_specs=row,
        out_shape=jax.ShapeDtypeStruct((n, D_MODEL), F32),
        compiler_params=_cparams(("parallel",)),
        name="peer_residual",
    )(x, y, mod)


def _final_norm_kernel(x_ref, g_ref, o_ref):
    o_ref[...] = _rms_rows(x_ref[...], g_ref[...])


def _final_norm(x, g, tm):
    n = x.shape[0]
    return pl.pallas_call(
        _final_norm_kernel,
        grid=(n // tm,),
        in_specs=[pl.BlockSpec((tm, D_MODEL), lambda i: (i, 0)), pl.BlockSpec((1, D_MODEL), lambda i: (0, 0))],
        out_specs=pl.BlockSpec((tm, D_MODEL), lambda i: (i, 0)),
        out_shape=jax.ShapeDtypeStruct((n, D_MODEL), F32),
        compiler_params=_cparams(("parallel",)),
        name="final_norm",
    )(x, g)


def _rot_half_columns(w):
    half = ROPE_AXIS_DIM // 2
    w4 = w.reshape(w.shape[:-1] + (2, 2, half))
    return jnp.stack([-w4[..., 1, :], w4[..., 0, :]], axis=-2).reshape(w.shape)


def _rope_tables(seq_len, rotate):
    cos = np.zeros((seq_len, LANES), np.float32)
    sin = np.zeros((seq_len, LANES), np.float32)
    cos[:, :MLA_NOPE + MLA_ROPE] = 1.0
    if rotate:
        pos = np.arange(seq_len)
        half = ROPE_AXIS_DIM // 2
        inv = (1.0 / (ROPE_BASE ** (np.arange(0, ROPE_AXIS_DIM, 2, dtype=np.float32) / ROPE_AXIS_DIM))).astype(np.float32)
        for axis, p in enumerate(((pos // GRID_W).astype(np.float32), (pos % GRID_W).astype(np.float32))):
            ang = p[:, None] * inv[None, :]
            for part in range(2):
                lo = MLA_NOPE + axis * ROPE_AXIS_DIM + part * half
                cos[:, lo:lo + half] = np.cos(ang)
                sin[:, lo:lo + half] = np.sin(ang)
    return jnp.asarray(cos), jnp.asarray(sin)


def _prep_layer_weights(l, w_in, mla_wqb, mla_wkvb, w_out, peer_wq):
    wi = w_in[l]
    kpe_w = wi[:, IN_KPE0:IN_KPE0 + MLA_ROPE]
    w_pad = jnp.concatenate(
        [wi[:, :IN_KPE0], jnp.zeros((D_MODEL, MLA_NOPE), F32), kpe_w, _rot_half_columns(kpe_w)], axis=1).astype(BF16)
    wq = mla_wqb[l].reshape(MLA_Q_LORA, MLA_HEADS, MLA_NOPE + MLA_ROPE)
    wq_pe = wq[..., MLA_NOPE:]
    wq128 = jnp.concatenate([wq, _rot_half_columns(wq_pe)], axis=-1).transpose(1, 0, 2).astype(BF16)
    wkv128 = mla_wkvb[l].reshape(MLA_KV_LORA, MLA_HEADS, MLA_NOPE + MLA_V).transpose(1, 0, 2).astype(BF16)
    return w_pad, wq128, wkv128, w_out[l].astype(BF16), peer_wq[l].astype(BF16)


TOK_TILE = 256


def _mixer_stage(x, mod, l, W, P, lb, batch, seq_len, ctx, rope):
    w_pad, wq128, wkv128, wo, _ = W
    tm = TOK_TILE
    hg, cqn, ckvn, kpe128 = _inproj(x, P['norm_mix'][l][None], mod, w_pad, P['mla_q_norm'][l][None],
                                    P['mla_kv_norm'][l][None], seq_len, tm)
    if ctx is None:
        s0 = jnp.zeros((batch, 2, HG_HEADS, HG_D, HG_D), F32)
    else:
        s0 = ctx[2]
    gn2 = jnp.tile(P['hgrn_norm'][l], 2)[None]
    hg_out, s_fin = _hgrn(hg, lb[0, l][None], lb[1, l][None], gn2, s0, batch, seq_len)

    cos, sin = rope
    q128 = _qproj(cqn, wq128, cos, sin, tm)
    k128, kv128 = _kvproj(ckvn, kpe128, wkv128, cos, sin, tm)
    lk = seq_len
    if ctx is not None:
        past = ctx[0].shape[1]
        cos_c, sin_c = _rope_tables(past, False)
        kpe_c = jnp.pad(ctx[1].reshape(batch * past, MLA_ROPE), ((0, 0), (MLA_NOPE, LANES - MLA_NOPE - MLA_ROPE)))
        k_c, kv_c = _kvproj(ctx[0].reshape(batch * past, MLA_KV_LORA), kpe_c, wkv128, cos_c, sin_c, past)
        join = lambda a, b: jnp.concatenate(
            [a.reshape(MLA_HEADS, batch, past, LANES), b.reshape(MLA_HEADS, batch, seq_len, LANES)],
            axis=2).reshape(MLA_HEADS, batch * (past + seq_len), LANES)
        k128, kv128 = join(k_c, k128), join(kv_c, kv128)
        lk = past + seq_len
    attn = _attention(q128, k128, kv128, batch, seq_len, lk, tm)
    x = _outproj(x, hg_out, attn, wo, mod, seq_len, tm)
    return x, (ckvn, kpe128[:, MLA_NOPE:MLA_NOPE + MLA_ROPE], s_fin)


def _retrieve_stage(x, mod, l, W, P, seq_len, after):
    keys = P['peer_keys'][l]
    return _peerq(x, P['norm_ffn'][l][None], mod, W[4], keys[0], keys[1], seq_len, TOK_TILE, after)


def kernel(x_prompt, x_sample, cache_ckv, cache_kpe, state_hgrn, c, c_ctx, w_ada, b_ada, norm_mix, w_in,
           hgrn_lb_logits, hgrn_norm, mla_q_norm, mla_wqb, mla_kv_norm, mla_wkvb, w_out, norm_ffn,
           peer_wq, peer_keys, peer_u, peer_v, final_norm):
    P = {'norm_mix': norm_mix, 'hgrn_norm': hgrn_norm, 'mla_q_norm': mla_q_norm, 'mla_kv_norm': mla_kv_norm,
         'norm_ffn': norm_ffn, 'peer_keys': peer_keys}
    depth = w_in.shape[0]
    u_packed = [_pack_bf16_pairs(peer_u, l) for l in range(depth)]
    v_packed = [_pack_bf16_pairs(peer_v, l) for l in range(depth)]
    bp, lp, _ = x_prompt.shape
    bs, ls, _ = x_sample.shape

    p = jax.nn.softmax(hgrn_lb_logits.astype(F32), axis=1)
    lb = jnp.cumsum(p, axis=1) - p[:, :1]

    cond8 = jnp.zeros((8, D_MODEL), F32).at[0].set(c_ctx).at[1:1 + bs].set(c)
    mod_all = [_ada_mod(cond8, w_ada, b_ada, l) for l in range(depth)]

    rope_p = _rope_tables(lp, False)
    rope_s = _rope_tables(ls, True)

    ckvs, kpes, states = [], [], []
    Ws = [_prep_layer_weights(l, w_in, mla_wqb, mla_wkvb, w_out, peer_wq) for l in range(depth)]
    mods_p = [mod_all[l][0:1][:, None, :] for l in range(depth)]
    mods_s = [mod_all[l][1:1 + bs][:, None, :] for l in range(depth)]

    def mixer_s(x, l):
        return _mixer_stage(x, mods_s[l], l, Ws[l], P, lb, bs, ls,
                            (cache_ckv[:, l], cache_kpe[:, l], state_hgrn[:, l]), rope_s)[0]

    def mixer_p(x, l):
        x, (ckv_l, kpe_l, st_l) = _mixer_stage(x, mods_p[l], l, Ws[l], P, lb, bp, lp, None, rope_p)
        ckvs.append(ckv_l.reshape(bp, lp, MLA_KV_LORA))
        kpes.append(kpe_l.reshape(bp, lp, MLA_ROPE))
        states.append(st_l)
        return x

    tm = TOK_TILE
    mixers, mods, lens = (mixer_p, mixer_s), (mods_p, mods_s), (lp, ls)
    x = [x_prompt.reshape(bp * lp, D_MODEL), x_sample.reshape(bs * ls, D_MODEL)]
    none_yet = jnp.zeros((8, D_MODEL), F32)

    def retrieve(g, l, after):
        return _retrieve_stage(x[g], mods[g][l], l, Ws[l], P, lens[g], after)

    x[0] = mixers[0](x[0], 0)
    h2, idx, gate = retrieve(0, 0, jnp.zeros((8, PEER_TK), jnp.int32))
    q = [(idx, gate, _peer_u(h2, idx, u_packed[0], none_yet)), None]
    for l in range(depth):
        last = l + 1 == depth
        x[1] = mixers[1](x[1], l)
        idx, gate, act = q[0]
        w0 = _peer_gelu(gate, act, tm)
        y0 = _peer_v(w0, idx, v_packed[l], x[1])
        h2, idx, gate = retrieve(1, l, w0)
        q[1] = (idx, gate, _peer_u(h2, idx, u_packed[l], y0))
        x[0] = _peer_residual(x[0], y0, mods[0][l], lens[0], tm)
        if not last:
            x[0] = mixers[0](x[0], l + 1)
        idx, gate, act = q[1]
        w1 = _peer_gelu(gate, act, tm)
        y1 = _peer_v(w1, idx, v_packed[l], none_yet if last else x[0])
        if not last:
            h2, idx, gate = retrieve(0, l + 1, w1)
            q[0] = (idx, gate, _peer_u(h2, idx, u_packed[l + 1], y1))
        x[1] = _peer_residual(x[1], y1, mods[1][l], lens[1], tm)

    y_prompt = _final_norm(x[0], final_norm[None], tm).reshape(bp, lp, D_MODEL)
    y_sample = _final_norm(x[1], final_norm[None], tm).reshape(bs, ls, D_MODEL)
    return (y_prompt, y_sample, jnp.stack(ckvs, axis=1), jnp.stack(kpes, axis=1), jnp.stack(states, axis=1))
```

```python
import functools

import numpy as np
import jax
import jax.numpy as jnp
from jax import lax
from jax.experimental import pallas as pl
from jax.experimental.pallas import tpu as pltpu
from jax.experimental.pallas import tpu_sc as plsc

F32 = jnp.float32
BF16 = jnp.bfloat16
HIGHEST = lax.Precision.HIGHEST

D_MODEL = 1024
EPS = 1e-6
GRID_W = 64
HG_HEADS = 8
HG_D = 64
HG_CHUNK = 32
HG_W = HG_HEADS * HG_D
FORGET_FLOOR = 1e-6
MLA_HEADS = 8
MLA_Q_LORA = 384
MLA_KV_LORA = 256
MLA_NOPE = 64
MLA_ROPE = 32
MLA_V = 64
MLA_SCALE = (MLA_NOPE + MLA_ROPE) ** -0.5
ROPE_AXIS_DIM = MLA_ROPE // 2
ROPE_BASE = 10000.0
PEER_HEADS = 8
PEER_NKEYS = 128
PEER_DQ = 256
PEER_TOPK = 16

LANES = 128
VMEM_LIMIT = 48 * 1024 * 1024

IN_HG = 5 * HG_W
IN_CQ0 = IN_HG
IN_CKV0 = IN_CQ0 + MLA_Q_LORA
IN_KPE0 = IN_CKV0 + MLA_KV_LORA
IN_PAD = IN_KPE0 + LANES


def _cparams(sem):
    return pltpu.CompilerParams(dimension_semantics=sem, vmem_limit_bytes=VMEM_LIMIT)


def _split_bf16(a):
    hi = a.astype(BF16)
    lo = (a - hi.astype(F32)).astype(BF16)
    return hi, lo


def _dot3(a, b, dims=(((1,), (0,)), ((), ()))):
    ah, al = _split_bf16(a)
    bh, bl = _split_bf16(b)
    d = functools.partial(lax.dot_general, dimension_numbers=dims, preferred_element_type=F32)
    return d(ah, bh) + d(al, bh) + d(ah, bl)


def _dot_sel(a, sel_bf):
    ah, al = _split_bf16(a)
    return (jnp.dot(ah, sel_bf, preferred_element_type=F32)
            + jnp.dot(al, sel_bf, preferred_element_type=F32))


def _rms_rows(x, g):
    return x * lax.rsqrt(jnp.mean(x * x, axis=-1, keepdims=True) + EPS) * g


def _ada_kernel(cond_ref, w_ref, b_ref, o_ref):
    cnd = cond_ref[...]
    s = cnd * jax.nn.sigmoid(cnd)
    o_ref[0] = jnp.dot(s, w_ref[0], precision=HIGHEST, preferred_element_type=F32) + b_ref[0]


def _ada_mod(cond8, w_ada, b_ada, l):
    depth, d, n6 = w_ada.shape
    tn = 1536
    return pl.pallas_call(
        _ada_kernel,
        grid=(n6 // tn,),
        in_specs=[pl.BlockSpec((8, d), lambda j: (0, 0)),
                  pl.BlockSpec((1, d, tn), lambda j: (l, 0, j)),
                  pl.BlockSpec((1, 1, tn), lambda j: (l, 0, j))],
        out_specs=pl.BlockSpec((1, 8, tn), lambda j: (0, 0, j)),
        out_shape=jax.ShapeDtypeStruct((1, 8, n6), F32),
        compiler_params=_cparams(("parallel",)),
        name="ada_mod",
    )(cond8, w_ada, b_ada.reshape(depth, 1, n6))[0]


def _inproj_kernel(x_ref, y_ref, modp_ref, g_ref, mod_ref, w_ref, gq_ref, gkv_ref,
                   xn_ref, hg_ref, cq_ref, ckv_ref, kpe_ref):
    x = x_ref[...] + modp_ref[0, :, 5 * D_MODEL:6 * D_MODEL] * y_ref[...]
    xn_ref[...] = x
    shift = mod_ref[0, :, 0:D_MODEL]
    scale = mod_ref[0, :, D_MODEL:2 * D_MODEL]
    h = (_rms_rows(x, g_ref[...]) * (1.0 + scale) + shift).astype(BF16)
    hg_ref[...] = jnp.dot(h, w_ref[:, 0:IN_HG], preferred_element_type=F32)
    cq = jnp.dot(h, w_ref[:, IN_CQ0:IN_CKV0], preferred_element_type=F32)
    cq_ref[...] = _rms_rows(cq, gq_ref[...]).astype(BF16)
    ckv = jnp.dot(h, w_ref[:, IN_CKV0:IN_KPE0], preferred_element_type=F32)
    ckv_ref[...] = _rms_rows(ckv, gkv_ref[...])
    kpe_ref[...] = jnp.dot(h, w_ref[:, IN_KPE0:IN_PAD], preferred_element_type=F32)


def _inproj(x, y, mod_prev, g, mod, w_pad, gq, gkv, seq_len, tm):
    n = x.shape[0]
    per_seq = seq_len // tm
    n_cond = mod.shape[0]
    cond_of = (lambda i: (0, 0, 0)) if n_cond == 1 else (lambda i: (i // per_seq, 0, 0))
    row = lambda i: (i, 0)
    const = lambda i: (0, 0)
    return pl.pallas_call(
        _inproj_kernel,
        grid=(n // tm,),
        in_specs=[pl.BlockSpec((tm, D_MODEL), row),
                  pl.BlockSpec((tm, D_MODEL), row),
                  pl.BlockSpec((1, 1, 6 * D_MODEL), cond_of),
                  pl.BlockSpec((1, D_MODEL), const),
                  pl.BlockSpec((1, 1, 6 * D_MODEL), cond_of),
                  pl.BlockSpec((D_MODEL, IN_PAD), const),
                  pl.BlockSpec((1, MLA_Q_LORA), const),
                  pl.BlockSpec((1, MLA_KV_LORA), const)],
        out_specs=[pl.BlockSpec((tm, D_MODEL), row),
                   pl.BlockSpec((tm, IN_HG), row),
                   pl.BlockSpec((tm, MLA_Q_LORA), row),
                   pl.BlockSpec((tm, MLA_KV_LORA), row),
                   pl.BlockSpec((tm, LANES), row)],
        out_shape=[jax.ShapeDtypeStruct((n, D_MODEL), F32),
                   jax.ShapeDtypeStruct((n, IN_HG), F32),
                   jax.ShapeDtypeStruct((n, MLA_Q_LORA), BF16),
                   jax.ShapeDtypeStruct((n, MLA_KV_LORA), F32),
                   jax.ShapeDtypeStruct((n, LANES), F32)],
        compiler_params=_cparams(("parallel",)),
        name="in_proj",
    )(x, y, mod_prev, g, mod, w_pad, gq, gkv)


HG_SUB = 8
HG_UNROLL = 8


def _hgrn_kernel(q_ref, ff_ref, fb_ref, v_ref, gt_ref, lbf_ref, lbb_ref, gn_ref, s0_ref,
                 out_ref, sfin_ref, qt_ref, acc_ref, u_ref, d_ref, *, seq_len):
    C, SB = HG_CHUNK, HG_SUB
    n_sub = C // SB
    n_chunks = seq_len // C
    r2 = lax.broadcasted_iota(jnp.int32, (LANES, LANES), 0) // HG_D
    c2 = lax.broadcasted_iota(jnp.int32, (LANES, LANES), 1) // HG_D
    same_head = r2 == c2
    sel_bf = same_head.astype(BF16)
    ti = lax.broadcasted_iota(jnp.int32, (C, C), 0)
    ui = lax.broadcasted_iota(jnp.int32, (C, C), 1)
    tri = ((ui <= ti).astype(F32), (ui >= ti).astype(F32))
    s3 = lax.broadcasted_iota(jnp.int32, (SB, SB, 1), 0)
    t3 = lax.broadcasted_iota(jnp.int32, (SB, SB, 1), 1)
    diag_mask = (t3 >= s3, t3 <= s3)
    key_row = lax.broadcasted_iota(jnp.int32, (C, 1), 0)
    lane1 = lax.broadcasted_iota(jnp.int32, (1, LANES), 1)
    head_lane = ((lane1 < HG_D).astype(F32), (lane1 >= HG_D).astype(F32))
    ones_c = jnp.ones((C, LANES), F32)
    tdims = (((0,), (0,)), ((), ()))
    nt = (((1,), (1,)), ((), ()))
    z_refs, lb_refs = (ff_ref, fb_ref), (lbf_ref, lbb_ref)

    off_blocks = (tuple(range(1, n_sub)), tuple(range(0, n_sub - 1)))
    n_off = n_sub - 1
    sr = lax.broadcasted_iota(jnp.int32, (n_off * 2 * SB, n_off * C), 0) // (2 * SB)
    sc = lax.broadcasted_iota(jnp.int32, (n_off * 2 * SB, n_off * C), 1) // C
    own_block = sr == sc

    units = [(ci, d) for ci in range(HG_UNROLL) for d in range(2)]

    def prep_body(it, carry):
        cs = [it * HG_UNROLL + ci for ci in range(HG_UNROLL)]
        rows = [pl.ds(pl.multiple_of(c * C, C), C) for c in cs]
        q = [q_ref[r, :] * (HG_D ** -0.5) for r in rows]
        v = [v_ref[r, :] for r in rows]
        v_bf = [x.astype(BF16) for x in v]
        v_rep = [jnp.concatenate([x] * n_off, axis=0) for x in v_bf]
        gs, ks = [], []
        for ci, d in units:
            lb = lb_refs[d][...]
            f = jnp.maximum(lb + (1.0 - lb) * jax.nn.sigmoid(z_refs[d][rows[ci], :]), FORGET_FLOOR)
            gs.append(jnp.log(f))
            ks.append(1.0 - f)
        bs = [jnp.dot(tri[d], gs[u], precision=HIGHEST, preferred_element_type=F32) for u, (ci, d) in enumerate(units)]
        totals = [lax.dot_general(g, ones_c, tdims, precision=HIGHEST, preferred_element_type=F32) for g in gs]
        incs, sums, scores = [], [], []
        for u, (ci, d) in enumerate(units):
            b_end = bs[u][C - 1:C, :] if d == 0 else bs[u][0:1, :]
            kk = (ks[u] * jnp.exp(b_end - bs[u])).astype(BF16)
            incs.append(lax.dot_general(kk, v_bf[ci], tdims, preferred_element_type=F32))
        for u, (ci, d) in enumerate(units):
            blocks = []
            for blk in range(n_sub):
                sl = slice(blk * SB, (blk + 1) * SB)
                bi = bs[u][sl]
                diff = bi[None, :, :] - bi[:, None, :]
                e = jnp.where(diag_mask[d], jnp.exp(jnp.where(diag_mask[d], diff, 0.0)), 0.0)
                blocks.append((e * q[ci][sl][None, :, :] * ks[u][sl][:, None, :]).reshape(SB * SB, LANES))
            sums.append(jnp.dot(jnp.concatenate(blocks, axis=0).astype(BF16), sel_bf, preferred_element_type=F32))
        for u, (ci, d) in enumerate(units):
            lhs, khs = [], []
            for blk in off_blocks[d]:
                sl = slice(blk * SB, (blk + 1) * SB)
                if d == 0:
                    r = bs[u][blk * SB - 1:blk * SB, :]
                    keys = key_row < blk * SB
                else:
                    r = bs[u][(blk + 1) * SB:(blk + 1) * SB + 1, :]
                    keys = key_row >= (blk + 1) * SB
                qh = q[ci][sl] * jnp.exp(bs[u][sl] - r)
                lhs += [qh * head_lane[0], qh * head_lane[1]]
                khs.append(jnp.where(keys, ks[u] * jnp.exp(jnp.minimum(r - bs[u], 0.0)), 0.0))
            scores.append(lax.dot_general(jnp.concatenate(lhs, axis=0).astype(BF16),
                                          jnp.concatenate(khs, axis=0).astype(BF16), nt,
                                          preferred_element_type=F32))
        o2s = [jnp.dot(jnp.where(own_block, scores[u], 0.0).astype(BF16), v_rep[ci], preferred_element_type=F32)
               for u, (ci, d) in enumerate(units)]
        for u, (ci, d) in enumerate(units):
            outs = []
            for blk in range(n_sub):
                sl = slice(blk * SB, (blk + 1) * SB)
                a_blk = sums[u][blk * SB * SB:(blk + 1) * SB * SB].reshape(SB, SB, LANES)
                o_blk = jnp.sum(a_blk * v[ci][sl][:, None, :], axis=0)
                if blk in off_blocks[d]:
                    j = off_blocks[d].index(blk) * 2 * SB
                    o_blk = o_blk + o2s[u][j:j + SB] * head_lane[0] + o2s[u][j + SB:j + 2 * SB] * head_lane[1]
                outs.append(o_blk)
            qt_ref[d, rows[ci], :] = (q[ci] * jnp.exp(bs[u])).astype(BF16)
            u_ref[d, cs[ci]] = jnp.where(same_head, incs[u], 0.0)
            d_ref[d, cs[ci]] = jnp.exp(totals[u])
            acc_ref[d, rows[ci], :] = jnp.concatenate(outs, axis=0)
        return carry

    lax.fori_loop(0, n_chunks // HG_UNROLL, prep_body, 0)

    def block_diag(s_a, s_b):
        z = jnp.zeros((HG_D, HG_D), F32)
        return jnp.concatenate([jnp.concatenate([s_a, z], axis=1),
                                jnp.concatenate([z, s_b], axis=1)], axis=0)

    def scan_body(i, carry):
        s_f, s_b = carry
        cb = n_chunks - 1 - i
        inc_f, inc_b = u_ref[0, i], u_ref[1, cb]
        u_ref[0, i] = s_f
        u_ref[1, cb] = s_b
        return d_ref[0, i] * s_f + inc_f, d_ref[1, cb] * s_b + inc_b

    s_f, s_b = lax.fori_loop(0, n_chunks, scan_body,
                             (block_diag(s0_ref[0, 0, 0], s0_ref[0, 0, 1]),
                              block_diag(s0_ref[0, 1, 0], s0_ref[0, 1, 1])))
    sfin_ref[0, 0, 0] = s_f[0:HG_D, 0:HG_D]
    sfin_ref[0, 0, 1] = s_f[HG_D:, HG_D:]
    sfin_ref[0, 1, 0] = s_b[0:HG_D, 0:HG_D]
    sfin_ref[0, 1, 1] = s_b[HG_D:, HG_D:]

    bdims = (((2,), (1,)), ((0,), (0,)))
    o = acc_ref[0] + acc_ref[1]
    for d in range(2):
        inter = lax.dot_general(qt_ref[d].reshape(n_chunks, C, LANES), u_ref[d].astype(BF16), bdims,
                                preferred_element_type=F32)
        o = o + inter.reshape(seq_len, LANES)
    ms = _dot_sel(o * o, sel_bf) * (1.0 / HG_D)
    gt = gt_ref[...]
    out_ref[...] = (o * lax.rsqrt(ms + EPS) * gn_ref[...] * (gt * jax.nn.sigmoid(gt))).astype(BF16)


def _hgrn(hg, lb_f, lb_b, gn2, s0, batch, seq_len):
    n = hg.shape[0]
    pairs = HG_W // LANES
    part = lambda p: pl.BlockSpec((seq_len, LANES), lambda b, j, p=p: (b, p * pairs + j))
    lane_row = pl.BlockSpec((1, LANES), lambda b, j: (0, j))
    st = pl.BlockSpec((1, 2, 2, HG_D, HG_D), lambda b, j: (b, 0, j, 0, 0))
    return pl.pallas_call(
        functools.partial(_hgrn_kernel, seq_len=seq_len),
        grid=(batch, pairs),
        in_specs=[part(0), part(1), part(2), part(3), part(4), lane_row, lane_row,
                  pl.BlockSpec((1, LANES), lambda b, j: (0, 0)), st],
        out_specs=[pl.BlockSpec((seq_len, LANES), lambda b, j: (b, j)), st],
        out_shape=[jax.ShapeDtypeStruct((n, HG_W), BF16),
                   jax.ShapeDtypeStruct((batch, 2, HG_HEADS, HG_D, HG_D), F32)],
        scratch_shapes=[pltpu.VMEM((2, seq_len, LANES), BF16),
                        pltpu.VMEM((2, seq_len, LANES), F32),
                        pltpu.VMEM((2, seq_len // HG_CHUNK, LANES, LANES), F32),
                        pltpu.VMEM((2, seq_len // HG_CHUNK, LANES, LANES), F32)],
        compiler_params=_cparams(("parallel", "parallel")),
        name="hgrn2",
    )(hg, hg, hg, hg, hg, lb_f, lb_b, gn2, s0)


def _rope_lanes(x, cos, sin):
    return x * cos + pltpu.roll(x, 96, 1) * sin


def _qproj_kernel(cq_ref, w_ref, cos_ref, sin_ref, q_ref):
    cq = cq_ref[...]
    for h in range(MLA_HEADS):
        qh = jnp.dot(cq, w_ref[h], preferred_element_type=F32)
        q_ref[h] = (_rope_lanes(qh, cos_ref[...], sin_ref[...]) * MLA_SCALE).astype(BF16)


def _kvproj_kernel(ckv_ref, kpe_ref, w_ref, cos_ref, sin_ref, k_ref, kv_ref):
    ckv = ckv_ref[...].astype(BF16)
    kpe = _rope_lanes(kpe_ref[...], cos_ref[...], sin_ref[...])
    nope = lax.broadcasted_iota(jnp.int32, kpe.shape, 1) < MLA_NOPE
    for h in range(MLA_HEADS):
        kvh = jnp.dot(ckv, w_ref[h], preferred_element_type=F32)
        kv_ref[h] = kvh.astype(BF16)
        k_ref[h] = jnp.where(nope, kvh, kpe).astype(BF16)


def _qproj(cqn, wq, cos, sin, tm):
    n = cqn.shape[0]
    pos_blocks = cos.shape[0] // tm
    return pl.pallas_call(
        _qproj_kernel,
        grid=(n // tm,),
        in_specs=[pl.BlockSpec((tm, MLA_Q_LORA), lambda i: (i, 0)),
                  pl.BlockSpec((MLA_HEADS, MLA_Q_LORA, LANES), lambda i: (0, 0, 0)),
                  pl.BlockSpec((tm, LANES), lambda i: (i % pos_blocks, 0)),
                  pl.BlockSpec((tm, LANES), lambda i: (i % pos_blocks, 0))],
        out_specs=pl.BlockSpec((MLA_HEADS, tm, LANES), lambda i: (0, i, 0)),
        out_shape=jax.ShapeDtypeStruct((MLA_HEADS, n, LANES), BF16),
        compiler_params=_cparams(("parallel",)),
        name="mla_q_proj",
    )(cqn, wq, cos, sin)


def _kvproj(ckvn, kpe128, wkv, cos, sin, tm):
    n = ckvn.shape[0]
    pos_blocks = cos.shape[0] // tm
    hm = pl.BlockSpec((MLA_HEADS, tm, LANES), lambda i: (0, i, 0))
    return pl.pallas_call(
        _kvproj_kernel,
        grid=(n // tm,),
        in_specs=[pl.BlockSpec((tm, MLA_KV_LORA), lambda i: (i, 0)),
                  pl.BlockSpec((tm, LANES), lambda i: (i, 0)),
                  pl.BlockSpec((MLA_HEADS, MLA_KV_LORA, LANES), lambda i: (0, 0, 0)),
                  pl.BlockSpec((tm, LANES), lambda i: (i % pos_blocks, 0)),
                  pl.BlockSpec((tm, LANES), lambda i: (i % pos_blocks, 0))],
        out_specs=[hm, hm],
        out_shape=[jax.ShapeDtypeStruct((MLA_HEADS, n, LANES), BF16)] * 2,
        compiler_params=_cparams(("parallel",)),
        name="mla_kv_proj",
    )(ckvn, kpe128, wkv, cos, sin)


def _attn_kernel(q_ref, k_ref, kv_ref, o_ref):
    low = lax.broadcasted_iota(jnp.int32, (q_ref.shape[1], LANES), 1) < MLA_V
    nt = (((1,), (1,)), ((), ()))
    outs = []
    for h in range(MLA_HEADS):
        s = lax.dot_general(q_ref[h], k_ref[h], nt, preferred_element_type=F32)
        p = jnp.exp(s - jnp.max(s, axis=-1, keepdims=True))
        denom = jnp.sum(p, axis=-1, keepdims=True)
        o = jnp.dot(p.astype(BF16), kv_ref[h], preferred_element_type=F32) / denom
        outs.append(o)
    for j in range(MLA_HEADS // 2):
        pair = jnp.where(low, pltpu.roll(outs[2 * j], MLA_V, 1), outs[2 * j + 1])
        o_ref[:, j * LANES:(j + 1) * LANES] = pair.astype(BF16)


def _attention(q128, k128, kv128, batch, lq, lk, tq):
    n = q128.shape[1]
    qb = lq // tq
    kspec = pl.BlockSpec((MLA_HEADS, lk, LANES), lambda b, i: (0, b, 0))
    return pl.pallas_call(
        _attn_kernel,
        grid=(batch, qb),
        in_specs=[pl.BlockSpec((MLA_HEADS, tq, LANES), lambda b, i: (0, b * qb + i, 0)), kspec, kspec],
        out_specs=pl.BlockSpec((tq, MLA_HEADS * MLA_V), lambda b, i: (b * qb + i, 0)),
        out_shape=jax.ShapeDtypeStruct((n, MLA_HEADS * MLA_V), BF16),
        compiler_params=_cparams(("parallel", "parallel")),
        name="mla_attention",
    )(q128, k128, kv128)


def _outproj_kernel(x_ref, hg_ref, at_ref, w_ref, mod_ref, o_ref):
    gate = mod_ref[0, :, 2 * D_MODEL:3 * D_MODEL]
    mix = (jnp.dot(hg_ref[...], w_ref[0:HG_W, :], preferred_element_type=F32)
           + jnp.dot(at_ref[...], w_ref[HG_W:, :], preferred_element_type=F32))
    o_ref[...] = x_ref[...] + gate * mix


def _outproj(x, hg_out, attn, w_out, mod, seq_len, tm):
    n = x.shape[0]
    per_seq = seq_len // tm
    n_cond = mod.shape[0]
    cond_of = (lambda i: (0, 0, 0)) if n_cond == 1 else (lambda i: (i // per_seq, 0, 0))
    row = lambda i: (i, 0)
    return pl.pallas_call(
        _outproj_kernel,
        grid=(n // tm,),
        in_specs=[pl.BlockSpec((tm, D_MODEL), row),
                  pl.BlockSpec((tm, HG_W), row),
                  pl.BlockSpec((tm, MLA_HEADS * MLA_V), row),
                  pl.BlockSpec((2 * HG_W, D_MODEL), lambda i: (0, 0)),
                  pl.BlockSpec((1, 1, 6 * D_MODEL), cond_of)],
        out_specs=pl.BlockSpec((tm, D_MODEL), row),
        out_shape=jax.ShapeDtypeStruct((n, D_MODEL), F32),
        compiler_params=_cparams(("parallel",)),
        name="out_proj",
    )(x, hg_out, attn, w_out, mod)


_CAND = [(a, b) for a in range(PEER_TOPK) for b in range(PEER_TOPK) if (a + 1) * (b + 1) <= PEER_TOPK]
_CAND_PAD = -(-len(_CAND) // 8) * 8


def _topk_rows(work, k):
    rows = lax.broadcasted_iota(jnp.int32, work.shape, 0).astype(F32)
    n_rows = float(work.shape[0])
    vals, idxs = [], []
    for _ in range(k):
        m = jnp.max(work, axis=0, keepdims=True)
        sel = jnp.min(jnp.where(work == m, rows, n_rows), axis=0, keepdims=True)
        vals.append(m)
        idxs.append(sel)
        work = jnp.where(rows == sel, -jnp.inf, work)
    return jnp.concatenate(vals, axis=0), jnp.concatenate(idxs, axis=0)


def _peerq_kernel(x_ref, g_ref, mod_ref, w_ref, k1_ref, k2_ref, after_ref, h2_ref, idx_ref, gate_ref):
    del after_ref
    idx_rows, gate_rows = [], []
    x = x_ref[...]
    shift = mod_ref[0, :, 3 * D_MODEL:4 * D_MODEL]
    scale = mod_ref[0, :, 4 * D_MODEL:5 * D_MODEL]
    h2 = _rms_rows(x, g_ref[...]) * (1.0 + scale) + shift
    h2b = h2.astype(BF16)
    bits = lax.bitcast_convert_type(h2b.astype(F32), jnp.uint32)
    h2_ref[...] = (bits[:, :PACK_W] >> 16) | bits[:, PACK_W:]
    qp = jnp.dot(h2b, w_ref[...], preferred_element_type=F32)
    nt = (((1,), (1,)), ((), ()))
    half = PEER_DQ // 2
    for h in range(PEER_HEADS):
        q1 = qp[:, h * PEER_DQ:h * PEER_DQ + half]
        q2 = qp[:, h * PEER_DQ + half:(h + 1) * PEER_DQ]
        s1 = _dot3(k1_ref[h], q1, nt)
        s2 = _dot3(k2_ref[h], q2, nt)
        v1, i1 = _topk_rows(s1, PEER_TOPK)
        v2, i2 = _topk_rows(s2, PEER_TOPK)
        cs = [v1[a:a + 1, :] + v2[b:b + 1, :] for a, b in _CAND]
        ci = [i1[a:a + 1, :] * PEER_NKEYS + i2[b:b + 1, :] for a, b in _CAND]
        n_pad = _CAND_PAD - len(_CAND)
        if n_pad:
            cs.append(jnp.full((n_pad, x.shape[0]), -jnp.inf, F32))
            ci.append(jnp.zeros((n_pad, x.shape[0]), F32))
        cand_s = jnp.concatenate(cs, axis=0)
        cand_i = jnp.concatenate(ci, axis=0)
        best_s, best_pos = _topk_rows(cand_s, PEER_TOPK)
        rows = lax.broadcasted_iota(jnp.int32, cand_s.shape, 0).astype(F32)
        picked = [jnp.sum(jnp.where(rows == best_pos[r:r + 1, :], cand_i, 0.0), axis=0, keepdims=True)
                  for r in range(PEER_TOPK)]
        idx_rows.extend(picked)
        ex = jnp.exp(best_s - best_s[0:1, :])
        gate_rows.append(ex / jnp.sum(ex, axis=0, keepdims=True))
    idx_ref[...] = jnp.concatenate(idx_rows, axis=0).T.astype(jnp.int32)
    gate_ref[...] = jnp.concatenate(gate_rows, axis=0).T


def _peerq(x, g, mod, wq, k1, k2, seq_len, tm, after):
    n = x.shape[0]
    per_seq = seq_len // tm
    n_cond = mod.shape[0]
    cond_of = (lambda i: (0, 0, 0)) if n_cond == 1 else (lambda i: (i // per_seq, 0, 0))
    kspec = pl.BlockSpec((PEER_HEADS, PEER_NKEYS, PEER_DQ // 2), lambda i: (0, 0, 0))
    tk = PEER_HEADS * PEER_TOPK
    return pl.pallas_call(
        _peerq_kernel,
        grid=(n // tm,),
        in_specs=[pl.BlockSpec((tm, D_MODEL), lambda i: (i, 0)),
                  pl.BlockSpec((1, D_MODEL), lambda i: (0, 0)),
                  pl.BlockSpec((1, 1, 6 * D_MODEL), cond_of),
                  pl.BlockSpec((D_MODEL, PEER_HEADS * PEER_DQ), lambda i: (0, 0)),
                  kspec, kspec,
                  pl.BlockSpec(memory_space=pl.ANY)],
        out_specs=[pl.BlockSpec((tm, D_MODEL // 2), lambda i: (i, 0)),
                   pl.BlockSpec((tm, tk), lambda i: (i, 0)),
                   pl.BlockSpec((tm, tk), lambda i: (i, 0))],
        out_shape=[jax.ShapeDtypeStruct((n, D_MODEL // 2), jnp.uint32),
                   jax.ShapeDtypeStruct((n, tk), jnp.int32),
                   jax.ShapeDtypeStruct((n, tk), F32)],
        compiler_params=_cparams(("parallel",)),
        name="peer_retrieve",
    )(x, g, mod, wq, k1, k2, after)


SC_CORES = 2
SC_SUBCORES = 16
SC_LANES = 16
SC_TILES = SC_CORES * SC_SUBCORES
SC_NBUF_U = 4
SC_NBUF_V = 4
ITEM_HEADS = 2
ITEM_ROWS = ITEM_HEADS * PEER_TOPK
ITEMS_PER_TOKEN = PEER_HEADS // ITEM_HEADS
PEER_TK = PEER_HEADS * PEER_TOPK
D_VREGS = D_MODEL // SC_LANES
BF_RUN = 4


PACK_W = D_MODEL // 2


def _pack_kernel(t_ref, o_ref):
    bits = lax.bitcast_convert_type(t_ref[...].astype(BF16).astype(F32), jnp.uint32)
    o_ref[...] = (bits[:, :PACK_W] >> 16) | bits[:, PACK_W:]


def _pack_bf16_pairs(tables, l, tm=1024):
    n = tables.shape[1]
    return pl.pallas_call(
        _pack_kernel,
        grid=(n // tm,),
        in_specs=[pl.BlockSpec((None, tm, D_MODEL), lambda i: (l, i, 0))],
        out_specs=pl.BlockSpec((tm, PACK_W), lambda i: (i, 0)),
        out_shape=jax.ShapeDtypeStruct((n, PACK_W), jnp.uint32),
        compiler_params=_cparams(("parallel",)),
        name="pack_expert_table",
    )(tables)


def _unpack_pair(word):
    lo = lax.bitcast_convert_type(word << 16, F32)
    hi = lax.bitcast_convert_type(word & jnp.uint32(0xFFFF0000), F32)
    return lo, hi


def _sc_mesh():
    return plsc.VectorSubcoreMesh(core_axis_name="c", subcore_axis_name="s")


def _sc_params():
    return pltpu.CompilerParams(needs_layout_passes=False)


def _sc_token_base(tokens_per_tile):
    return (lax.axis_index("s") * SC_CORES + lax.axis_index("c")) * tokens_per_tile


def _peer_u_body(h2_hbm, idx_hbm, u_hbm, after_hbm, act_hbm, idx_v, act_v, x_v, rows_v, sem_x, sem_g, *, tpt,
                 nbuf):
    del after_hbm
    base = _sc_token_base(tpt)
    n_items = tpt * ITEMS_PER_TOKEN
    pltpu.sync_copy(idx_hbm.at[pl.ds(base, tpt)], idx_v)

    def gather(i, slot):
        t, g = i // ITEMS_PER_TOKEN, i % ITEMS_PER_TOKEN
        return pltpu.make_async_copy(u_hbm.at[idx_v.at[t, pl.ds(g * ITEM_ROWS, ITEM_ROWS)]],
                                     rows_v.at[slot], sem_g.at[slot])

    def x_copy(t):
        return pltpu.make_async_copy(h2_hbm.at[base + t], x_v.at[t % 2], sem_x.at[t % 2])

    x_copy(0).start()
    for b in range(nbuf - 1):
        gather(b, b).start()
    lane = lax.iota(jnp.int32, SC_LANES)
    zero = jnp.zeros((SC_LANES,), F32)

    @pl.loop(0, n_items)
    def _(i):
        for b in (i % nbuf,):
            t, g = i // ITEMS_PER_TOKEN, i % ITEMS_PER_TOKEN

            @pl.when(g == 0)
            def _():
                x_copy(t).wait()

                @pl.when(t + 1 < tpt)
                def _():
                    x_copy(t + 1).start()

            @pl.when(i + nbuf - 1 < n_items)
            def _():
                gather(i + nbuf - 1, (b + nbuf - 1) % nbuf).start()

            gather(i, b).wait()
            xs = t % 2

            @pl.loop(0, ITEM_HEADS)
            def _(sub):
                r0 = sub * PEER_TOPK

                def cbody(cg, accs):
                    parts = [None] * PEER_TOPK
                    for cc in range(BF_RUN):
                        col = pl.ds((cg * BF_RUN + cc) * SC_LANES, SC_LANES)
                        xb = plsc.bitcast(x_v[xs, col], BF16)
                        for j in range(PEER_TOPK):
                            p = xb * plsc.bitcast(rows_v[b, r0 + j, col], BF16)
                            parts[j] = p if cc == 0 else parts[j] + p
                    out = []
                    for j, a in enumerate(accs):
                        lo, hi = _unpack_pair(plsc.bitcast(parts[j], jnp.uint32))
                        out.append(a + lo + hi)
                    return tuple(out)

                accs = plsc.parallel_loop(0, PACK_W // SC_LANES // BF_RUN, carry=(zero,) * PEER_TOPK)(cbody)
                vecs = list(accs)
                s = SC_LANES // 2
                while s >= 1:
                    swap = lane ^ s
                    pick_first = (lane & s) == 0
                    folded = [v + v.at[swap].get(mode="promise_in_bounds") for v in vecs]
                    vecs = [jnp.where(pick_first, folded[j], folded[j + s]) for j in range(s)]
                    s //= 2
                act_v[t, pl.ds(g * ITEM_ROWS + r0, PEER_TOPK)] = vecs[0]

    pltpu.sync_copy(act_v, act_hbm.at[pl.ds(base, tpt)])


def _peer_v_body(w_hbm, idx_hbm, v_hbm, after_hbm, y_hbm, idx_v, w_v, y_v, rows_v, sem_y, sem_g, *, tpt,
                 nbuf):
    del after_hbm
    base = _sc_token_base(tpt)
    n_items = tpt * ITEMS_PER_TOKEN
    pltpu.sync_copy(idx_hbm.at[pl.ds(base, tpt)], idx_v)
    pltpu.sync_copy(w_hbm.at[pl.ds(base, tpt)], w_v)

    def gather(i, slot):
        t, g = i // ITEMS_PER_TOKEN, i % ITEMS_PER_TOKEN
        return pltpu.make_async_copy(v_hbm.at[idx_v.at[t, pl.ds(g * ITEM_ROWS, ITEM_ROWS)]],
                                     rows_v.at[slot], sem_g.at[slot])

    def y_copy(t):
        return pltpu.make_async_copy(y_v.at[t % 2], y_hbm.at[base + t], sem_y.at[t % 2])

    for b in range(nbuf - 1):
        gather(b, b).start()
    lane = lax.iota(jnp.int32, SC_LANES)
    zero = jnp.zeros((SC_LANES,), F32)

    @pl.loop(0, n_items)
    def _(i):
        for b in (i % nbuf,):
            t, g = i // ITEMS_PER_TOKEN, i % ITEMS_PER_TOKEN
            ys = t % 2

            @pl.when(i + nbuf - 1 < n_items)
            def _():
                gather(i + nbuf - 1, (b + nbuf - 1) % nbuf).start()

            @pl.when(g == 0)
            def _():
                @pl.when(t >= 2)
                def _():
                    y_copy(t - 2).wait()

                @plsc.parallel_loop(0, D_VREGS, unroll=8)
                def _(c):
                    y_v[ys, pl.ds(c * SC_LANES, SC_LANES)] = zero

            gather(i, b).wait()
            for sub in range(ITEM_HEADS):
                r0 = sub * PEER_TOPK
                wv = w_v[t, pl.ds(g * ITEM_ROWS + r0, PEER_TOPK)]
                ws = [plsc.bitcast(wv.at[jnp.full((SC_LANES,), j, jnp.int32)].get(mode="promise_in_bounds"), BF16)
                      for j in range(PEER_TOPK)]

                @plsc.parallel_loop(0, PACK_W // SC_LANES)
                def _(c, r0=r0, ws=ws):
                    col = pl.ds(c * SC_LANES, SC_LANES)
                    col_hi = pl.ds(PACK_W + c * SC_LANES, SC_LANES)
                    terms = [ws[j] * plsc.bitcast(rows_v[b, r0 + j, col], BF16) for j in range(PEER_TOPK)]
                    while len(terms) > 1:
                        terms = [terms[k] + terms[k + 1] for k in range(0, len(terms), 2)]
                    lo, hi = _unpack_pair(plsc.bitcast(terms[0], jnp.uint32))
                    y_v[ys, col] = y_v[ys, col] + lo
                    y_v[ys, col_hi] = y_v[ys, col_hi] + hi

            @pl.when(g == ITEMS_PER_TOKEN - 1)
            def _():
                y_copy(t).start()

    y_copy(tpt - 2).wait()
    y_copy(tpt - 1).wait()


def _sc_scratch(tpt, side_dtype, token_buf, nbuf):
    return [pltpu.VMEM((tpt, PEER_TK), jnp.int32),
            pltpu.VMEM((tpt, PEER_TK), side_dtype),
            token_buf,
            pltpu.VMEM((nbuf, ITEM_ROWS, PACK_W), jnp.uint32),
            pltpu.SemaphoreType.DMA((2,)),
            pltpu.SemaphoreType.DMA((nbuf,))]


def _peer_u(h2, idx, u_all, after):
    n = h2.shape[0]
    tpt = n // SC_TILES
    return pl.kernel(
        functools.partial(_peer_u_body, tpt=tpt, nbuf=SC_NBUF_U),
        out_type=jax.ShapeDtypeStruct((n, PEER_TK), F32),
        mesh=_sc_mesh(),
        scratch_types=_sc_scratch(tpt, F32, pltpu.VMEM((2, PACK_W), jnp.uint32), SC_NBUF_U),
        compiler_params=_sc_params(),
        name="peer_u_sc",
    )(h2, idx, u_all, after)


def _peer_v(w, idx, v_all, after):
    n = w.shape[0]
    tpt = n // SC_TILES
    return pl.kernel(
        functools.partial(_peer_v_body, tpt=tpt, nbuf=SC_NBUF_V),
        out_type=jax.ShapeDtypeStruct((n, D_MODEL), F32),
        mesh=_sc_mesh(),
        scratch_types=_sc_scratch(tpt, jnp.int32, pltpu.VMEM((2, D_MODEL), F32), SC_NBUF_V),
        compiler_params=_sc_params(),
        name="peer_v_sc",
    )(w, idx, v_all, after)


def _peer_gelu_kernel(gate_ref, act_ref, w_ref):
    act = act_ref[...]
    w = gate_ref[...] * (0.5 * act * (1.0 + lax.erf(act * (2.0 ** -0.5))))
    bits = lax.bitcast_convert_type(w.astype(BF16).astype(F32), jnp.uint32)
    w_ref[...] = lax.bitcast_convert_type(bits | (bits >> 16), jnp.int32)


def _peer_gelu(gate, act, tm):
    n = gate.shape[0]
    spec = pl.BlockSpec((tm, PEER_TK), lambda i: (i, 0))
    return pl.pallas_call(
        _peer_gelu_kernel,
        grid=(n // tm,),
        in_specs=[spec, spec],
        out_specs=spec,
        out_shape=jax.ShapeDtypeStruct((n, PEER_TK), jnp.int32),
        compiler_params=_cparams(("parallel",)),
        name="peer_gelu",
    )(gate, act)


def _peer_residual_kernel(x_ref, y_ref, mod_ref, o_ref):
    o_ref[...] = x_ref[...] + mod_ref[0, :, 5 * D_MODEL:6 * D_MODEL] * y_ref[...]


def _peer_residual(x, y, mod, seq_len, tm):
    n = x.shape[0]
    per_seq = seq_len // tm
    n_cond = mod.shape[0]
    cond_of = (lambda i: (0, 0, 0)) if n_cond == 1 else (lambda i: (i // per_seq, 0, 0))
    row = pl.BlockSpec((tm, D_MODEL), lambda i: (i, 0))
    return pl.pallas_call(
        _peer_residual_kernel,
        grid=(n // tm,),
        in_specs=[row, row, pl.BlockSpec((1, 1, 6 * D_MODEL), cond_of)],
        out_specs=row,
        out_shape=jax.ShapeDtypeStruct((n, D_MODEL), F32),
        compiler_params=_cparams(("parallel",)),
        name="peer_residual",
    )(x, y, mod)


def _final_norm_kernel(x_ref, g_ref, o_ref):
    o_ref[...] = _rms_rows(x_ref[...], g_ref[...])


def _final_norm(x, g, tm):
    n = x.shape[0]
    return pl.pallas_call(
        _final_norm_kernel,
        grid=(n // tm,),
        in_specs=[pl.BlockSpec((tm, D_MODEL), lambda i: (i, 0)), pl.BlockSpec((1, D_MODEL), lambda i: (0, 0))],
        out_specs=pl.BlockSpec((tm, D_MODEL), lambda i: (i, 0)),
        out_shape=jax.ShapeDtypeStruct((n, D_MODEL), F32),
        compiler_params=_cparams(("parallel",)),
        name="final_norm",
    )(x, g)


def _rot_half_columns(w):
    half = ROPE_AXIS_DIM // 2
    w4 = w.reshape(w.shape[:-1] + (2, 2, half))
    return jnp.stack([-w4[..., 1, :], w4[..., 0, :]], axis=-2).reshape(w.shape)


def _rope_tables(seq_len, rotate):
    cos = np.zeros((seq_len, LANES), np.float32)
    sin = np.zeros((seq_len, LANES), np.float32)
    cos[:, :MLA_NOPE + MLA_ROPE] = 1.0
    if rotate:
        pos = np.arange(seq_len)
        half = ROPE_AXIS_DIM // 2
        inv = (1.0 / (ROPE_BASE ** (np.arange(0, ROPE_AXIS_DIM, 2, dtype=np.float32) / ROPE_AXIS_DIM))).astype(np.float32)
        for axis, p in enumerate(((pos // GRID_W).astype(np.float32), (pos % GRID_W).astype(np.float32))):
            ang = p[:, None] * inv[None, :]
            for part in range(2):
                lo = MLA_NOPE + axis * ROPE_AXIS_DIM + part * half
                cos[:, lo:lo + half] = np.cos(ang)
                sin[:, lo:lo + half] = np.sin(ang)
    return jnp.asarray(cos), jnp.asarray(sin)


def _prep_layer_weights(l, w_in, mla_wqb, mla_wkvb, w_out, peer_wq):
    wi = w_in[l]
    kpe_w = wi[:, IN_KPE0:IN_KPE0 + MLA_ROPE]
    w_pad = jnp.concatenate(
        [wi[:, :IN_KPE0], jnp.zeros((D_MODEL, MLA_NOPE), F32), kpe_w, _rot_half_columns(kpe_w)], axis=1).astype(BF16)
    wq = mla_wqb[l].reshape(MLA_Q_LORA, MLA_HEADS, MLA_NOPE + MLA_ROPE)
    wq_pe = wq[..., MLA_NOPE:]
    wq128 = jnp.concatenate([wq, _rot_half_columns(wq_pe)], axis=-1).transpose(1, 0, 2).astype(BF16)
    wkv128 = mla_wkvb[l].reshape(MLA_KV_LORA, MLA_HEADS, MLA_NOPE + MLA_V).transpose(1, 0, 2).astype(BF16)
    return w_pad, wq128, wkv128, w_out[l].astype(BF16), peer_wq[l].astype(BF16)


TOK_TILE = 256


def _mixer_stage(x, res, mod, l, W, P, lb, batch, seq_len, ctx, rope):
    w_pad, wq128, wkv128, wo, _ = W
    tm = TOK_TILE
    y, mod_prev = (jnp.zeros_like(x), mod) if res is None else res
    x, hg, cqn, ckvn, kpe128 = _inproj(x, y, mod_prev, P['norm_mix'][l][None], mod, w_pad, P['mla_q_norm'][l][None],
                                       P['mla_kv_norm'][l][None], seq_len, tm)
    if ctx is None:
        s0 = jnp.zeros((batch, 2, HG_HEADS, HG_D, HG_D), F32)
    else:
        s0 = ctx[2]
    gn2 = jnp.tile(P['hgrn_norm'][l], 2)[None]
    hg_out, s_fin = _hgrn(hg, lb[0, l][None], lb[1, l][None], gn2, s0, batch, seq_len)

    cos, sin = rope
    q128 = _qproj(cqn, wq128, cos, sin, tm)
    k128, kv128 = _kvproj(ckvn, kpe128, wkv128, cos, sin, tm)
    lk = seq_len
    if ctx is not None:
        past = ctx[0].shape[1]
        cos_c, sin_c = _rope_tables(past, False)
        kpe_c = jnp.pad(ctx[1].reshape(batch * past, MLA_ROPE), ((0, 0), (MLA_NOPE, LANES - MLA_NOPE - MLA_ROPE)))
        k_c, kv_c = _kvproj(ctx[0].reshape(batch * past, MLA_KV_LORA), kpe_c, wkv128, cos_c, sin_c, past)
        join = lambda a, b: jnp.concatenate(
            [a.reshape(MLA_HEADS, batch, past, LANES), b.reshape(MLA_HEADS, batch, seq_len, LANES)],
            axis=2).reshape(MLA_HEADS, batch * (past + seq_len), LANES)
        k128, kv128 = join(k_c, k128), join(kv_c, kv128)
        lk = past + seq_len
    attn = _attention(q128, k128, kv128, batch, seq_len, lk, tm)
    x = _outproj(x, hg_out, attn, wo, mod, seq_len, tm)
    return x, (ckvn, kpe128[:, MLA_NOPE:MLA_NOPE + MLA_ROPE], s_fin)


def _retrieve_stage(x, mod, l, W, P, seq_len, after):
    keys = P['peer_keys'][l]
    return _peerq(x, P['norm_ffn'][l][None], mod, W[4], keys[0], keys[1], seq_len, TOK_TILE, after)


def kernel(x_prompt, x_sample, cache_ckv, cache_kpe, state_hgrn, c, c_ctx, w_ada, b_ada, norm_mix, w_in,
           hgrn_lb_logits, hgrn_norm, mla_q_norm, mla_wqb, mla_kv_norm, mla_wkvb, w_out, norm_ffn,
           peer_wq, peer_keys, peer_u, peer_v, final_norm):
    P = {'norm_mix': norm_mix, 'hgrn_norm': hgrn_norm, 'mla_q_norm': mla_q_norm, 'mla_kv_norm': mla_kv_norm,
         'norm_ffn': norm_ffn, 'peer_keys': peer_keys}
    depth = w_in.shape[0]
    for batch, seq_len in (x_prompt.shape[:2], x_sample.shape[:2]):
        assert seq_len % TOK_TILE == 0 and seq_len % (HG_CHUNK * HG_UNROLL) == 0, seq_len
        assert (batch * seq_len) % (2 * SC_TILES) == 0, (batch, seq_len)
    assert x_sample.shape[1] % GRID_W == 0 and peer_u.shape[1] == PEER_NKEYS * PEER_NKEYS
    u_packed = [_pack_bf16_pairs(peer_u, l) for l in range(depth)]
    v_packed = [_pack_bf16_pairs(peer_v, l) for l in range(depth)]
    bp, lp, _ = x_prompt.shape
    bs, ls, _ = x_sample.shape

    p = jax.nn.softmax(hgrn_lb_logits.astype(F32), axis=1)
    lb = jnp.cumsum(p, axis=1) - p[:, :1]

    cond8 = jnp.zeros((8, D_MODEL), F32).at[0].set(c_ctx).at[1:1 + bs].set(c)
    mod_all = [_ada_mod(cond8, w_ada, b_ada, l) for l in range(depth)]

    rope_p = _rope_tables(lp, False)
    rope_s = _rope_tables(ls, True)

    ckvs, kpes, states = [], [], []
    Ws = [_prep_layer_weights(l, w_in, mla_wqb, mla_wkvb, w_out, peer_wq) for l in range(depth)]
    mods_p = [mod_all[l][0:1][:, None, :] for l in range(depth)]
    mods_s = [mod_all[l][1:1 + bs][:, None, :] for l in range(depth)]

    def mixer_s(x, res, l):
        return _mixer_stage(x, res, mods_s[l], l, Ws[l], P, lb, bs, ls,
                            (cache_ckv[:, l], cache_kpe[:, l], state_hgrn[:, l]), rope_s)[0]

    def mixer_p(x, res, l):
        x, (ckv_l, kpe_l, st_l) = _mixer_stage(x, res, mods_p[l], l, Ws[l], P, lb, bp, lp, None, rope_p)
        ckvs.append(ckv_l.reshape(bp, lp, MLA_KV_LORA))
        kpes.append(kpe_l.reshape(bp, lp, MLA_ROPE))
        states.append(st_l)
        return x

    tm = TOK_TILE
    mixers, mods, lens = (mixer_p, mixer_s), (mods_p, mods_s), (lp, ls)
    x = [x_prompt.reshape(bp * lp, D_MODEL), x_sample.reshape(bs * ls, D_MODEL)]
    none_yet = jnp.zeros((8, D_MODEL), F32)

    def retrieve(g, l, after):
        return _retrieve_stage(x[g], mods[g][l], l, Ws[l], P, lens[g], after)

    res = [None, None]
    x[0] = mixers[0](x[0], None, 0)
    h2, idx, gate = retrieve(0, 0, jnp.zeros((8, PEER_TK), jnp.int32))
    q = [(idx, gate, _peer_u(h2, idx, u_packed[0], none_yet)), None]
    for l in range(depth):
        last = l + 1 == depth
        x[1] = mixers[1](x[1], res[1], l)
        idx, gate, act = q[0]
        w0 = _peer_gelu(gate, act, tm)
        y0 = _peer_v(w0, idx, v_packed[l], x[1])
        h2, idx, gate = retrieve(1, l, w0)
        q[1] = (idx, gate, _peer_u(h2, idx, u_packed[l], y0))
        res[0] = (y0, mods[0][l])
        if not last:
            x[0] = mixers[0](x[0], res[0], l + 1)
        idx, gate, act = q[1]
        w1 = _peer_gelu(gate, act, tm)
        y1 = _peer_v(w1, idx, v_packed[l], none_yet if last else x[0])
        if not last:
            h2, idx, gate = retrieve(0, l + 1, w1)
            q[0] = (idx, gate, _peer_u(h2, idx, u_packed[l + 1], y1))
        res[1] = (y1, mods[1][l])
    x = [_peer_residual(x[g], res[g][0], res[g][1], lens[g], tm) for g in range(2)]

    y_prompt = _final_norm(x[0], final_norm[None], tm).reshape(bp, lp, D_MODEL)
    y_sample = _final_norm(x[1], final_norm[None], tm).reshape(bs, ls, D_MODEL)
    return (y_prompt, y_sample, jnp.stack(ckvs, axis=1), jnp.stack(kpes, axis=1), jnp.stack(states, axis=1))
```
